```python
import math
import jax
import jax.numpy as jnp
from jax import lax
import numpy as np

D_MODEL = 1024
BATCH = 4
SEQ = 4096
DEPTH = 4
DEC_BATCH = 128
DEC_SEQ = 1
PAST_LEN = 8192
PAGE_SIZE = 128

HEAD_DIM = 64
A_HEADS = 8
A_KV_HEADS = 2
A_GROUP = A_HEADS // A_KV_HEADS
WINDOW = 128
B_WIDTH = 512
CONV_W = 3
C_HEADS = 16
C_KV_HEADS = 4
C_GROUP = C_HEADS // C_KV_HEADS
Q_BLOCK = 128
FORGET_BIAS_MEAN = 3.0
N_BUCKETS = 32
MAX_DISTANCE = 128
N_GROUPS = 4
EXPERTS_PER_GROUP = 4
N_EXPERTS = N_GROUPS * EXPERTS_PER_GROUP
TOP_K = 2
D_EXPERT = 256
N_EVEN = (DEPTH + 1) // 2
N_ODD = DEPTH // 2
ALPHA = (2.0 * DEPTH) ** 0.25
BETA = (8.0 * DEPTH) ** -0.25
LN_EPS = 1e-5
NEG_INF = -1e30
A_Q = A_HEADS * HEAD_DIM
A_KV = A_KV_HEADS * HEAD_DIM
EVEN_IN = A_Q + 2 * A_KV + 3 * B_WIDTH
EVEN_MIX = A_Q + B_WIDTH
C_Q = C_HEADS * HEAD_DIM
C_KV = C_KV_HEADS * HEAD_DIM
ODD_IN = C_Q + 2 * C_KV + C_HEADS

kernel_name = 'hybrid_swa_conv_fox_hmoe_step'


def layer_norm(x, g, b):
    x32 = x.astype(jnp.float32)
    mu = x32.mean(-1, keepdims=True)
    var = jnp.square(x32 - mu).mean(-1, keepdims=True)
    y = (x32 - mu) * lax.rsqrt(var + LN_EPS) * g.astype(jnp.float32) + b.astype(jnp.float32)
    return y.astype(x.dtype)


def t5_bucket(dist):
    n = jnp.maximum(dist, 0)
    max_exact = N_BUCKETS // 2
    nf = jnp.maximum(n, 1).astype(jnp.float32)
    large = max_exact + (jnp.log(nf / max_exact) / math.log(MAX_DISTANCE / max_exact)
                         * (N_BUCKETS - max_exact)).astype(jnp.int32)
    large = jnp.minimum(large, N_BUCKETS - 1)
    return jnp.where(n < max_exact, n, large)


def t5_bias(table, dist):
    bias = table.astype(jnp.float32)[t5_bucket(dist)]
    return jnp.moveaxis(bias, -1, 0).reshape(A_KV_HEADS, A_GROUP, *dist.shape)


def sink_attend(q, k, v, bias, mask, sink):
    s = jnp.einsum('...qhgd,...shd->...hgqs', q, k).astype(jnp.float32) * (HEAD_DIM ** -0.5) + bias
    s = jnp.where(mask, s, NEG_INF)
    sk = sink.astype(jnp.float32).reshape(A_KV_HEADS, A_GROUP, 1, 1)
    m = jnp.maximum(s.max(-1, keepdims=True), sk)
    p = jnp.exp(s - m)
    p = p / (p.sum(-1, keepdims=True) + jnp.exp(sk - m))
    return jnp.einsum('...hgqs,...shd->...qhgd', p.astype(v.dtype), v)


def swa_prompt(q, k, v, table, sink):
    n_, s_ = q.shape[0], q.shape[1]
    nblk = s_ // WINDOW
    qb = q.reshape(n_, nblk, WINDOW, A_KV_HEADS, A_GROUP, HEAD_DIM)
    kb = k.reshape(n_, nblk, WINDOW, A_KV_HEADS, HEAD_DIM)
    vb = v.reshape(n_, nblk, WINDOW, A_KV_HEADS, HEAD_DIM)
    def with_prev(t):
        prev = jnp.concatenate([jnp.zeros_like(t[:, :1]), t[:, :-1]], axis=1)
        return jnp.concatenate([prev, t], axis=2)
    qi = jnp.arange(WINDOW)[:, None]
    kj = jnp.arange(2 * WINDOW)[None, :]
    dist = WINDOW + qi - kj
    band = (dist >= 0) & (dist < WINDOW)
    has_prev = (jnp.arange(nblk) > 0)[:, None, None] | (kj >= WINDOW)[None]
    mask = (band[None] & has_prev)[:, None, None]
    out = sink_attend(qb, with_prev(kb), with_prev(vb), t5_bias(table, dist), mask, sink)
    return out.reshape(n_, s_, A_Q)


def swa_sample(q, k, v, k_buf, v_buf, table, sink, past_len):
    t_ = q.shape[1]
    wb = k_buf.shape[1]
    kk = jnp.concatenate([k_buf.astype(k.dtype), k], axis=1)
    vv = jnp.concatenate([v_buf.astype(v.dtype), v], axis=1)
    qpos = past_len + jnp.arange(t_)
    kpos = past_len - wb + jnp.arange(wb + t_)
    dist = qpos[:, None] - kpos[None, :]
    mask = (dist >= 0) & (dist < WINDOW)
    out = sink_attend(q, kk, vv, t5_bias(table, dist), mask, sink)
    return out.reshape(q.shape[0], t_, A_Q), kk[:, -wb:], vv[:, -wb:]


def short_conv(b_gate, c_gate, x_in, w, prev):
    u = c_gate * x_in
    up = jnp.concatenate([prev.astype(u.dtype), u], axis=1)
    t_ = u.shape[1]
    z = sum(w[i] * up[:, i:i + t_] for i in range(CONV_W))
    return b_gate * z, up[:, -(CONV_W - 1):]


def cum_log_forget(logf):
    c = jnp.cumsum(logf, axis=1)
    n_, t_ = c.shape[:2]
    return jnp.moveaxis(c, 2, 1).reshape(n_, C_KV_HEADS, C_GROUP, t_)


def fox_attend(q, k, v, cq, ck, mask):
    s = jnp.einsum('bqhgd,bshd->bhgqs', q, k).astype(jnp.float32) * (HEAD_DIM ** -0.5)
    s = s + cq[..., :, None] - ck[..., None, :]
    s = jnp.where(mask, s, NEG_INF)
    p = jax.nn.softmax(s, axis=-1)
    return jnp.einsum('bhgqs,bshd->bqhgd', p.astype(v.dtype), v)


def fox_prompt(q, k, v, logf):
    b_, s_ = q.shape[:2]
    c = cum_log_forget(logf)
    kpos = jnp.arange(s_)
    def block(i):
        start = i * Q_BLOCK
        qb = lax.dynamic_slice_in_dim(q, start, Q_BLOCK, axis=1)
        cq = lax.dynamic_slice_in_dim(c, start, Q_BLOCK, axis=3)
        mask = kpos[None, :] <= (start + jnp.arange(Q_BLOCK))[:, None]
        return fox_attend(qb, k, v, cq, c, mask)
    out = lax.map(block, jnp.arange(s_ // Q_BLOCK))
    return jnp.moveaxis(out, 0, 1).reshape(b_, s_, C_Q)


def fox_sample(q, k, v, logf, k_cache, v_cache, lf_cache, page_table):
    b_, t_ = q.shape[:2]
    past = page_table.shape[1] * PAGE_SIZE
    k_past = k_cache[page_table].reshape(b_, past, C_KV_HEADS, HEAD_DIM)
    v_past = v_cache[page_table].reshape(b_, past, C_KV_HEADS, HEAD_DIM)
    lf_past = lf_cache[page_table].reshape(b_, past, C_HEADS).astype(jnp.float32)
    kk = jnp.concatenate([k_past.astype(k.dtype), k], axis=1)
    vv = jnp.concatenate([v_past.astype(v.dtype), v], axis=1)
    c = cum_log_forget(jnp.concatenate([lf_past, logf], axis=1))
    mask = jnp.arange(past + t_)[None, :] <= (past + jnp.arange(t_))[:, None]
    out = fox_attend(q, kk, vv, c[..., past:], c, mask)
    return out.reshape(b_, t_, C_Q)


def even_proj(x, w_in):
    n_, t_ = x.shape[:2]
    h = jnp.einsum('ntd,de->nte', x, w_in)
    cuts = [A_Q, A_Q + A_KV, A_Q + 2 * A_KV, A_Q + 2 * A_KV + B_WIDTH, A_Q + 2 * A_KV + 2 * B_WIDTH]
    q, k, v, bg, cg, xin = jnp.split(h, cuts, axis=-1)
    return (q.reshape(n_, t_, A_KV_HEADS, A_GROUP, HEAD_DIM),
            k.reshape(n_, t_, A_KV_HEADS, HEAD_DIM),
            v.reshape(n_, t_, A_KV_HEADS, HEAD_DIM), bg, cg, xin)


def even_out(att, conv, w_out):
    return jnp.einsum('ntc,cd->ntd', jnp.concatenate([att, conv], axis=-1), w_out)


def odd_proj(x, w_in, b_f):
    n_, t_ = x.shape[:2]
    h = jnp.einsum('ntd,de->nte', x, w_in)
    q, k, v, fl = jnp.split(h, [C_Q, C_Q + C_KV, C_Q + 2 * C_KV], axis=-1)
    logf = jax.nn.log_sigmoid(fl.astype(jnp.float32) + b_f.astype(jnp.float32))
    return (q.reshape(n_, t_, C_KV_HEADS, C_GROUP, HEAD_DIM),
            k.reshape(n_, t_, C_KV_HEADS, HEAD_DIM),
            v.reshape(n_, t_, C_KV_HEADS, HEAD_DIM), logf)


def hier_moe(x, w_grp, b_grp, w_rt, b_rt, w_gate, w_up, w_down):
    shp = x.shape
    xt = x.reshape(-1, shp[-1])
    x32 = xt.astype(jnp.float32)
    grp_prob = jax.nn.softmax(x32 @ w_grp.astype(jnp.float32) + b_grp.astype(jnp.float32), axis=-1)
    grp_w, grp_idx = lax.top_k(grp_prob, 1)
    exp_logits = (x32 @ w_rt.astype(jnp.float32) + b_rt.astype(jnp.float32)).reshape(-1, N_GROUPS, EXPERTS_PER_GROUP)
    in_grp = jnp.einsum('nge,ng->ne', exp_logits, jax.nn.one_hot(grp_idx[:, 0], N_GROUPS, dtype=jnp.float32))
    top_logit, top_idx = lax.top_k(in_grp, TOP_K)
    gate = jax.nn.softmax(top_logit, axis=-1) * grp_w
    expert_id = grp_idx * EXPERTS_PER_GROUP + top_idx
    combine = jnp.einsum('nk,nke->ne', gate, jax.nn.one_hot(expert_id, N_EXPERTS, dtype=jnp.float32))
    h = jax.nn.silu(jnp.einsum('nd,edf->nef', xt, w_gate)) * jnp.einsum('nd,edf->nef', xt, w_up)
    h = h * combine.astype(h.dtype)[:, :, None]
    return jnp.einsum('nef,efd->nd', h, w_down).reshape(shp)


def post_layer(x, mix, layer, ln_g, ln_b, w_group, b_group, w_router, b_router, w_gate, w_up, w_down):
    x = layer_norm(ALPHA * x + mix, ln_g[layer, 0], ln_b[layer, 0])
    moe = hier_moe(x, w_group[layer], b_group[layer], w_router[layer], b_router[layer],
                   w_gate[layer], w_up[layer], w_down[layer])
    return layer_norm(ALPHA * x + moe, ln_g[layer, 1], ln_b[layer, 1])


def setup_inputs(seed: int = 0) -> dict:
    key = jax.random.key(seed)
    ks = iter(jax.random.split(key, 32))
    def nrm(shape, scale):
        return jax.random.normal(next(ks), shape, jnp.float32) * scale
    n_pages = PAST_LEN // PAGE_SIZE
    n_used = DEC_BATCH * n_pages
    n_pool = n_used + max(1, n_used // 4)
    w_buf = min(WINDOW, PAST_LEN)
    x_prompt = nrm((BATCH, SEQ, D_MODEL), 1.0)
    x_sample = nrm((DEC_BATCH, DEC_SEQ, D_MODEL), 1.0)
    cache_swa_k = nrm((N_EVEN, DEC_BATCH, w_buf, A_KV_HEADS, HEAD_DIM), 1.0)
    cache_swa_v = nrm((N_EVEN, DEC_BATCH, w_buf, A_KV_HEADS, HEAD_DIM), 1.0)
    state_conv = nrm((N_EVEN, DEC_BATCH, CONV_W - 1, B_WIDTH), 1.0)
    cache_fox_k = nrm((N_ODD, n_pool, PAGE_SIZE, C_KV_HEADS, HEAD_DIM), 1.0)
    cache_fox_v = nrm((N_ODD, n_pool, PAGE_SIZE, C_KV_HEADS, HEAD_DIM), 1.0)
    cache_fox_logf = jax.nn.log_sigmoid(FORGET_BIAS_MEAN + nrm((N_ODD, n_pool, PAGE_SIZE, C_HEADS), 1.0))
    page_table = jax.random.permutation(next(ks), n_pool)[:n_used].reshape(DEC_BATCH, n_pages).astype(jnp.int32)
    return {
        'x_prompt': x_prompt,
        'x_sample': x_sample,
        'cache_swa_k': cache_swa_k,
        'cache_swa_v': cache_swa_v,
        'state_conv': state_conv,
        'cache_fox_k': cache_fox_k,
        'cache_fox_v': cache_fox_v,
        'cache_fox_logf': cache_fox_logf,
        'page_table': page_table,
        'rel_bias_table': nrm((N_BUCKETS, A_HEADS), 0.5),
        'attn_sinks': nrm((N_EVEN, A_HEADS), 0.5),
        'w_in_even': nrm((N_EVEN, D_MODEL, EVEN_IN), D_MODEL ** -0.5),
        'conv_w': nrm((N_EVEN, CONV_W, B_WIDTH), CONV_W ** -0.5),
        'w_out_even': nrm((N_EVEN, EVEN_MIX, D_MODEL), BETA * EVEN_MIX ** -0.5),
        'w_in_odd': nrm((N_ODD, D_MODEL, ODD_IN), D_MODEL ** -0.5),
        'b_forget': FORGET_BIAS_MEAN + nrm((N_ODD, C_HEADS), 0.1),
        'w_out_odd': nrm((N_ODD, C_Q, D_MODEL), BETA * C_Q ** -0.5),
        'ln_g': 1.0 + nrm((DEPTH, 2, D_MODEL), 0.02),
        'ln_b': nrm((DEPTH, 2, D_MODEL), 0.02),
        'w_group': nrm((DEPTH, D_MODEL, N_GROUPS), D_MODEL ** -0.5),
        'b_group': nrm((DEPTH, N_GROUPS), 0.01),
        'w_router': nrm((DEPTH, D_MODEL, N_EXPERTS), D_MODEL ** -0.5),
        'b_router': nrm((DEPTH, N_EXPERTS), 0.01),
        'w_gate': nrm((DEPTH, N_EXPERTS, D_MODEL, D_EXPERT), D_MODEL ** -0.5),
        'w_up': nrm((DEPTH, N_EXPERTS, D_MODEL, D_EXPERT), D_MODEL ** -0.5),
        'w_down': nrm((DEPTH, N_EXPERTS, D_EXPERT, D_MODEL), BETA * D_EXPERT ** -0.5),
    }


def reference(x_prompt, x_sample, cache_swa_k, cache_swa_v, state_conv, cache_fox_k, cache_fox_v,
              cache_fox_logf, page_table, rel_bias_table, attn_sinks, w_in_even, conv_w, w_out_even,
              w_in_odd, b_forget, w_out_odd, ln_g, ln_b, w_group, b_group, w_router, b_router,
              w_gate, w_up, w_down):
    past_len = page_table.shape[1] * PAGE_SIZE
    xp, xs = x_prompt, x_sample
    swa_kp, swa_vp, swa_ks, swa_vs, conv_p, conv_s = [], [], [], [], [], []
    fox_kp, fox_vp, fox_lp, fox_ks, fox_vs, fox_ls = [], [], [], [], [], []
    for layer in range(DEPTH):
        if layer % 2 == 0:
            e = layer // 2
            q, k, v, bg, cg, xin = even_proj(xp, w_in_even[e])
            att = swa_prompt(q, k, v, rel_bias_table, attn_sinks[e])
            zero_buf = jnp.zeros((xp.shape[0], CONV_W - 1, B_WIDTH), xin.dtype)
            cv, buf = short_conv(bg, cg, xin, conv_w[e], zero_buf)
            mix_p = even_out(att, cv, w_out_even[e])
            keep = min(WINDOW, xp.shape[1])
            swa_kp.append(k[:, -keep:])
            swa_vp.append(v[:, -keep:])
            conv_p.append(buf)
            q, k, v, bg, cg, xin = even_proj(xs, w_in_even[e])
            att, kb, vb = swa_sample(q, k, v, cache_swa_k[e], cache_swa_v[e], rel_bias_table, attn_sinks[e], past_len)
            cv, buf = short_conv(bg, cg, xin, conv_w[e], state_conv[e])
            mix_s = even_out(att, cv, w_out_even[e])
            swa_ks.append(kb)
            swa_vs.append(vb)
            conv_s.append(buf)
        else:
            o = layer // 2
            q, k, v, lf = odd_proj(xp, w_in_odd[o], b_forget[o])
            mix_p = jnp.einsum('ntc,cd->ntd', fox_prompt(q, k, v, lf), w_out_odd[o])
            fox_kp.append(k)
            fox_vp.append(v)
            fox_lp.append(lf)
            q, k, v, lf = odd_proj(xs, w_in_odd[o], b_forget[o])
            att = fox_sample(q, k, v, lf, cache_fox_k[o], cache_fox_v[o], cache_fox_logf[o], page_table)
            mix_s = jnp.einsum('ntc,cd->ntd', att, w_out_odd[o])
            fox_ks.append(k)
            fox_vs.append(v)
            fox_ls.append(lf)
        xp = post_layer(xp, mix_p, layer, ln_g, ln_b, w_group, b_group, w_router, b_router, w_gate, w_up, w_down)
        xs = post_layer(xs, mix_s, layer, ln_g, ln_b, w_group, b_group, w_router, b_router, w_gate, w_up, w_down)
    y_prompt, y_sample = xp, xs
    new_swa_k_prompt = jnp.stack(swa_kp)
    new_swa_v_prompt = jnp.stack(swa_vp)
    new_swa_k_sample = jnp.stack(swa_ks)
    new_swa_v_sample = jnp.stack(swa_vs)
    new_conv_prompt = jnp.stack(conv_p)
    new_conv_sample = jnp.stack(conv_s)
    new_fox_k_prompt = jnp.stack(fox_kp)
    new_fox_v_prompt = jnp.stack(fox_vp)
    new_fox_logf_prompt = jnp.stack(fox_lp)
    new_fox_k_sample = jnp.stack(fox_ks)
    new_fox_v_sample = jnp.stack(fox_vs)
    new_fox_logf_sample = jnp.stack(fox_ls)
    return (y_prompt, y_sample, new_swa_k_prompt, new_swa_v_prompt, new_swa_k_sample, new_swa_v_sample,
            new_conv_prompt, new_conv_sample, new_fox_k_prompt, new_fox_v_prompt, new_fox_logf_prompt,
            new_fox_k_sample, new_fox_v_sample, new_fox_logf_sample)
```

```python
import functools
import math

import jax
import jax.numpy as jnp
from jax import lax
from jax.experimental import pallas as pl
from jax.experimental.pallas import tpu as pltpu

F32 = jnp.float32
BF16 = jnp.bfloat16

D_MODEL = 1024
BATCH = 4
SEQ = 4096
DEC_BATCH = 128
PAGE = 128
N_PAGES = 64
HEAD_DIM = 64
A_HEADS = 8
A_KV = 2
WINDOW = 128
B_WIDTH = 512
C_HEADS = 16
C_KV = 4
N_BUCKETS = 32
MAX_DISTANCE = 128
N_GROUPS = 4
N_EXPERTS = 16
D_EXPERT = 256
DEPTH = 4
ALPHA = (2.0 * DEPTH) ** 0.25
LN_EPS = 1e-5
NEG_INF = -1e30
SCALE = HEAD_DIM ** -0.5

N_PROMPT = BATCH * SEQ
TM = 512
N_PTILES = N_PROMPT // TM
TILES_PER_SEQ = SEQ // TM
N_TILES = N_PTILES + 1
N_TOK = N_TILES * TM
LANES = 128

TE = 256
N_SLOTS = 2 * N_TOK
N_ETILES = N_SLOTS // TE + N_EXPERTS
N_POS = N_ETILES * TE

VMEM_LIMIT = 56 * 1024 * 1024


def _cparams(*sem):
    return pltpu.CompilerParams(dimension_semantics=sem, vmem_limit_bytes=VMEM_LIMIT)


def _layer_norm(y, g, b):
    mu = jnp.mean(y, axis=-1, keepdims=True)
    yc = y - mu
    var = jnp.mean(yc * yc, axis=-1, keepdims=True)
    return yc * lax.rsqrt(var + LN_EPS) * g + b


def _even_in_kernel(x_ref, w_ref, cw_ref, s0_ref, s1_ref,
                    q_ref, k_ref, v_ref, cv_ref, convp_ref, us_ref, carry_ref):
    i = pl.program_id(0)
    h = jnp.dot(x_ref[...].astype(BF16), w_ref[...], preferred_element_type=F32)
    q_ref[...] = h[:, 0:512]
    k_ref[...] = h[:, 512:640]
    v_ref[...] = h[:, 640:768]
    bg = h[:, 768:1280]
    u = h[:, 1280:1792] * h[:, 1792:2304]
    w0 = cw_ref[0:1, :]
    w1 = cw_ref[1:2, :]
    w2 = cw_ref[2:3, :]

    @pl.when(i < N_PTILES)
    def _():
        @pl.when(i % TILES_PER_SEQ == 0)
        def _():
            carry_ref[...] = jnp.zeros_like(carry_ref)
        row = lax.broadcasted_iota(jnp.int32, (TM, B_WIDTH), 0)
        c2 = carry_ref[0:1, :]
        c1 = carry_ref[1:2, :]
        u1 = jnp.where(row == 0, c1, pltpu.roll(u, 1, 0))
        u2 = jnp.where(row == 0, c2, jnp.where(row == 1, c1, pltpu.roll(u, 2, 0)))
        cv_ref[...] = bg * (w0 * u2 + w1 * u1 + w2 * u)
        carry_ref[0:2, :] = u[TM - 2:TM, :]
        convp_ref[...] = u[TM - 2:TM, :]

    @pl.when(i == N_PTILES)
    def _():
        us = u[0:DEC_BATCH, :]
        z = w0 * s0_ref[...] + w1 * s1_ref[...] + w2 * us
        cv_ref[0:DEC_BATCH, :] = bg[0:DEC_BATCH, :] * z
        cv_ref[DEC_BATCH:TM, :] = jnp.zeros((TM - DEC_BATCH, B_WIDTH), F32)
        us_ref[...] = us


def _even_in(x, w_bf, cw, s0, s1):
    tile = lambda n: pl.BlockSpec((TM, n), lambda i: (i, 0))
    full = lambda shape: pl.BlockSpec(shape, lambda i: (0,) * len(shape))
    return pl.pallas_call(
        _even_in_kernel,
        grid=(N_TILES,),
        in_specs=[tile(D_MODEL), full((D_MODEL, 2304)), full((3, B_WIDTH)),
                  full((DEC_BATCH, B_WIDTH)), full((DEC_BATCH, B_WIDTH))],
        out_specs=[tile(512), tile(128), tile(128), tile(B_WIDTH),
                   pl.BlockSpec((None, 2, B_WIDTH),
                                lambda i: (jnp.minimum(i // TILES_PER_SEQ, BATCH - 1), 0, 0)),
                   full((DEC_BATCH, B_WIDTH))],
        out_shape=[jax.ShapeDtypeStruct((N_TOK, 512), F32),
                   jax.ShapeDtypeStruct((N_TOK, 128), F32),
                   jax.ShapeDtypeStruct((N_TOK, 128), F32),
                   jax.ShapeDtypeStruct((N_TOK, B_WIDTH), F32),
                   jax.ShapeDtypeStruct((BATCH, 2, B_WIDTH), F32),
                   jax.ShapeDtypeStruct((DEC_BATCH, B_WIDTH), F32)],
        scratch_shapes=[pltpu.VMEM((8, B_WIDTH), F32)],
        compiler_params=_cparams("arbitrary"),
        name="even_in",
    )(x, w_bf, cw, s0, s1)


def _swa_prompt_kernel(q_ref, kc_ref, kp_ref, vc_ref, vp_ref, bias_ref, sink_ref, o_ref):
    j = pl.program_id(1)
    qi = lax.broadcasted_iota(jnp.int32, (WINDOW, 2 * WINDOW), 0)
    kj = lax.broadcasted_iota(jnp.int32, (WINDOW, 2 * WINDOW), 1)
    dist = WINDOW + qi - kj
    valid = (dist >= 0) & (dist < WINDOW) & ((kj >= WINDOW) | (j > 0))
    for h in range(A_HEADS):
        kv = h // (A_HEADS // A_KV)
        cs = slice(kv * HEAD_DIM, (kv + 1) * HEAD_DIM)
        qh = q_ref[:, h * HEAD_DIM:(h + 1) * HEAD_DIM].astype(BF16)
        kk = jnp.concatenate([kp_ref[:, cs], kc_ref[:, cs]], axis=0).astype(BF16)
        vv = jnp.concatenate([vp_ref[:, cs], vc_ref[:, cs]], axis=0).astype(BF16)
        s = lax.dot_general(qh, kk, (((1,), (1,)), ((), ())), preferred_element_type=F32)
        s = s * SCALE + bias_ref[h]
        s = jnp.where(valid, s, NEG_INF)
        sk = sink_ref[h:h + 1, 0:1]
        m = jnp.maximum(jnp.max(s, axis=-1, keepdims=True), sk)
        p = jnp.exp(s - m)
        den = jnp.sum(p, axis=-1, keepdims=True) + jnp.exp(sk - m)
        o = jnp.dot(p.astype(BF16), vv, preferred_element_type=F32)
        o_ref[:, h * HEAD_DIM:(h + 1) * HEAD_DIM] = o / den


def _swa_prompt(q, k, v, bias, sink_b):
    nblk = SEQ // WINDOW
    cur = lambda b, j: (b * nblk + j, 0)
    prev = lambda b, j: (jnp.maximum(b * nblk + j - 1, 0), 0)
    return pl.pallas_call(
        _swa_prompt_kernel,
        grid=(BATCH, nblk),
        in_specs=[pl.BlockSpec((WINDOW, 512), cur),
                  pl.BlockSpec((WINDOW, 128), cur), pl.BlockSpec((WINDOW, 128), prev),
                  pl.BlockSpec((WINDOW, 128), cur), pl.BlockSpec((WINDOW, 128), prev),
                  pl.BlockSpec((A_HEADS, WINDOW, 2 * WINDOW), lambda b, j: (0, 0, 0)),
                  pl.BlockSpec((A_HEADS, LANES), lambda b, j: (0, 0))],
        out_specs=pl.BlockSpec((WINDOW, 512), cur),
        out_shape=jax.ShapeDtypeStruct((N_TOK, 512), F32),
        input_output_aliases={},
        compiler_params=_cparams("arbitrary", "arbitrary"),
        name="swa_prompt",
    )(q, k, k, v, v, bias, sink_b)


SWA_RB = 8


def _swa_sample_kernel(att_in_ref, q_ref, k_ref, v_ref, kt_ref, vt_ref, bias_ref, bnew_ref, sink_ref, o_ref):
    del att_in_ref

    @pl.when(pl.program_id(0) >= DEC_BATCH // SWA_RB)
    def _():
        o_ref[...] = jnp.zeros_like(o_ref)

    @pl.when(pl.program_id(0) < DEC_BATCH // SWA_RB)
    def _():
        _swa_sample_rows(q_ref, k_ref, v_ref, kt_ref, vt_ref, bias_ref, bnew_ref, sink_ref, o_ref)


def _swa_sample_rows(q_ref, k_ref, v_ref, kt_ref, vt_ref, bias_ref, bnew_ref, sink_ref, o_ref):
    lane = lax.broadcasted_iota(jnp.int32, (4, WINDOW), 1)
    grp = A_HEADS // A_KV
    for r in range(SWA_RB):
        for kv in range(A_KV):
            hs = slice(kv * grp, (kv + 1) * grp)
            q4 = jnp.concatenate(
                [q_ref[r:r + 1, (kv * grp + g) * HEAD_DIM:(kv * grp + g + 1) * HEAD_DIM] for g in range(grp)],
                axis=0)
            kt = kt_ref[r, kv]
            vt = vt_ref[r, kv]
            s_old = jnp.dot(q4.astype(BF16), kt.astype(BF16), preferred_element_type=F32)
            s_old = s_old * SCALE + bias_ref[hs, :]
            s_old = jnp.where(lane >= 1, s_old, NEG_INF)
            kn = k_ref[r:r + 1, kv * HEAD_DIM:(kv + 1) * HEAD_DIM]
            vn = v_ref[r:r + 1, kv * HEAD_DIM:(kv + 1) * HEAD_DIM]
            s_new = jnp.sum(q4 * kn, axis=-1, keepdims=True) * SCALE + bnew_ref[hs, 0:1]
            sk = sink_ref[hs, 0:1]
            m = jnp.maximum(jnp.maximum(jnp.max(s_old, axis=-1, keepdims=True), s_new), sk)
            p_old = jnp.exp(s_old - m)
            p_new = jnp.exp(s_new - m)
            den = jnp.sum(p_old, axis=-1, keepdims=True) + p_new + jnp.exp(sk - m)
            o = lax.dot_general(p_old.astype(BF16), vt.astype(BF16), (((1,), (1,)), ((), ())),
                                preferred_element_type=F32)
            o = (o + p_new * vn) / den
            for g in range(grp):
                hh = kv * grp + g
                o_ref[r:r + 1, hh * HEAD_DIM:(hh + 1) * HEAD_DIM] = o[g:g + 1, :]


def _swa_sample(att, q, k, v, kt, vt, bias_s, bnew_b, sink_b):
    base = N_PROMPT // SWA_RB
    rows = lambda n: pl.BlockSpec((SWA_RB, n), lambda i: (base + i, 0))
    cache = pl.BlockSpec((SWA_RB, A_KV, HEAD_DIM, WINDOW),
                         lambda i: (jnp.minimum(i, DEC_BATCH // SWA_RB - 1), 0, 0, 0))
    small = pl.BlockSpec((A_HEADS, LANES), lambda i: (0, 0))
    return pl.pallas_call(
        _swa_sample_kernel,
        grid=(TM // SWA_RB,),
        in_specs=[pl.BlockSpec(memory_space=pl.ANY), rows(512), rows(128), rows(128), cache, cache,
                  small, small, small],
        out_specs=rows(512),
        out_shape=jax.ShapeDtypeStruct((N_TOK, 512), F32),
        input_output_aliases={0: 0},
        compiler_params=_cparams("arbitrary"),
        name="swa_sample",
    )(att, q, k, v, kt, vt, bias_s, bnew_b, sink_b)


def _route(x1, wr_ref, br_ref):
    logits = jnp.dot(x1, wr_ref[...], preferred_element_type=F32, precision=lax.Precision.HIGHEST)
    logits = logits + br_ref[...]
    lane = lax.broadcasted_iota(jnp.int32, logits.shape, 1)
    lane_f = lane.astype(F32)
    lane_grp = (lane >> 2).astype(F32)
    is_grp = (lane >= N_EXPERTS) & (lane < N_EXPERTS + N_GROUPS)
    big = 1e9
    gl = jnp.where(is_grp, logits, NEG_INF)
    gmax = jnp.max(gl, axis=-1, keepdims=True)
    gidx = jnp.min(jnp.where(is_grp & (logits == gmax), lane_f - N_EXPERTS, big), axis=-1, keepdims=True)
    gsum = jnp.sum(jnp.where(is_grp, jnp.exp(gl - gmax), 0.0), axis=-1, keepdims=True)
    grp_w = 1.0 / gsum
    in_grp = (lane < N_EXPERTS) & (lane_grp == gidx)
    e1 = jnp.where(in_grp, logits, NEG_INF)
    t1 = jnp.max(e1, axis=-1, keepdims=True)
    i1 = jnp.min(jnp.where(in_grp & (logits == t1), lane_f, big), axis=-1, keepdims=True)
    rest = in_grp & (lane_f != i1)
    e2 = jnp.where(rest, logits, NEG_INF)
    t2 = jnp.max(e2, axis=-1, keepdims=True)
    i2 = jnp.min(jnp.where(rest & (logits == t2), lane_f, big), axis=-1, keepdims=True)
    ex = jnp.exp(t2 - t1)
    g1 = grp_w / (1.0 + ex)
    g2 = grp_w * ex / (1.0 + ex)
    out = jnp.where(lane == 0, i1, 0.0)
    out = jnp.where(lane == 1, i2, out)
    out = jnp.where(lane == 2, g1, out)
    out = jnp.where(lane == 3, g2, out)
    return out


def _post_mix_kernel(n_mix, *refs):
    mix_refs = refs[:n_mix]
    x_ref, w_ref, g_ref, b_ref, wr_ref, br_ref, x1_ref, r_ref = refs[n_mix:]
    kw = D_MODEL // n_mix
    mix = None
    for t, m_ref in enumerate(mix_refs):
        part = jnp.dot(m_ref[...].astype(BF16), w_ref[t * kw:(t + 1) * kw, :], preferred_element_type=F32)
        mix = part if mix is None else mix + part
    x1 = _layer_norm(ALPHA * x_ref[...] + mix, g_ref[...], b_ref[...])
    x1_ref[...] = x1
    r_ref[...] = _route(x1, wr_ref, br_ref)


def _post_mix(mixes, x, w_bf, g, b, wr, br):
    n_mix = len(mixes)
    kw = D_MODEL // n_mix
    tile = lambda n: pl.BlockSpec((TM, n), lambda i: (i, 0))
    full = lambda shape: pl.BlockSpec(shape, lambda i: (0,) * len(shape))
    return pl.pallas_call(
        functools.partial(_post_mix_kernel, n_mix),
        grid=(N_TILES,),
        in_specs=[tile(kw)] * n_mix + [tile(D_MODEL), full((D_MODEL, D_MODEL)), full((1, D_MODEL)),
                                       full((1, D_MODEL)), full((D_MODEL, LANES)), full((1, LANES))],
        out_specs=[tile(D_MODEL), tile(LANES)],
        out_shape=[jax.ShapeDtypeStruct((N_TOK, D_MODEL), F32),
                   jax.ShapeDtypeStruct((N_TOK, LANES), F32)],
        compiler_params=_cparams("arbitrary"),
        name="post_mix",
    )(*mixes, x, w_bf, g, b, wr, br)


def _expert_kernel(te_ref, nt_ref, xs_ref, gate_ref, wg_ref, wu_ref, wd_ref, ys_ref):
    i = pl.program_id(0)

    @pl.when(i < nt_ref[0])
    def _():
        xs = xs_ref[...]
        g = jnp.dot(xs, wg_ref[...], preferred_element_type=F32)
        u = jnp.dot(xs, wu_ref[...], preferred_element_type=F32)
        h = g * (1.0 / (1.0 + jnp.exp(-g))) * u * gate_ref[...]
        ys_ref[...] = jnp.dot(h.astype(BF16), wd_ref[...], preferred_element_type=F32)

    @pl.when(i >= nt_ref[0])
    def _():
        ys_ref[...] = jnp.zeros_like(ys_ref)


def _experts(tile_expert, n_used, xs, gate_pos, wg, wu, wd):
    grid_spec = pltpu.PrefetchScalarGridSpec(
        num_scalar_prefetch=2,
        grid=(N_ETILES,),
        in_specs=[pl.BlockSpec((TE, D_MODEL), lambda i, te, nt: (i, 0)),
                  pl.BlockSpec((TE, 1), lambda i, te, nt: (i, 0)),
                  pl.BlockSpec((None, D_MODEL, D_EXPERT), lambda i, te, nt: (te[i], 0, 0)),
                  pl.BlockSpec((None, D_MODEL, D_EXPERT), lambda i, te, nt: (te[i], 0, 0)),
                  pl.BlockSpec((None, D_EXPERT, D_MODEL), lambda i, te, nt: (te[i], 0, 0))],
        out_specs=pl.BlockSpec((TE, D_MODEL), lambda i, te, nt: (i, 0)),
    )
    return pl.pallas_call(
        _expert_kernel,
        grid_spec=grid_spec,
        out_shape=jax.ShapeDtypeStruct((N_POS, D_MODEL), F32),
        compiler_params=_cparams("arbitrary"),
        name="experts",
    )(tile_expert, n_used, xs, gate_pos, wg, wu, wd)


def _dispatch(rinfo):
    ids = rinfo[:, 0:2].astype(jnp.int32).reshape(-1)
    gates = rinfo[:, 2:4].reshape(-1)
    order = jnp.argsort(ids, stable=True).astype(jnp.int32)
    e_sorted = ids[order]
    counts = jnp.sum(ids[:, None] == jnp.arange(N_EXPERTS, dtype=jnp.int32)[None, :], axis=0).astype(jnp.int32)
    padded = ((counts + TE - 1) // TE) * TE
    pad_end = jnp.cumsum(padded)
    pad_off = pad_end - padded
    off = jnp.cumsum(counts) - counts
    pos_sorted = pad_off[e_sorted] + (jnp.arange(N_SLOTS, dtype=jnp.int32) - off[e_sorted])
    pos_of_slot = jnp.zeros((N_SLOTS,), jnp.int32).at[order].set(pos_sorted)
    src_tok = jnp.zeros((N_POS,), jnp.int32).at[pos_sorted].set(order // 2)
    gate_pos = jnp.zeros((N_POS,), F32).at[pos_sorted].set(gates[order])
    tile_start = jnp.arange(N_ETILES, dtype=jnp.int32) * TE
    tile_expert = jnp.minimum(jnp.searchsorted(pad_end, tile_start, side="right"), N_EXPERTS - 1).astype(jnp.int32)
    n_used = (pad_end[-1] // TE).astype(jnp.int32).reshape(1)
    return src_tok, gate_pos.reshape(N_POS, 1), pos_of_slot, tile_expert, n_used


def _ln2_kernel(x1_ref, ya_ref, yb_ref, g_ref, b_ref, o_ref):
    o_ref[...] = _layer_norm(ALPHA * x1_ref[...] + (ya_ref[...] + yb_ref[...]), g_ref[...], b_ref[...])


def _ln2(x1, ya, yb, g, b):
    tile = pl.BlockSpec((TM, D_MODEL), lambda i: (i, 0))
    vec = pl.BlockSpec((1, D_MODEL), lambda i: (0, 0))
    return pl.pallas_call(
        _ln2_kernel,
        grid=(N_TILES,),
        in_specs=[tile, tile, tile, vec, vec],
        out_specs=tile,
        out_shape=jax.ShapeDtypeStruct((N_TOK, D_MODEL), F32),
        compiler_params=_cparams("arbitrary"),
        name="ln2",
    )(x1, ya, yb, g, b)


def _moe(x1, rinfo, wg, wu, wd, g2, b2):
    src_tok, gate_pos, pos_of_slot, tile_expert, n_used = _dispatch(rinfo)
    xs = x1.astype(BF16)[src_tok]
    ys = _experts(tile_expert, n_used, xs, gate_pos, wg, wu, wd)
    pos2 = pos_of_slot.reshape(N_TOK, 2)
    return _ln2(x1, ys[pos2[:, 0]], ys[pos2[:, 1]], g2, b2)


def _odd_in_kernel(x_ref, w_ref, bf_ref,
                   q_ref, ktp_ref, vtp_ref, lftp_ref, ctp_ref, c_ref, kts_ref, vts_ref, lfts_ref, carry_ref):
    i = pl.program_id(0)
    h = jnp.dot(x_ref[...].astype(BF16), w_ref[...], preferred_element_type=F32)
    q_ref[...] = h[:, 0:1024]
    kt = h[:, 1024:1280].T
    vt = h[:, 1280:1536].T
    z = h[:, 1536:1664] + bf_ref[...]
    lf = -(jnp.maximum(-z, 0.0) + jnp.log1p(jnp.exp(-jnp.abs(z))))
    lft = lf.T[0:C_HEADS, :]

    @pl.when(i < N_PTILES)
    def _():
        @pl.when(i % TILES_PER_SEQ == 0)
        def _():
            carry_ref[...] = jnp.zeros_like(carry_ref)
        row = lax.broadcasted_iota(jnp.int32, (TM, TM), 0)
        col = lax.broadcasted_iota(jnp.int32, (TM, TM), 1)
        tri = jnp.where(row >= col, 1.0, 0.0).astype(F32)
        c = jnp.dot(tri, lf, preferred_element_type=F32, precision=lax.Precision.HIGHEST) + carry_ref[0:1, :]
        carry_ref[0:1, :] = c[TM - 1:TM, :]
        c_ref[...] = c
        ctp_ref[...] = c.T[0:C_HEADS, :]
        ktp_ref[...] = kt
        vtp_ref[...] = vt
        lftp_ref[...] = lft

    @pl.when(i == N_PTILES)
    def _():
        c_ref[...] = lf
        kts_ref[...] = kt[:, 0:DEC_BATCH]
        vts_ref[...] = vt[:, 0:DEC_BATCH]
        lfts_ref[...] = lft[:, 0:DEC_BATCH]


def _odd_in(x, w_bf, bf):
    tile = lambda n: pl.BlockSpec((TM, n), lambda i: (i, 0))
    full = lambda shape: pl.BlockSpec(shape, lambda i: (0,) * len(shape))

    def tpose(rows):
        return pl.BlockSpec((None, rows, TM),
                            lambda i: (jnp.minimum(i // TILES_PER_SEQ, BATCH - 1), 0,
                                       jnp.where(i < N_PTILES, i % TILES_PER_SEQ, TILES_PER_SEQ - 1)))
    return pl.pallas_call(
        _odd_in_kernel,
        grid=(N_TILES,),
        in_specs=[tile(D_MODEL), full((D_MODEL, 1664)), full((1, LANES))],
        out_specs=[tile(1024), tpose(256), tpose(256), tpose(C_HEADS), tpose(C_HEADS), tile(LANES),
                   full((256, DEC_BATCH)), full((256, DEC_BATCH)), full((C_HEADS, DEC_BATCH))],
        out_shape=[jax.ShapeDtypeStruct((N_TOK, 1024), F32),
                   jax.ShapeDtypeStruct((BATCH, 256, SEQ), F32),
                   jax.ShapeDtypeStruct((BATCH, 256, SEQ), F32),
                   jax.ShapeDtypeStruct((BATCH, C_HEADS, SEQ), F32),
                   jax.ShapeDtypeStruct((BATCH, C_HEADS, SEQ), F32),
                   jax.ShapeDtypeStruct((N_TOK, LANES), F32),
                   jax.ShapeDtypeStruct((256, DEC_BATCH), F32),
                   jax.ShapeDtypeStruct((256, DEC_BATCH), F32),
                   jax.ShapeDtypeStruct((C_HEADS, DEC_BATCH), F32)],
        scratch_shapes=[pltpu.VMEM((8, LANES), F32)],
        compiler_params=_cparams("arbitrary"),
        name="odd_in",
    )(x, w_bf, bf)


FQ = 128
FK = 256
C_GROUP = C_HEADS // C_KV


def _fox_prompt_kernel(q_ref, c_ref, kt_ref, vt_ref, ct_ref, o_ref):
    kv = pl.program_id(1)
    qi = pl.program_id(2)
    rows = C_GROUP * FQ
    q4 = jnp.concatenate([q_ref[:, g * HEAD_DIM:(g + 1) * HEAD_DIM] for g in range(C_GROUP)], axis=0)
    q4 = (q4 * SCALE).astype(BF16)
    lane = lax.broadcasted_iota(jnp.int32, (FQ, LANES), 1)
    cfull = c_ref[...]
    cq = jnp.concatenate(
        [jnp.sum(jnp.where(lane == kv * C_GROUP + g, cfull, 0.0), axis=-1, keepdims=True) for g in range(C_GROUP)],
        axis=0)
    qpos = qi * FQ + lax.broadcasted_iota(jnp.int32, (rows, FK), 0) % FQ
    kcol = lax.broadcasted_iota(jnp.int32, (rows, FK), 1)

    def chunk(j, carry, masked):
        m, l, acc = carry
        start = pl.multiple_of(j * FK, FK)
        ktc = kt_ref[:, pl.ds(start, FK)].astype(BF16)
        vtc = vt_ref[:, pl.ds(start, FK)].astype(BF16)
        ck = ct_ref[:, pl.ds(start, FK)]
        ck4 = jnp.concatenate([jnp.broadcast_to(ck[g:g + 1, :], (FQ, FK)) for g in range(C_GROUP)], axis=0)
        s = jnp.dot(q4, ktc, preferred_element_type=F32) + (cq - ck4)
        if masked:
            s = jnp.where(start + kcol <= qpos, s, NEG_INF)
        m_new = jnp.maximum(m, jnp.max(s, axis=-1, keepdims=True))
        a = jnp.exp(m - m_new)
        p = jnp.exp(s - m_new)
        l = a * l + jnp.sum(p, axis=-1, keepdims=True)
        pv = lax.dot_general(p.astype(BF16), vtc, (((1,), (1,)), ((), ())), preferred_element_type=F32)
        return m_new, l, a * acc + pv

    init = (jnp.full((rows, 1), NEG_INF, F32), jnp.zeros((rows, 1), F32), jnp.zeros((rows, HEAD_DIM), F32))
    n_full = (qi * FQ) // FK
    carry = lax.fori_loop(0, n_full, lambda j, c: chunk(j, c, False), init)
    m, l, acc = chunk(n_full, carry, True)
    out = acc / l
    for g in range(C_GROUP):
        o_ref[:, g * HEAD_DIM:(g + 1) * HEAD_DIM] = out[g * FQ:(g + 1) * FQ, :]


def _fox_prompt(q, c, ktp, vtp, ctp4):
    nq = SEQ // FQ
    return pl.pallas_call(
        _fox_prompt_kernel,
        grid=(BATCH, C_KV, nq),
        in_specs=[pl.BlockSpec((FQ, C_GROUP * HEAD_DIM), lambda b, kv, qi: (b * nq + qi, kv)),
                  pl.BlockSpec((FQ, LANES), lambda b, kv, qi: (b * nq + qi, 0)),
                  pl.BlockSpec((None, HEAD_DIM, SEQ), lambda b, kv, qi: (b, kv, 0)),
                  pl.BlockSpec((None, HEAD_DIM, SEQ), lambda b, kv, qi: (b, kv, 0)),
                  pl.BlockSpec((None, None, C_GROUP, SEQ), lambda b, kv, qi: (b, kv, 0, 0))],
        out_specs=pl.BlockSpec((FQ, C_GROUP * HEAD_DIM), lambda b, kv, qi: (b * nq + qi, kv)),
        out_shape=jax.ShapeDtypeStruct((N_TOK, C_HEADS * HEAD_DIM), F32),
        compiler_params=_cparams("arbitrary", "arbitrary", "arbitrary"),
        name="fox_prompt",
    )(q, c, ktp, vtp, ctp4)


PG = 8
N_PSTEPS = N_PAGES // PG
KVD = C_KV * HEAD_DIM


def _fox_sample_kernel(pt_ref, qbd_ref, kts_ref, vts_ref, lfts_ref, *refs):
    del pt_ref
    k_refs = refs[0:PG]
    v_refs = refs[PG:2 * PG]
    lf_refs = refs[2 * PG:3 * PG]
    o_ref, m_ref, l_ref, acc_ref, carry_ref = refs[3 * PG:]
    r = pl.program_id(0)
    j = pl.program_id(1)

    @pl.when(j == 0)
    def _():
        m_ref[...] = jnp.full_like(m_ref, NEG_INF)
        l_ref[...] = jnp.zeros_like(l_ref)
        acc_ref[...] = jnp.zeros_like(acc_ref)
        carry_ref[...] = jnp.zeros_like(carry_ref)

    qbd = qbd_ref[...].astype(BF16)

    def attend(s, vts):
        m_old = m_ref[:, 0:1]
        m_new = jnp.maximum(m_old, jnp.max(s, axis=-1, keepdims=True))
        a = jnp.exp(m_old - m_new)
        p = jnp.exp(s - m_new)
        l_ref[...] = jnp.broadcast_to(a * l_ref[:, 0:1] + jnp.sum(p, axis=-1, keepdims=True), l_ref.shape)
        m_ref[...] = jnp.broadcast_to(m_new, m_ref.shape)
        pv = None
        for g, vt in enumerate(vts):
            part = lax.dot_general(p[:, g * PAGE:(g + 1) * PAGE].astype(BF16), vt.astype(BF16),
                                   (((1,), (1,)), ((), ())), preferred_element_type=F32)
            pv = part if pv is None else pv + part
        acc_ref[...] = a * acc_ref[...] + pv

    lf_all = jnp.concatenate([lf_refs[g][...] for g in range(PG)], axis=0)
    rr = lax.broadcasted_iota(jnp.int32, (PAGE, PAGE), 0)
    cc = lax.broadcasted_iota(jnp.int32, (PAGE, PAGE), 1)
    upper = jnp.where(rr <= cc, 1.0, 0.0).astype(F32)
    c_loc = jnp.dot(lf_all, upper, preferred_element_type=F32, precision=lax.Precision.HIGHEST)
    carry = carry_ref[:, 0:1]
    s_parts = []
    for g in range(PG):
        c_g = c_loc[g * C_HEADS:(g + 1) * C_HEADS, :] + carry
        carry = c_g[:, PAGE - 1:PAGE]
        kt = k_refs[g][...].reshape(KVD, PAGE).astype(BF16)
        s_parts.append(jnp.dot(qbd, kt, preferred_element_type=F32) - c_g)
    carry_ref[...] = jnp.broadcast_to(carry, carry_ref.shape)
    attend(jnp.concatenate(s_parts, axis=1), [v_refs[g][...].reshape(KVD, PAGE) for g in range(PG)])

    @pl.when(j == N_PSTEPS - 1)
    def _():
        lane = lax.broadcasted_iota(jnp.int32, (C_HEADS, DEC_BATCH), 1)
        c_new = carry + lfts_ref[...]
        s_new = jnp.dot(qbd, kts_ref[...].astype(BF16), preferred_element_type=F32) - c_new
        s_new = jnp.where(lane == r, s_new, NEG_INF)
        attend(s_new, [vts_ref[...]])
        o_ref[...] = acc_ref[...] / l_ref[:, 0:1]


def _fox_sample(layer, page_flat, qbd, kts, vts, lfts, kc, vc, lfc):
    def page_spec(g, inner):
        nz = (0,) * len(inner)
        return pl.BlockSpec((None, None) + inner,
                            lambda r, j, pt: (layer, pt[r * N_PAGES + j * PG + g]) + nz)
    const = lambda shape: pl.BlockSpec(shape, lambda r, j, pt: (0,) * len(shape))
    grid_spec = pltpu.PrefetchScalarGridSpec(
        num_scalar_prefetch=1,
        grid=(DEC_BATCH, N_PSTEPS),
        in_specs=([pl.BlockSpec((None, C_HEADS, KVD), lambda r, j, pt: (r, 0, 0)),
                   const((KVD, DEC_BATCH)), const((KVD, DEC_BATCH)), const((C_HEADS, DEC_BATCH))]
                  + [page_spec(g, (C_KV, HEAD_DIM, PAGE)) for g in range(PG)]
                  + [page_spec(g, (C_KV, HEAD_DIM, PAGE)) for g in range(PG)]
                  + [page_spec(g, (C_HEADS, PAGE)) for g in range(PG)]),
        out_specs=pl.BlockSpec((None, C_HEADS, KVD), lambda r, j, pt: (r, 0, 0)),
        scratch_shapes=[pltpu.VMEM((C_HEADS, LANES), F32), pltpu.VMEM((C_HEADS, LANES), F32),
                        pltpu.VMEM((C_HEADS, KVD), F32), pltpu.VMEM((C_HEADS, LANES), F32)],
    )
    return pl.pallas_call(
        _fox_sample_kernel,
        grid_spec=grid_spec,
        out_shape=jax.ShapeDtypeStruct((DEC_BATCH, C_HEADS, KVD), F32),
        compiler_params=_cparams("arbitrary", "arbitrary"),
        name="fox_sample",
    )(page_flat, qbd, kts, vts, lfts, *([kc] * PG), *([vc] * PG), *([lfc] * PG))


def _t5_bucket(dist):
    n = jnp.maximum(dist, 0)
    max_exact = N_BUCKETS // 2
    nf = jnp.maximum(n, 1).astype(F32)
    large = max_exact + (jnp.log(nf / max_exact) / math.log(MAX_DISTANCE / max_exact)
                         * (N_BUCKETS - max_exact)).astype(jnp.int32)
    large = jnp.minimum(large, N_BUCKETS - 1)
    return jnp.where(n < max_exact, n, large)


def kernel(x_prompt, x_sample, cache_swa_k, cache_swa_v, state_conv, cache_fox_k, cache_fox_v, cache_fox_logf, page_table, rel_bias_table, attn_sinks, w_in_even, conv_w, w_out_even, w_in_odd, b_forget, w_out_odd, ln_g, ln_b, w_group, b_group, w_router, b_router, w_gate, w_up, w_down):
    x = jnp.concatenate([x_prompt.reshape(N_PROMPT, D_MODEL), x_sample.reshape(DEC_BATCH, D_MODEL),
                         jnp.zeros((N_TOK - N_PROMPT - DEC_BATCH, D_MODEL), F32)], axis=0)

    qi = jnp.arange(WINDOW)[:, None]
    kj = jnp.arange(2 * WINDOW)[None, :]
    bias_p = jnp.moveaxis(rel_bias_table[_t5_bucket(WINDOW + qi - kj)], -1, 0)
    bias_s = rel_bias_table[_t5_bucket(WINDOW - jnp.arange(WINDOW))].T
    bnew_b = jnp.broadcast_to(rel_bias_table[0][:, None], (A_HEADS, LANES))

    swa_kt = jnp.transpose(cache_swa_k, (0, 1, 3, 4, 2))
    swa_vt = jnp.transpose(cache_swa_v, (0, 1, 3, 4, 2))
    fox_kt = jnp.transpose(cache_fox_k, (0, 1, 3, 4, 2))
    fox_vt = jnp.transpose(cache_fox_v, (0, 1, 3, 4, 2))
    fox_lft = jnp.transpose(cache_fox_logf, (0, 1, 3, 2))
    page_flat = page_table.reshape(-1)

    wr_all = jnp.concatenate([w_router, w_group, jnp.zeros((DEPTH, D_MODEL, LANES - 20), F32)], axis=-1)
    br_all = jnp.concatenate([b_router, b_group, jnp.zeros((DEPTH, LANES - 20), F32)], axis=-1)
    wg_bf = w_gate.astype(BF16)
    wu_bf = w_up.astype(BF16)
    wd_bf = w_down.astype(BF16)
    eye = jnp.eye(C_KV, dtype=F32)

    outs = {n: [] for n in ("swa_kp", "swa_vp", "swa_ks", "swa_vs", "conv_p", "conv_s",
                            "fox_kp", "fox_vp", "fox_lp", "fox_ks", "fox_vs", "fox_ls")}
    for layer in range(DEPTH):
        if layer % 2 == 0:
            e = layer // 2
            s0 = state_conv[e, :, 0, :]
            s1 = state_conv[e, :, 1, :]
            q, k, v, cv, convp, us = _even_in(x, w_in_even[e].astype(BF16), conv_w[e], s0, s1)
            sink_b = jnp.broadcast_to(attn_sinks[e][:, None], (A_HEADS, LANES))
            att = _swa_prompt(q, k, v, bias_p, sink_b)
            att = _swa_sample(att, q, k, v, swa_kt[e], swa_vt[e], bias_s, bnew_b, sink_b)
            mixes = [att, cv]
            w_out = w_out_even[e]
            kp = k[:N_PROMPT].reshape(BATCH, SEQ, A_KV, HEAD_DIM)
            vp = v[:N_PROMPT].reshape(BATCH, SEQ, A_KV, HEAD_DIM)
            ks = k[N_PROMPT:N_PROMPT + DEC_BATCH].reshape(DEC_BATCH, 1, A_KV, HEAD_DIM)
            vs = v[N_PROMPT:N_PROMPT + DEC_BATCH].reshape(DEC_BATCH, 1, A_KV, HEAD_DIM)
            outs["swa_kp"].append(kp[:, SEQ - WINDOW:])
            outs["swa_vp"].append(vp[:, SEQ - WINDOW:])
            outs["swa_ks"].append(jnp.concatenate([cache_swa_k[e][:, 1:], ks], axis=1))
            outs["swa_vs"].append(jnp.concatenate([cache_swa_v[e][:, 1:], vs], axis=1))
            outs["conv_p"].append(convp)
            outs["conv_s"].append(jnp.stack([s1, us], axis=1))
        else:
            o = layer // 2
            w_pad = jnp.concatenate([w_in_odd[o], jnp.zeros((D_MODEL, 1664 - 1552), F32)], axis=-1).astype(BF16)
            bf_pad = jnp.concatenate([b_forget[o], jnp.zeros((LANES - C_HEADS,), F32)])[None, :]
            q, ktp, vtp, lftp, ctp, c, kts, vts, lfts = _odd_in(x, w_pad, bf_pad)
            att = _fox_prompt(q, c, ktp, vtp, ctp.reshape(BATCH, C_KV, C_GROUP, SEQ))
            qs = q[N_PROMPT:N_PROMPT + DEC_BATCH].reshape(DEC_BATCH, C_KV, C_GROUP, 1, HEAD_DIM) * SCALE
            qbd = (qs * eye[None, :, None, :, None]).reshape(DEC_BATCH, C_HEADS, KVD)
            of = _fox_sample(o, page_flat, qbd, kts, vts, lfts, fox_kt, fox_vt, fox_lft)
            of = of.reshape(DEC_BATCH, C_KV, C_GROUP, C_KV, HEAD_DIM)
            att_s = jnp.sum(of * eye[None, :, None, :, None], axis=3).reshape(DEC_BATCH, C_HEADS * HEAD_DIM)
            att_s = jnp.concatenate([att_s, jnp.zeros((TM - DEC_BATCH, C_HEADS * HEAD_DIM), F32)], axis=0)
            att = lax.dynamic_update_slice(att, att_s, (N_PROMPT, 0))
            mixes = [att]
            w_out = w_out_odd[o]
            outs["fox_kp"].append(jnp.transpose(ktp.reshape(BATCH, C_KV, HEAD_DIM, SEQ), (0, 3, 1, 2)))
            outs["fox_vp"].append(jnp.transpose(vtp.reshape(BATCH, C_KV, HEAD_DIM, SEQ), (0, 3, 1, 2)))
            outs["fox_lp"].append(jnp.transpose(lftp, (0, 2, 1)))
            outs["fox_ks"].append(jnp.transpose(kts.reshape(C_KV, HEAD_DIM, DEC_BATCH), (2, 0, 1))[:, None])
            outs["fox_vs"].append(jnp.transpose(vts.reshape(C_KV, HEAD_DIM, DEC_BATCH), (2, 0, 1))[:, None])
            outs["fox_ls"].append(lfts.T[:, None, :])
        x1, rinfo = _post_mix(mixes, x, w_out.astype(BF16), ln_g[layer, 0][None, :], ln_b[layer, 0][None, :],
                              wr_all[layer], br_all[layer][None, :])
        x = _moe(x1, rinfo, wg_bf[layer], wu_bf[layer], wd_bf[layer], ln_g[layer, 1][None, :], ln_b[layer, 1][None, :])

    st = {n: jnp.stack(vl) for n, vl in outs.items()}
    y_prompt = x[:N_PROMPT].reshape(BATCH, SEQ, D_MODEL)
    y_sample = x[N_PROMPT:N_PROMPT + DEC_BATCH].reshape(DEC_BATCH, 1, D_MODEL)
    return (y_prompt, y_sample, st["swa_kp"], st["swa_vp"], st["swa_ks"], st["swa_vs"], st["conv_p"], st["conv_s"],
            st["fox_kp"], st["fox_vp"], st["fox_lp"], st["fox_ks"], st["fox_vs"], st["fox_ls"])
```

```python
import functools
import math

import jax
import jax.numpy as jnp
import numpy as np
from jax import lax
from jax.experimental import pallas as pl
from jax.experimental.pallas import tpu as pltpu

F32 = jnp.float32
BF16 = jnp.bfloat16

D_MODEL = 1024
BATCH = 4
SEQ = 4096
DEC_BATCH = 128
PAGE = 128
N_PAGES = 64
HEAD_DIM = 64
A_HEADS = 8
A_KV = 2
WINDOW = 128
B_WIDTH = 512
C_HEADS = 16
C_KV = 4
N_BUCKETS = 32
MAX_DISTANCE = 128
N_GROUPS = 4
N_EXPERTS = 16
D_EXPERT = 256
DEPTH = 4
ALPHA = (2.0 * DEPTH) ** 0.25
LN_EPS = 1e-5
NEG_INF = -1e30
SCALE = HEAD_DIM ** -0.5

N_PROMPT = BATCH * SEQ
TM = 512
N_PTILES = N_PROMPT // TM
TILES_PER_SEQ = SEQ // TM
N_TILES = N_PTILES + 1
N_TOK = N_TILES * TM
LANES = 128

TE = 256
N_SLOTS = 2 * N_TOK
N_VISITS = N_SLOTS // TE + N_EXPERTS - 1

VMEM_LIMIT = 56 * 1024 * 1024


def _cparams(*sem):
    return pltpu.CompilerParams(dimension_semantics=sem, vmem_limit_bytes=VMEM_LIMIT)


def _layer_norm(y, g, b):
    mu = jnp.mean(y, axis=-1, keepdims=True)
    yc = y - mu
    var = jnp.mean(yc * yc, axis=-1, keepdims=True)
    return yc * lax.rsqrt(var + LN_EPS) * g + b


def _even_in_kernel(x_ref, w_ref, cw_ref, s0_ref, s1_ref,
                    q_ref, k_ref, v_ref, cv_ref, convp_ref, us_ref, carry_ref):
    i = pl.program_id(0)
    h = jnp.dot(x_ref[...].astype(BF16), w_ref[...], preferred_element_type=F32)
    q_ref[...] = h[:, 0:512]
    k_ref[...] = h[:, 512:640]
    v_ref[...] = h[:, 640:768]
    bg = h[:, 768:1280]
    u = h[:, 1280:1792] * h[:, 1792:2304]
    w0 = cw_ref[0:1, :]
    w1 = cw_ref[1:2, :]
    w2 = cw_ref[2:3, :]

    @pl.when(i < N_PTILES)
    def _():
        @pl.when(i % TILES_PER_SEQ == 0)
        def _():
            carry_ref[...] = jnp.zeros_like(carry_ref)
        row = lax.broadcasted_iota(jnp.int32, (TM, B_WIDTH), 0)
        c2 = carry_ref[0:1, :]
        c1 = carry_ref[1:2, :]
        u1 = jnp.where(row == 0, c1, pltpu.roll(u, 1, 0))
        u2 = jnp.where(row == 0, c2, jnp.where(row == 1, c1, pltpu.roll(u, 2, 0)))
        cv_ref[...] = bg * (w0 * u2 + w1 * u1 + w2 * u)
        carry_ref[0:2, :] = u[TM - 2:TM, :]
        convp_ref[...] = u[TM - 2:TM, :]

    @pl.when(i == N_PTILES)
    def _():
        us = u[0:DEC_BATCH, :]
        z = w0 * s0_ref[...] + w1 * s1_ref[...] + w2 * us
        cv_ref[0:DEC_BATCH, :] = bg[0:DEC_BATCH, :] * z
        cv_ref[DEC_BATCH:TM, :] = jnp.zeros((TM - DEC_BATCH, B_WIDTH), F32)
        us_ref[...] = us


def _even_in(x, w_bf, cw, s0, s1):
    tile = lambda n: pl.BlockSpec((TM, n), lambda i: (i, 0))
    full = lambda shape: pl.BlockSpec(shape, lambda i: (0,) * len(shape))
    return pl.pallas_call(
        _even_in_kernel,
        grid=(N_TILES,),
        in_specs=[tile(D_MODEL), full((D_MODEL, 2304)), full((3, B_WIDTH)),
                  full((DEC_BATCH, B_WIDTH)), full((DEC_BATCH, B_WIDTH))],
        out_specs=[tile(512), tile(128), tile(128), tile(B_WIDTH),
                   pl.BlockSpec((None, 2, B_WIDTH),
                                lambda i: (jnp.minimum(i // TILES_PER_SEQ, BATCH - 1), 0, 0)),
                   full((DEC_BATCH, B_WIDTH))],
        out_shape=[jax.ShapeDtypeStruct((N_TOK, 512), F32),
                   jax.ShapeDtypeStruct((N_TOK, 128), F32),
                   jax.ShapeDtypeStruct((N_TOK, 128), F32),
                   jax.ShapeDtypeStruct((N_TOK, B_WIDTH), F32),
                   jax.ShapeDtypeStruct((BATCH, 2, B_WIDTH), F32),
                   jax.ShapeDtypeStruct((DEC_BATCH, B_WIDTH), F32)],
        scratch_shapes=[pltpu.VMEM((8, B_WIDTH), F32)],
        compiler_params=_cparams("arbitrary"),
        name="even_in",
    )(x, w_bf, cw, s0, s1)


def _swa_prompt_kernel(q_ref, kc_ref, kp_ref, vc_ref, vp_ref, bias_ref, sink_ref, o_ref):
    j = pl.program_id(1)
    qi = lax.broadcasted_iota(jnp.int32, (WINDOW, 2 * WINDOW), 0)
    kj = lax.broadcasted_iota(jnp.int32, (WINDOW, 2 * WINDOW), 1)
    dist = WINDOW + qi - kj
    valid = (dist >= 0) & (dist < WINDOW) & ((kj >= WINDOW) | (j > 0))
    for h in range(A_HEADS):
        kv = h // (A_HEADS // A_KV)
        cs = slice(kv * HEAD_DIM, (kv + 1) * HEAD_DIM)
        qh = q_ref[:, h * HEAD_DIM:(h + 1) * HEAD_DIM].astype(BF16)
        kk = jnp.concatenate([kp_ref[:, cs], kc_ref[:, cs]], axis=0).astype(BF16)
        vv = jnp.concatenate([vp_ref[:, cs], vc_ref[:, cs]], axis=0).astype(BF16)
        s = lax.dot_general(qh, kk, (((1,), (1,)), ((), ())), preferred_element_type=F32)
        s = s * SCALE + bias_ref[h]
        s = jnp.where(valid, s, NEG_INF)
        sk = sink_ref[h:h + 1, 0:1]
        m = jnp.maximum(jnp.max(s, axis=-1, keepdims=True), sk)
        p = jnp.exp(s - m)
        den = jnp.sum(p, axis=-1, keepdims=True) + jnp.exp(sk - m)
        o = jnp.dot(p.astype(BF16), vv, preferred_element_type=F32)
        o_ref[:, h * HEAD_DIM:(h + 1) * HEAD_DIM] = o / den


def _swa_prompt(q, k, v, bias, sink_b):
    nblk = SEQ // WINDOW
    cur = lambda b, j: (b * nblk + j, 0)
    prev = lambda b, j: (jnp.maximum(b * nblk + j - 1, 0), 0)
    return pl.pallas_call(
        _swa_prompt_kernel,
        grid=(BATCH, nblk),
        in_specs=[pl.BlockSpec((WINDOW, 512), cur),
                  pl.BlockSpec((WINDOW, 128), cur), pl.BlockSpec((WINDOW, 128), prev),
                  pl.BlockSpec((WINDOW, 128), cur), pl.BlockSpec((WINDOW, 128), prev),
                  pl.BlockSpec((A_HEADS, WINDOW, 2 * WINDOW), lambda b, j: (0, 0, 0)),
                  pl.BlockSpec((A_HEADS, LANES), lambda b, j: (0, 0))],
        out_specs=pl.BlockSpec((WINDOW, 512), cur),
        out_shape=jax.ShapeDtypeStruct((N_TOK, 512), F32),
        input_output_aliases={},
        compiler_params=_cparams("arbitrary", "arbitrary"),
        name="swa_prompt",
    )(q, k, k, v, v, bias, sink_b)


SWA_RB = 8


def _swa_sample_kernel(att_in_ref, q_ref, k_ref, v_ref, kt_ref, vt_ref, bias_ref, bnew_ref, sink_ref, o_ref):
    del att_in_ref

    @pl.when(pl.program_id(0) >= DEC_BATCH // SWA_RB)
    def _():
        o_ref[...] = jnp.zeros_like(o_ref)

    @pl.when(pl.program_id(0) < DEC_BATCH // SWA_RB)
    def _():
        _swa_sample_rows(q_ref, k_ref, v_ref, kt_ref, vt_ref, bias_ref, bnew_ref, sink_ref, o_ref)


def _swa_sample_rows(q_ref, k_ref, v_ref, kt_ref, vt_ref, bias_ref, bnew_ref, sink_ref, o_ref):
    lane = lax.broadcasted_iota(jnp.int32, (4, WINDOW), 1)
    grp = A_HEADS // A_KV
    for r in range(SWA_RB):
        for kv in range(A_KV):
            hs = slice(kv * grp, (kv + 1) * grp)
            q4 = jnp.concatenate(
                [q_ref[r:r + 1, (kv * grp + g) * HEAD_DIM:(kv * grp + g + 1) * HEAD_DIM] for g in range(grp)],
                axis=0)
            kt = kt_ref[r, kv]
            vt = vt_ref[r, kv]
            s_old = jnp.dot(q4.astype(BF16), kt.astype(BF16), preferred_element_type=F32)
            s_old = s_old * SCALE + bias_ref[hs, :]
            s_old = jnp.where(lane >= 1, s_old, NEG_INF)
            kn = k_ref[r:r + 1, kv * HEAD_DIM:(kv + 1) * HEAD_DIM]
            vn = v_ref[r:r + 1, kv * HEAD_DIM:(kv + 1) * HEAD_DIM]
            s_new = jnp.sum(q4 * kn, axis=-1, keepdims=True) * SCALE + bnew_ref[hs, 0:1]
            sk = sink_ref[hs, 0:1]
            m = jnp.maximum(jnp.maximum(jnp.max(s_old, axis=-1, keepdims=True), s_new), sk)
            p_old = jnp.exp(s_old - m)
            p_new = jnp.exp(s_new - m)
            den = jnp.sum(p_old, axis=-1, keepdims=True) + p_new + jnp.exp(sk - m)
            o = lax.dot_general(p_old.astype(BF16), vt.astype(BF16), (((1,), (1,)), ((), ())),
                                preferred_element_type=F32)
            o = (o + p_new * vn) / den
            for g in range(grp):
                hh = kv * grp + g
                o_ref[r:r + 1, hh * HEAD_DIM:(hh + 1) * HEAD_DIM] = o[g:g + 1, :]


def _swa_sample(att, q, k, v, kt, vt, bias_s, bnew_b, sink_b):
    base = N_PROMPT // SWA_RB
    rows = lambda n: pl.BlockSpec((SWA_RB, n), lambda i: (base + i, 0))
    cache = pl.BlockSpec((SWA_RB, A_KV, HEAD_DIM, WINDOW),
                         lambda i: (jnp.minimum(i, DEC_BATCH // SWA_RB - 1), 0, 0, 0))
    small = pl.BlockSpec((A_HEADS, LANES), lambda i: (0, 0))
    return pl.pallas_call(
        _swa_sample_kernel,
        grid=(TM // SWA_RB,),
        in_specs=[pl.BlockSpec(memory_space=pl.ANY), rows(512), rows(128), rows(128), cache, cache,
                  small, small, small],
        out_specs=rows(512),
        out_shape=jax.ShapeDtypeStruct((N_TOK, 512), F32),
        input_output_aliases={0: 0},
        compiler_params=_cparams("arbitrary"),
        name="swa_sample",
    )(att, q, k, v, kt, vt, bias_s, bnew_b, sink_b)


def _route(x1, wr_ref, br_ref, cnt_ref):
    x_hi = x1.astype(BF16)
    x_lo = (x1 - x_hi.astype(F32)).astype(BF16)
    pa = jnp.dot(x_hi, wr_ref[...], preferred_element_type=F32)
    pb = jnp.dot(x_lo, wr_ref[...], preferred_element_type=F32)
    logits = pa + pltpu.roll(pa, LANES - 32, 1) + pb + br_ref[...]
    lane = lax.broadcasted_iota(jnp.int32, logits.shape, 1)
    lane_f = lane.astype(F32)
    lane_grp = (lane >> 2).astype(F32)
    is_grp = (lane >= N_EXPERTS) & (lane < N_EXPERTS + N_GROUPS)
    big = 1e9
    gl = jnp.where(is_grp, logits, NEG_INF)
    gmax = jnp.max(gl, axis=-1, keepdims=True)
    gidx = jnp.min(jnp.where(is_grp & (logits == gmax), lane_f - N_EXPERTS, big), axis=-1, keepdims=True)
    gsum = jnp.sum(jnp.where(is_grp, jnp.exp(gl - gmax), 0.0), axis=-1, keepdims=True)
    grp_w = 1.0 / gsum
    in_grp = (lane < N_EXPERTS) & (lane_grp == gidx)
    e1 = jnp.where(in_grp, logits, NEG_INF)
    t1 = jnp.max(e1, axis=-1, keepdims=True)
    i1 = jnp.min(jnp.where(in_grp & (logits == t1), lane_f, big), axis=-1, keepdims=True)
    rest = in_grp & (lane_f != i1)
    e2 = jnp.where(rest, logits, NEG_INF)
    t2 = jnp.max(e2, axis=-1, keepdims=True)
    i2 = jnp.min(jnp.where(rest & (logits == t2), lane_f, big), axis=-1, keepdims=True)
    ex = jnp.exp(t2 - t1)
    g1 = grp_w / (1.0 + ex)
    g2 = grp_w * ex / (1.0 + ex)
    sel1 = lane_f == i1
    sel2 = lane_f == i2
    onehot = jnp.where(sel1 | sel2, 1.0, 0.0)
    rr = lax.broadcasted_iota(jnp.int32, (TM, TM), 0)
    cc = lax.broadcasted_iota(jnp.int32, (TM, TM), 1)
    below = jnp.where(rr > cc, 1.0, 0.0).astype(BF16)
    before = jnp.dot(below, onehot.astype(BF16), preferred_element_type=F32) + cnt_ref[0:1, :]
    rank1 = jnp.sum(jnp.where(sel1, before, 0.0), axis=-1, keepdims=True)
    rank2 = jnp.sum(jnp.where(sel2, before, 0.0), axis=-1, keepdims=True)
    cnt_ref[0:1, :] = cnt_ref[0:1, :] + jnp.sum(onehot, axis=0, keepdims=True)
    out = jnp.where(lane == 0, i1, 0.0)
    out = jnp.where(lane == 1, i2, out)
    out = jnp.where(lane == 2, g1, out)
    out = jnp.where(lane == 3, g2, out)
    out = jnp.where(lane == 4, rank1, out)
    out = jnp.where(lane == 5, rank2, out)
    return out


def _post_mix_kernel(n_mix, *refs):
    mix_refs = refs[:n_mix]
    x_ref, w_ref, g_ref, b_ref, wr_ref, br_ref, x1_ref, r_ref, tot_ref, cnt_ref = refs[n_mix:]
    kw = D_MODEL // n_mix

    @pl.when(pl.program_id(0) == 0)
    def _():
        cnt_ref[...] = jnp.zeros_like(cnt_ref)
    mix = None
    for t, m_ref in enumerate(mix_refs):
        part = jnp.dot(m_ref[...].astype(BF16), w_ref[t * kw:(t + 1) * kw, :], preferred_element_type=F32)
        mix = part if mix is None else mix + part
    x1 = _layer_norm(ALPHA * x_ref[...] + mix, g_ref[...], b_ref[...])
    x1_ref[...] = x1
    r_ref[...] = _route(x1, wr_ref, br_ref, cnt_ref)
    tot_ref[...] = cnt_ref[...]


def _post_mix(mixes, x, w_bf, g, b, wr_bf, br):
    n_mix = len(mixes)
    kw = D_MODEL // n_mix
    tile = lambda n: pl.BlockSpec((TM, n), lambda i: (i, 0))
    full = lambda shape: pl.BlockSpec(shape, lambda i: (0,) * len(shape))
    return pl.pallas_call(
        functools.partial(_post_mix_kernel, n_mix),
        grid=(N_TILES,),
        in_specs=[tile(kw)] * n_mix + [tile(D_MODEL), full((D_MODEL, D_MODEL)), full((1, D_MODEL)),
                                       full((1, D_MODEL)), full((D_MODEL, LANES)), full((1, LANES))],
        out_specs=[tile(D_MODEL), tile(LANES), full((8, LANES))],
        out_shape=[jax.ShapeDtypeStruct((N_TOK, D_MODEL), F32),
                   jax.ShapeDtypeStruct((N_TOK, LANES), F32),
                   jax.ShapeDtypeStruct((8, LANES), F32)],
        scratch_shapes=[pltpu.VMEM((8, LANES), F32)],
        compiler_params=_cparams("arbitrary"),
        name="post_mix",
    )(*mixes, x, w_bf, g, b, wr_bf, br)


def _expert_kernel(vt_ref, ve_ref, lo_ref, hi_ref, first_ref, xs_ref, gate_ref, wg_ref, wu_ref, wd_ref, ys_ref):
    del vt_ref, ve_ref
    v = pl.program_id(0)
    lo = lo_ref[v]
    hi = hi_ref[v]

    @pl.when(hi > lo)
    def _():
        row = lax.broadcasted_iota(jnp.int32, (TE, 1), 0)
        gate = jnp.where((row >= lo) & (row < hi), gate_ref[...], 0.0)
        xs = xs_ref[...].astype(BF16)
        g = jnp.dot(xs, wg_ref[...], preferred_element_type=F32)
        u = jnp.dot(xs, wu_ref[...], preferred_element_type=F32)
        h = g * (1.0 / (1.0 + jnp.exp(-g))) * u * gate
        y = jnp.dot(h.astype(BF16), wd_ref[...], preferred_element_type=F32)

        @pl.when(first_ref[v] == 1)
        def _():
            ys_ref[...] = y

        @pl.when(first_ref[v] == 0)
        def _():
            ys_ref[...] += y


def _experts(meta, xs, gate_sorted, wg, wu, wd):
    tile_map = lambda v, vt, ve, lo, hi, fi: (vt[v], 0)
    w_map = lambda v, vt, ve, lo, hi, fi: (ve[v], 0, 0)
    grid_spec = pltpu.PrefetchScalarGridSpec(
        num_scalar_prefetch=5,
        grid=(N_VISITS,),
        in_specs=[pl.BlockSpec((TE, D_MODEL), tile_map),
                  pl.BlockSpec((TE, 1), tile_map),
                  pl.BlockSpec((None, D_MODEL, D_EXPERT), w_map),
                  pl.BlockSpec((None, D_MODEL, D_EXPERT), w_map),
                  pl.BlockSpec((None, D_EXPERT, D_MODEL), w_map)],
        out_specs=pl.BlockSpec((TE, D_MODEL), tile_map),
    )
    return pl.pallas_call(
        _expert_kernel,
        grid_spec=grid_spec,
        out_shape=jax.ShapeDtypeStruct((N_SLOTS, D_MODEL), F32),
        compiler_params=_cparams("arbitrary"),
        name="experts",
    )(*meta, xs, gate_sorted, wg, wu, wd)


def _dispatch(rinfo, totals):
    ids = rinfo[:, 0:2].astype(jnp.int32)
    ranks = rinfo[:, 4:6].astype(jnp.int32)
    counts = totals[0, 0:N_EXPERTS].astype(jnp.int32)
    end = jnp.cumsum(counts)
    off = end - counts
    pos_of_pair = off[ids] + ranks
    tok = lax.broadcasted_iota(jnp.int32, (N_TOK, 2), 0)
    _, sorted_tok, sorted_gate = lax.sort((ids.reshape(-1), tok.reshape(-1), rinfo[:, 2:4].reshape(-1)),
                                          num_keys=1, is_stable=True)
    first_tile = off // TE
    n_vis = jnp.where(counts > 0, (end - 1) // TE - first_tile + 1, 0)
    v_end = jnp.cumsum(n_vis)
    v_start = v_end - n_vis
    v = jnp.arange(N_VISITS, dtype=jnp.int32)
    valid = v < v_end[-1]
    e_v = jnp.minimum(jnp.sum((v[:, None] >= v_end[None, :]).astype(jnp.int32), axis=1), N_EXPERTS - 1)
    tile_v = jnp.where(valid, first_tile[e_v] + v - v_start[e_v], N_SLOTS // TE - 1)
    lo = jnp.where(valid, jnp.clip(off[e_v] - tile_v * TE, 0, TE), 0)
    hi = jnp.where(valid, jnp.clip(end[e_v] - tile_v * TE, 0, TE), 0)
    prev_tile = jnp.concatenate([jnp.full((1,), -1, jnp.int32), tile_v[:-1]])
    first = (valid & (tile_v != prev_tile)).astype(jnp.int32)
    meta = (tile_v.astype(jnp.int32), e_v.astype(jnp.int32), lo.astype(jnp.int32), hi.astype(jnp.int32), first)
    return sorted_tok, sorted_gate.reshape(N_SLOTS, 1), pos_of_pair, meta


def _ln2_kernel(x1_ref, ya_ref, yb_ref, g_ref, b_ref, o_ref):
    o_ref[...] = _layer_norm(ALPHA * x1_ref[...] + (ya_ref[...] + yb_ref[...]), g_ref[...], b_ref[...])


def _ln2(x1, ya, yb, g, b):
    tile = pl.BlockSpec((TM, D_MODEL), lambda i: (i, 0))
    vec = pl.BlockSpec((1, D_MODEL), lambda i: (0, 0))
    return pl.pallas_call(
        _ln2_kernel,
        grid=(N_TILES,),
        in_specs=[tile, tile, tile, vec, vec],
        out_specs=tile,
        out_shape=jax.ShapeDtypeStruct((N_TOK, D_MODEL), F32),
        compiler_params=_cparams("arbitrary"),
        name="ln2",
    )(x1, ya, yb, g, b)


def _moe(x1, rinfo, totals, wg, wu, wd, g2, b2):
    sorted_tok, sorted_gate, pos_of_pair, meta = _dispatch(rinfo, totals)
    ys = _experts(meta, x1[sorted_tok], sorted_gate, wg, wu, wd)
    return _ln2(x1, ys[pos_of_pair[:, 0]], ys[pos_of_pair[:, 1]], g2, b2)


LOG2E = math.log2(math.e)
QSCALE = SCALE * LOG2E
AUG_ONE = HEAD_DIM
AUG_CK = HEAD_DIM + 3
L2_ONE = 48
C_GROUP = C_HEADS // C_KV


def _aug_constants():
    perm_k = np.zeros((256 + LANES, C_KV * LANES), np.float32)
    for kv in range(C_KV):
        for d in range(HEAD_DIM):
            perm_k[kv * HEAD_DIM + d, kv * LANES + d] = 1.0
        for p in range(3):
            perm_k[256 + L2_ONE, kv * LANES + AUG_ONE + p] = 1.0
            for g in range(C_GROUP):
                perm_k[256 + 16 * p + kv * C_GROUP + g, kv * LANES + AUG_CK + 4 * p + g] = 1.0
    perm_q = np.zeros((C_HEADS * HEAD_DIM, HEAD_DIM), np.float32)
    for h in range(C_HEADS):
        for p in range(3):
            perm_q[h * HEAD_DIM + p, 16 * p + h] = 1.0
            perm_q[h * HEAD_DIM + 3 + 4 * p + h % C_GROUP, L2_ONE] = -1.0
    return jnp.asarray(perm_k, BF16), jnp.asarray(perm_q, BF16)


def _split3(v):
    hi = v.astype(BF16).astype(F32)
    mid = (v - hi).astype(BF16).astype(F32)
    lo = (v - hi - mid).astype(BF16).astype(F32)
    return hi, mid, lo


def _split_layout(v, lane):
    hi, mid, lo = _split3(jnp.where(lane < C_HEADS, v, 0.0))
    return hi + pltpu.roll(mid, 16, 1) + pltpu.roll(lo, 32, 1)


def _odd_in_kernel(x_ref, wt_ref, wkf_ref, bf_ref, permk_ref, permq_ref,
                   qa_ref, ka_ref, ktp_ref, vtp_ref, lftp_ref, qts_ref, kts_ref, vts_ref, lfts_ref,
                   carry_ref, r_ref):
    i = pl.program_id(0)
    xb = x_ref[...].astype(BF16)
    nt_dims = (((1,), (1,)), ((), ()))
    hr = jnp.dot(xb, wkf_ref[...], preferred_element_type=F32)
    lane = lax.broadcasted_iota(jnp.int32, (TM, LANES), 1)
    z = hr[:, 256:384] + bf_ref[...]
    lf = -(jnp.maximum(-z, 0.0) + jnp.log1p(jnp.exp(-jnp.abs(z))))
    lf = jnp.where(lane < C_HEADS, lf, 0.0)
    lft = lf.T[0:C_HEADS, :]

    @pl.when(i < N_PTILES)
    def _():
        @pl.when(i % TILES_PER_SEQ == 0)
        def _():
            carry_ref[...] = jnp.zeros_like(carry_ref)
        ht = lax.dot_general(wt_ref[...], xb, nt_dims, preferred_element_type=F32)
        row = lax.broadcasted_iota(jnp.int32, (TM, TM), 0)
        col = lax.broadcasted_iota(jnp.int32, (TM, TM), 1)
        tri = jnp.where(row >= col, 1.0, 0.0).astype(BF16)
        r_ref[...] = jnp.dot(tri, _split_layout(lf, lane).astype(BF16), preferred_element_type=F32)
        r = r_ref[...]
        c = r + pltpu.roll(r, LANES - 16, 1) + pltpu.roll(r, LANES - 32, 1)
        c = jnp.where(lane < C_HEADS, c, 0.0) + carry_ref[0:1, :]
        carry_ref[0:1, :] = c[TM - 1:TM, :]
        l2 = _split_layout(c * LOG2E, lane) + jnp.where(lane == L2_ONE, 1.0, 0.0)
        kin = jnp.concatenate([hr[:, 0:256], l2], axis=1).astype(BF16)
        ka_ref[...] = jnp.dot(kin, permk_ref[...], preferred_element_type=F32).astype(BF16)
        l2t = l2.T[0:HEAD_DIM, :].astype(BF16)
        qextra = jnp.dot(permq_ref[...], l2t, preferred_element_type=F32)
        for h in range(C_HEADS):
            rows = slice(h * HEAD_DIM, (h + 1) * HEAD_DIM)
            qa_ref[h, 0:HEAD_DIM, :] = (ht[rows, :] * QSCALE).astype(BF16)
            qa_ref[h, HEAD_DIM:2 * HEAD_DIM, :] = qextra[rows, :].astype(BF16)
        ktp_ref[...] = ht[1024:1280, :]
        vtp_ref[...] = ht[1280:1536, :]
        lftp_ref[...] = lft

    @pl.when(i == N_PTILES)
    def _():
        ht = lax.dot_general(wt_ref[...], xb[0:DEC_BATCH, :], nt_dims, preferred_element_type=F32)
        qts_ref[...] = ht[0:1024, :]
        kts_ref[...] = ht[1024:1280, :]
        vts_ref[...] = ht[1280:1536, :]
        lfts_ref[...] = lft[:, 0:DEC_BATCH]


def _odd_in(x, wt_bf, wkf_bf, bf, perm_k, perm_q):
    full = lambda shape: pl.BlockSpec(shape, lambda i: (0,) * len(shape))
    bidx = lambda i: jnp.minimum(i // TILES_PER_SEQ, BATCH - 1)
    tidx = lambda i: jnp.where(i < N_PTILES, i % TILES_PER_SEQ, TILES_PER_SEQ - 1)
    tpose = lambda rows: pl.BlockSpec((None, rows, TM), lambda i: (bidx(i), 0, tidx(i)))
    return pl.pallas_call(
        _odd_in_kernel,
        grid=(N_TILES,),
        in_specs=[pl.BlockSpec((TM, D_MODEL), lambda i: (i, 0)), full((1536, D_MODEL)), full((D_MODEL, 384)),
                  full((1, LANES)), full((256 + LANES, C_KV * LANES)), full((C_HEADS * HEAD_DIM, HEAD_DIM))],
        out_specs=[pl.BlockSpec((None, C_HEADS, 2 * HEAD_DIM, TM), lambda i: (bidx(i), 0, 0, tidx(i))),
                   pl.BlockSpec((TM, C_KV * LANES), lambda i: (jnp.minimum(i, N_PTILES - 1), 0)),
                   tpose(256), tpose(256), tpose(C_HEADS),
                   full((1024, DEC_BATCH)), full((256, DEC_BATCH)), full((256, DEC_BATCH)),
                   full((C_HEADS, DEC_BATCH))],
        out_shape=[jax.ShapeDtypeStruct((BATCH, C_HEADS, 2 * HEAD_DIM, SEQ), BF16),
                   jax.ShapeDtypeStruct((N_PROMPT, C_KV * LANES), BF16),
                   jax.ShapeDtypeStruct((BATCH, 256, SEQ), F32),
                   jax.ShapeDtypeStruct((BATCH, 256, SEQ), F32),
                   jax.ShapeDtypeStruct((BATCH, C_HEADS, SEQ), F32),
                   jax.ShapeDtypeStruct((1024, DEC_BATCH), F32),
                   jax.ShapeDtypeStruct((256, DEC_BATCH), F32),
                   jax.ShapeDtypeStruct((256, DEC_BATCH), F32),
                   jax.ShapeDtypeStruct((C_HEADS, DEC_BATCH), F32)],
        scratch_shapes=[pltpu.VMEM((8, LANES), F32), pltpu.VMEM((TM, LANES), F32)],
        compiler_params=_cparams("arbitrary"),
        name="odd_in",
    )(x, wt_bf, wkf_bf, bf, perm_k, perm_q)


FT = 256
ACC_ROWS = HEAD_DIM + 16


def _fox_prompt_kernel(qa_ref, ka_ref, vt_ref, o_ref, *scratch):
    acc_refs = scratch[0:C_GROUP]
    st_refs = scratch[C_GROUP:2 * C_GROUP]
    srow = lax.broadcasted_iota(jnp.int32, (FT, FT), 0)
    tcol = lax.broadcasted_iota(jnp.int32, (FT, FT), 1)

    def q_tile(qi, _):
        t0 = pl.multiple_of(qi * FT, FT)
        qts = [qa_ref[g, :, pl.ds(t0, FT)] for g in range(C_GROUP)]
        for acc_ref in acc_refs:
            acc_ref[...] = jnp.zeros_like(acc_ref)

        def chunk(j, ms, masked):
            s0 = pl.multiple_of(j * FT, FT)
            ka = ka_ref[pl.ds(s0, FT), :]
            vt = jnp.concatenate([vt_ref[:, pl.ds(s0, FT)], jnp.ones((ACC_ROWS - HEAD_DIM, FT), F32)],
                                 axis=0).astype(BF16)
            bmax = []
            for g in range(C_GROUP):
                st = jnp.dot(ka, qts[g], preferred_element_type=F32)
                if masked:
                    st = jnp.where(srow <= tcol, st, NEG_INF)
                st_refs[g][...] = st
                bmax.append(jnp.max(st, axis=0, keepdims=True))
            out = []
            for g in range(C_GROUP):
                m_new = jnp.maximum(ms[g], bmax[g])
                a = jnp.exp2(ms[g] - m_new)
                p = jnp.exp2(st_refs[g][...] - m_new)
                acc_refs[g][...] = a * acc_refs[g][...] + jnp.dot(vt, p.astype(BF16), preferred_element_type=F32)
                out.append(m_new)
            return tuple(out)

        init = (jnp.full((1, FT), NEG_INF, F32),) * C_GROUP
        ms = lax.fori_loop(0, qi, lambda j, c: chunk(j, c, False), init)
        chunk(qi, ms, True)
        heads = []
        for g in range(C_GROUP):
            acc = acc_refs[g][...]
            heads.append(acc[0:HEAD_DIM, :] * (1.0 / acc[HEAD_DIM:HEAD_DIM + 1, :]))
        o_ref[pl.ds(t0, FT), :] = jnp.concatenate(heads, axis=0).T
        return 0

    lax.fori_loop(0, SEQ // FT, q_tile, 0)


def _fox_prompt(qa, ka, vtp):
    return pl.pallas_call(
        _fox_prompt_kernel,
        grid=(BATCH, C_KV),
        in_specs=[pl.BlockSpec((None, C_GROUP, 2 * HEAD_DIM, SEQ), lambda b, kv: (b, kv, 0, 0)),
                  pl.BlockSpec((SEQ, LANES), lambda b, kv: (b, kv)),
                  pl.BlockSpec((None, HEAD_DIM, SEQ), lambda b, kv: (b, kv, 0))],
        out_specs=pl.BlockSpec((SEQ, C_GROUP * HEAD_DIM), lambda b, kv: (b, kv)),
        out_shape=jax.ShapeDtypeStruct((N_TOK, C_HEADS * HEAD_DIM), F32),
        scratch_shapes=[pltpu.VMEM((ACC_ROWS, FT), F32)] * C_GROUP +[pltpu.VMEM((FT, FT), F32)] * C_GROUP,
        compiler_params=_cparams("arbitrary", "arbitrary"),
        name="fox_prompt",
    )(qa, ka, vtp)


PG = 8
N_PSTEPS = N_PAGES // PG
KVD = C_KV * HEAD_DIM


def _fox_sample_kernel(pt_ref, qbd_ref, kts_ref, vts_ref, lfts_ref, *refs):
    del pt_ref
    k_refs = refs[0:PG]
    v_refs = refs[PG:2 * PG]
    lf_refs = refs[2 * PG:3 * PG]
    o_ref, m_ref, l_ref, acc_ref, carry_ref = refs[3 * PG:]
    r = pl.program_id(0)
    j = pl.program_id(1)

    @pl.when(j == 0)
    def _():
        m_ref[...] = jnp.full_like(m_ref, NEG_INF)
        l_ref[...] = jnp.zeros_like(l_ref)
        acc_ref[...] = jnp.zeros_like(acc_ref)
        carry_ref[...] = jnp.zeros_like(carry_ref)

    qbd = qbd_ref[...].astype(BF16)

    def attend(s, vts):
        m_old = m_ref[:, 0:1]
        m_new = jnp.maximum(m_old, jnp.max(s, axis=-1, keepdims=True))
        a = jnp.exp(m_old - m_new)
        p = jnp.exp(s - m_new)
        l_ref[...] = jnp.broadcast_to(a * l_ref[:, 0:1] + jnp.sum(p, axis=-1, keepdims=True), l_ref.shape)
        m_ref[...] = jnp.broadcast_to(m_new, m_ref.shape)
        pv = None
        for g, vt in enumerate(vts):
            part = lax.dot_general(p[:, g * PAGE:(g + 1) * PAGE].astype(BF16), vt.astype(BF16),
                                   (((1,), (1,)), ((), ())), preferred_element_type=F32)
            pv = part if pv is None else pv + part
        acc_ref[...] = a * acc_ref[...] + pv

    lf_all = jnp.concatenate([lf_refs[g][...] for g in range(PG)], axis=0)
    rr = lax.broadcasted_iota(jnp.int32, (PAGE, PAGE), 0)
    cc = lax.broadcasted_iota(jnp.int32, (PAGE, PAGE), 1)
    upper = jnp.where(rr <= cc, 1.0, 0.0).astype(BF16)
    parts = jnp.concatenate(_split3(lf_all), axis=0).astype(BF16)
    c3 = jnp.dot(parts, upper, preferred_element_type=F32)
    n = PG * C_HEADS
    c_loc = c3[0:n, :] + c3[n:2 * n, :] + c3[2 * n:3 * n, :]
    carry = carry_ref[:, 0:1]
    s_parts = []
    for g in range(PG):
        c_g = c_loc[g * C_HEADS:(g + 1) * C_HEADS, :] + carry
        carry = c_g[:, PAGE - 1:PAGE]
        kt = k_refs[g][...].reshape(KVD, PAGE).astype(BF16)
        s_parts.append(jnp.dot(qbd, kt, preferred_element_type=F32) - c_g)
    carry_ref[...] = jnp.broadcast_to(carry, carry_ref.shape)
    attend(jnp.concatenate(s_parts, axis=1), [v_refs[g][...].reshape(KVD, PAGE) for g in range(PG)])

    @pl.when(j == N_PSTEPS - 1)
    def _():
        lane = lax.broadcasted_iota(jnp.int32, (C_HEADS, DEC_BATCH), 1)
        c_new = carry + lfts_ref[...]
        s_new = jnp.dot(qbd, kts_ref[...].astype(BF16), preferred_element_type=F32) - c_new
        s_new = jnp.where(lane == r, s_new, NEG_INF)
        attend(s_new, [vts_ref[...]])
        o_ref[...] = acc_ref[...] / l_ref[:, 0:1]


def _fox_sample(layer, page_flat, qbd, kts, vts, lfts, kc, vc, lfc):
    def page_spec(g, inner):
        nz = (0,) * len(inner)
        return pl.BlockSpec((None, None) + inner,
                            lambda r, j, pt: (layer, pt[r * N_PAGES + j * PG + g]) + nz)
    const = lambda shape: pl.BlockSpec(shape, lambda r, j, pt: (0,) * len(shape))
    grid_spec = pltpu.PrefetchScalarGridSpec(
        num_scalar_prefetch=1,
        grid=(DEC_BATCH, N_PSTEPS),
        in_specs=([pl.BlockSpec((None, C_HEADS, KVD), lambda r, j, pt: (r, 0, 0)),
                   const((KVD, DEC_BATCH)), const((KVD, DEC_BATCH)), const((C_HEADS, DEC_BATCH))]
                  + [page_spec(g, (C_KV, HEAD_DIM, PAGE)) for g in range(PG)]
                  + [page_spec(g, (C_KV, HEAD_DIM, PAGE)) for g in range(PG)]
                  + [page_spec(g, (C_HEADS, PAGE)) for g in range(PG)]),
        out_specs=pl.BlockSpec((None, C_HEADS, KVD), lambda r, j, pt: (r, 0, 0)),
        scratch_shapes=[pltpu.VMEM((C_HEADS, LANES), F32), pltpu.VMEM((C_HEADS, LANES), F32),
                        pltpu.VMEM((C_HEADS, KVD), F32), pltpu.VMEM((C_HEADS, LANES), F32)],
    )
    return pl.pallas_call(
        _fox_sample_kernel,
        grid_spec=grid_spec,
        out_shape=jax.ShapeDtypeStruct((DEC_BATCH, C_HEADS, KVD), F32),
        compiler_params=_cparams("arbitrary", "arbitrary"),
        name="fox_sample",
    )(page_flat, qbd, kts, vts, lfts, *([kc] * PG), *([vc] * PG), *([lfc] * PG))


def _t5_bucket(dist):
    n = jnp.maximum(dist, 0)
    max_exact = N_BUCKETS // 2
    nf = jnp.maximum(n, 1).astype(F32)
    large = max_exact + (jnp.log(nf / max_exact) / math.log(MAX_DISTANCE / max_exact)
                         * (N_BUCKETS - max_exact)).astype(jnp.int32)
    large = jnp.minimum(large, N_BUCKETS - 1)
    return jnp.where(n < max_exact, n, large)


def kernel(x_prompt, x_sample, cache_swa_k, cache_swa_v, state_conv, cache_fox_k, cache_fox_v, cache_fox_logf, page_table, rel_bias_table, attn_sinks, w_in_even, conv_w, w_out_even, w_in_odd, b_forget, w_out_odd, ln_g, ln_b, w_group, b_group, w_router, b_router, w_gate, w_up, w_down):
    x = jnp.concatenate([x_prompt.reshape(N_PROMPT, D_MODEL), x_sample.reshape(DEC_BATCH, D_MODEL),
                         jnp.zeros((N_TOK - N_PROMPT - DEC_BATCH, D_MODEL), F32)], axis=0)

    qi = jnp.arange(WINDOW)[:, None]
    kj = jnp.arange(2 * WINDOW)[None, :]
    bias_p = jnp.moveaxis(rel_bias_table[_t5_bucket(WINDOW + qi - kj)], -1, 0)
    bias_s = rel_bias_table[_t5_bucket(WINDOW - jnp.arange(WINDOW))].T
    bnew_b = jnp.broadcast_to(rel_bias_table[0][:, None], (A_HEADS, LANES))

    swa_kt = jnp.transpose(cache_swa_k, (0, 1, 3, 4, 2))
    swa_vt = jnp.transpose(cache_swa_v, (0, 1, 3, 4, 2))
    fox_kt = jnp.transpose(cache_fox_k, (0, 1, 3, 4, 2))
    fox_vt = jnp.transpose(cache_fox_v, (0, 1, 3, 4, 2))
    fox_lft = jnp.transpose(cache_fox_logf, (0, 1, 3, 2))
    page_flat = page_table.reshape(-1)

    wr = jnp.concatenate([w_router, w_group], axis=-1)
    wr_hi = wr.astype(BF16)
    wr_lo = (wr - wr_hi.astype(F32)).astype(BF16)
    zpad = lambda n: jnp.zeros((DEPTH, D_MODEL, n), BF16)
    wr_all = jnp.concatenate([wr_hi, zpad(12), wr_lo, zpad(LANES - 52)], axis=-1)
    br_all = jnp.concatenate([b_router, b_group, jnp.zeros((DEPTH, LANES - 20), F32)], axis=-1)
    wg_bf = w_gate.astype(BF16)
    wu_bf = w_up.astype(BF16)
    wd_bf = w_down.astype(BF16)
    eye = jnp.eye(C_KV, dtype=F32)
    perm_k, perm_q = _aug_constants()

    outs = {n: [] for n in ("swa_kp", "swa_vp", "swa_ks", "swa_vs", "conv_p", "conv_s",
                            "fox_kp", "fox_vp", "fox_lp", "fox_ks", "fox_vs", "fox_ls")}
    for layer in range(DEPTH):
        if layer % 2 == 0:
            e = layer // 2
            s0 = state_conv[e, :, 0, :]
            s1 = state_conv[e, :, 1, :]
            q, k, v, cv, convp, us = _even_in(x, w_in_even[e].astype(BF16), conv_w[e], s0, s1)
            sink_b = jnp.broadcast_to(attn_sinks[e][:, None], (A_HEADS, LANES))
            att = _swa_prompt(q, k, v, bias_p, sink_b)
            att = _swa_sample(att, q, k, v, swa_kt[e], swa_vt[e], bias_s, bnew_b, sink_b)
            mixes = [att, cv]
            w_out = w_out_even[e]
            last = lambda a: a[:N_PROMPT].reshape(BATCH, SEQ, 128)[:, SEQ - WINDOW:, :].reshape(
                BATCH, WINDOW, A_KV, HEAD_DIM)
            ks = k[N_PROMPT:N_PROMPT + DEC_BATCH].reshape(DEC_BATCH, 1, A_KV, HEAD_DIM)
            vs = v[N_PROMPT:N_PROMPT + DEC_BATCH].reshape(DEC_BATCH, 1, A_KV, HEAD_DIM)
            outs["swa_kp"].append(last(k))
            outs["swa_vp"].append(last(v))
            outs["swa_ks"].append(jnp.concatenate([cache_swa_k[e][:, 1:], ks], axis=1))
            outs["swa_vs"].append(jnp.concatenate([cache_swa_v[e][:, 1:], vs], axis=1))
            outs["conv_p"].append(convp)
            outs["conv_s"].append(jnp.stack([s1, us], axis=1))
        else:
            o = layer // 2
            wt_bf = jnp.transpose(w_in_odd[o][:, 0:1536]).astype(BF16)
            wkf_bf = jnp.concatenate([w_in_odd[o][:, 1024:1280], w_in_odd[o][:, 1536:1552],
                                      jnp.zeros((D_MODEL, 384 - 272), F32)], axis=-1).astype(BF16)
            bf_pad = jnp.concatenate([b_forget[o], jnp.zeros((LANES - C_HEADS,), F32)])[None, :]
            qa, ka, ktp, vtp, lftp, qts, kts, vts, lfts = _odd_in(x, wt_bf, wkf_bf, bf_pad, perm_k, perm_q)
            att = _fox_prompt(qa, ka, vtp)
            qs = qts.T.reshape(DEC_BATCH, C_KV, C_GROUP, 1, HEAD_DIM) * SCALE
            qbd = (qs * eye[None, :, None, :, None]).reshape(DEC_BATCH, C_HEADS, KVD)
            of = _fox_sample(o, page_flat, qbd, kts, vts, lfts, fox_kt, fox_vt, fox_lft)
            of = of.reshape(DEC_BATCH, C_KV, C_GROUP, C_KV, HEAD_DIM)
            att_s = jnp.sum(of * eye[None, :, None, :, None], axis=3).reshape(DEC_BATCH, C_HEADS * HEAD_DIM)
            att_s = jnp.concatenate([att_s, jnp.zeros((TM - DEC_BATCH, C_HEADS * HEAD_DIM), F32)], axis=0)
            att = lax.dynamic_update_slice(att, att_s, (N_PROMPT, 0))
            mixes = [att]
            w_out = w_out_odd[o]
            outs["fox_kp"].append(jnp.transpose(ktp.reshape(BATCH, C_KV, HEAD_DIM, SEQ), (0, 3, 1, 2)))
            outs["fox_vp"].append(jnp.transpose(vtp.reshape(BATCH, C_KV, HEAD_DIM, SEQ), (0, 3, 1, 2)))
            outs["fox_lp"].append(jnp.transpose(lftp, (0, 2, 1)))
            outs["fox_ks"].append(jnp.transpose(kts.reshape(C_KV, HEAD_DIM, DEC_BATCH), (2, 0, 1))[:, None])
            outs["fox_vs"].append(jnp.transpose(vts.reshape(C_KV, HEAD_DIM, DEC_BATCH), (2, 0, 1))[:, None])
            outs["fox_ls"].append(lfts.T[:, None, :])
        x1, rinfo, totals = _post_mix(mixes, x, w_out.astype(BF16), ln_g[layer, 0][None, :], ln_b[layer, 0][None, :],
                                      wr_all[layer], br_all[layer][None, :])
        x = _moe(x1, rinfo, totals, wg_bf[layer], wu_bf[layer], wd_bf[layer],
                 ln_g[layer, 1][None, :], ln_b[layer, 1][None, :])

    st = {n: jnp.stack(vl) for n, vl in outs.items()}
    y_prompt = x[:N_PROMPT].reshape(BATCH, SEQ, D_MODEL)
    y_sample = x[N_PROMPT:N_PROMPT + DEC_BATCH].reshape(DEC_BATCH, 1, D_MODEL)
    return (y_prompt, y_sample, st["swa_kp"], st["swa_vp"], st["swa_ks"], st["swa_vs"], st["conv_p"], st["conv_s"],
            st["fox_kp"], st["fox_vp"], st["fox_lp"], st["fox_ks"], st["fox_vs"], st["fox_ls"])
```

```python
import functools
import math

import jax
import jax.numpy as jnp
import numpy as np
from jax import lax
from jax.experimental import pallas as pl
from jax.experimental.pallas import tpu as pltpu

F32 = jnp.float32
BF16 = jnp.bfloat16

D_MODEL = 1024
BATCH = 4
SEQ = 4096
DEC_BATCH = 128
PAGE = 128
N_PAGES = 64
HEAD_DIM = 64
A_HEADS = 8
A_KV = 2
WINDOW = 128
B_WIDTH = 512
C_HEADS = 16
C_KV = 4
N_BUCKETS = 32
MAX_DISTANCE = 128
N_GROUPS = 4
N_EXPERTS = 16
D_EXPERT = 256
DEPTH = 4
ALPHA = (2.0 * DEPTH) ** 0.25
LN_EPS = 1e-5
NEG_INF = -1e30
SCALE = HEAD_DIM ** -0.5

N_PROMPT = BATCH * SEQ
TM = 512
N_PTILES = N_PROMPT // TM
TILES_PER_SEQ = SEQ // TM
N_TILES = N_PTILES + 1
N_TOK = N_TILES * TM
LANES = 128

TE = 256
N_SLOTS = 2 * N_TOK
N_VISITS = N_SLOTS // TE + N_EXPERTS - 1

VMEM_LIMIT = 56 * 1024 * 1024


def _cparams(*sem):
    return pltpu.CompilerParams(dimension_semantics=sem, vmem_limit_bytes=VMEM_LIMIT)


def _layer_norm(y, g, b):
    mu = jnp.mean(y, axis=-1, keepdims=True)
    yc = y - mu
    var = jnp.mean(yc * yc, axis=-1, keepdims=True)
    return yc * lax.rsqrt(var + LN_EPS) * g + b


def _even_in_kernel(x_ref, w_ref, cw_ref, s0_ref, s1_ref,
                    q_ref, k_ref, v_ref, cv_ref, convp_ref, us_ref, carry_ref):
    i = pl.program_id(0)
    h = jnp.dot(x_ref[...].astype(BF16), w_ref[...], preferred_element_type=F32)
    q_ref[...] = h[:, 0:512]
    k_ref[...] = h[:, 512:640]
    v_ref[...] = h[:, 640:768]
    bg = h[:, 768:1280]
    u = h[:, 1280:1792] * h[:, 1792:2304]
    w0 = cw_ref[0:1, :]
    w1 = cw_ref[1:2, :]
    w2 = cw_ref[2:3, :]

    @pl.when(i < N_PTILES)
    def _():
        @pl.when(i % TILES_PER_SEQ == 0)
        def _():
            carry_ref[...] = jnp.zeros_like(carry_ref)
        row = lax.broadcasted_iota(jnp.int32, (TM, B_WIDTH), 0)
        c2 = carry_ref[0:1, :]
        c1 = carry_ref[1:2, :]
        u1 = jnp.where(row == 0, c1, pltpu.roll(u, 1, 0))
        u2 = jnp.where(row == 0, c2, jnp.where(row == 1, c1, pltpu.roll(u, 2, 0)))
        cv_ref[...] = bg * (w0 * u2 + w1 * u1 + w2 * u)
        carry_ref[0:2, :] = u[TM - 2:TM, :]
        convp_ref[...] = u[TM - 2:TM, :]

    @pl.when(i == N_PTILES)
    def _():
        us = u[0:DEC_BATCH, :]
        z = w0 * s0_ref[...] + w1 * s1_ref[...] + w2 * us
        cv_ref[0:DEC_BATCH, :] = bg[0:DEC_BATCH, :] * z
        cv_ref[DEC_BATCH:TM, :] = jnp.zeros((TM - DEC_BATCH, B_WIDTH), F32)
        us_ref[...] = us


def _even_in(x, w_bf, cw, s0, s1):
    tile = lambda n: pl.BlockSpec((TM, n), lambda i: (i, 0))
    full = lambda shape: pl.BlockSpec(shape, lambda i: (0,) * len(shape))
    return pl.pallas_call(
        _even_in_kernel,
        grid=(N_TILES,),
        in_specs=[tile(D_MODEL), full((D_MODEL, 2304)), full((3, B_WIDTH)),
                  full((DEC_BATCH, B_WIDTH)), full((DEC_BATCH, B_WIDTH))],
        out_specs=[tile(512), tile(128), tile(128), tile(B_WIDTH),
                   pl.BlockSpec((None, 2, B_WIDTH),
                                lambda i: (jnp.minimum(i // TILES_PER_SEQ, BATCH - 1), 0, 0)),
                   full((DEC_BATCH, B_WIDTH))],
        out_shape=[jax.ShapeDtypeStruct((N_TOK, 512), F32),
                   jax.ShapeDtypeStruct((N_TOK, 128), F32),
                   jax.ShapeDtypeStruct((N_TOK, 128), F32),
                   jax.ShapeDtypeStruct((N_TOK, B_WIDTH), F32),
                   jax.ShapeDtypeStruct((BATCH, 2, B_WIDTH), F32),
                   jax.ShapeDtypeStruct((DEC_BATCH, B_WIDTH), F32)],
        scratch_shapes=[pltpu.VMEM((8, B_WIDTH), F32)],
        compiler_params=_cparams("arbitrary"),
        name="even_in",
    )(x, w_bf, cw, s0, s1)


def _swa_prompt_kernel(q_ref, kc_ref, kp_ref, vc_ref, vp_ref, bias_ref, sink_ref, o_ref, *st_refs):
    qt = (q_ref[...] * SCALE).T
    kk = jnp.concatenate([kp_ref[...], kc_ref[...]], axis=0).astype(BF16)
    vvt = jnp.concatenate([vp_ref[...], vc_ref[...]], axis=0).T
    zeros = jnp.zeros((HEAD_DIM, WINDOW), F32)
    ones = jnp.ones((SWA_ACC_ROWS - HEAD_DIM, 2 * WINDOW), F32)
    cmax = []
    for h in range(A_HEADS):
        kv = h // (A_HEADS // A_KV)
        qh = qt[h * HEAD_DIM:(h + 1) * HEAD_DIM, :]
        qh = jnp.concatenate([qh, zeros] if kv == 0 else [zeros, qh], axis=0).astype(BF16)
        st = jnp.dot(kk, qh, preferred_element_type=F32) + bias_ref[h]
        st_refs[h][...] = st
        cmax.append(jnp.max(st, axis=0, keepdims=True))
    outs = []
    for h in range(A_HEADS):
        kv = h // (A_HEADS // A_KV)
        sk = sink_ref[h:h + 1, :]
        m = jnp.maximum(cmax[h], sk)
        p = jnp.exp(st_refs[h][...] - m)
        vt = jnp.concatenate([vvt[kv * HEAD_DIM:(kv + 1) * HEAD_DIM, :], ones], axis=0).astype(BF16)
        acc = jnp.dot(vt, p.astype(BF16), preferred_element_type=F32)
        den = acc[HEAD_DIM:HEAD_DIM + 1, :] + jnp.exp(sk - m)
        outs.append(acc[0:HEAD_DIM, :] / den)
    o_ref[...] = jnp.concatenate(outs, axis=0).T


def _swa_prompt(q, k, v, bias, sink_b):
    nblk = SEQ // WINDOW
    cur = lambda b, j: (b * nblk + j, 0)
    prev = lambda b, j: (jnp.maximum(b * nblk + j - 1, 0), 0)
    return pl.pallas_call(
        _swa_prompt_kernel,
        grid=(BATCH, nblk),
        in_specs=[pl.BlockSpec((WINDOW, 512), cur),
                  pl.BlockSpec((WINDOW, 128), cur), pl.BlockSpec((WINDOW, 128), prev),
                  pl.BlockSpec((WINDOW, 128), cur), pl.BlockSpec((WINDOW, 128), prev),
                  pl.BlockSpec((None, A_HEADS, 2 * WINDOW, WINDOW), lambda b, j: (jnp.minimum(j, 1), 0, 0, 0)),
                  pl.BlockSpec((A_HEADS, LANES), lambda b, j: (0, 0))],
        out_specs=pl.BlockSpec((WINDOW, 512), cur),
        out_shape=jax.ShapeDtypeStruct((N_PROMPT, 512), F32),
        scratch_shapes=[pltpu.VMEM((2 * WINDOW, WINDOW), F32)] * A_HEADS,
        compiler_params=_cparams("arbitrary", "arbitrary"),
        name="swa_prompt",
    )(q, k, k, v, v, bias, sink_b)


SWA_RB = 8
SWA_ACC_ROWS = HEAD_DIM + 16


def _swa_sample_kernel(q_ref, k_ref, v_ref, kt_ref, vt_ref, bias_ref, bnew_ref, sink_ref, o_ref):

    @pl.when(pl.program_id(0) >= DEC_BATCH // SWA_RB)
    def _():
        o_ref[...] = jnp.zeros_like(o_ref)

    @pl.when(pl.program_id(0) < DEC_BATCH // SWA_RB)
    def _():
        _swa_sample_rows(q_ref, k_ref, v_ref, kt_ref, vt_ref, bias_ref, bnew_ref, sink_ref, o_ref)


def _swa_sample_rows(q_ref, k_ref, v_ref, kt_ref, vt_ref, bias_ref, bnew_ref, sink_ref, o_ref):
    lane = lax.broadcasted_iota(jnp.int32, (4, WINDOW), 1)
    grp = A_HEADS // A_KV
    for r in range(SWA_RB):
        for kv in range(A_KV):
            hs = slice(kv * grp, (kv + 1) * grp)
            q4 = jnp.concatenate(
                [q_ref[r:r + 1, (kv * grp + g) * HEAD_DIM:(kv * grp + g + 1) * HEAD_DIM] for g in range(grp)],
                axis=0)
            kt = kt_ref[r, kv]
            vt = vt_ref[r, kv]
            s_old = jnp.dot(q4.astype(BF16), kt.astype(BF16), preferred_element_type=F32)
            s_old = s_old * SCALE + bias_ref[hs, :]
            s_old = jnp.where(lane >= 1, s_old, NEG_INF)
            kn = k_ref[r:r + 1, kv * HEAD_DIM:(kv + 1) * HEAD_DIM]
            vn = v_ref[r:r + 1, kv * HEAD_DIM:(kv + 1) * HEAD_DIM]
            s_new = jnp.sum(q4 * kn, axis=-1, keepdims=True) * SCALE + bnew_ref[hs, 0:1]
            sk = sink_ref[hs, 0:1]
            m = jnp.maximum(jnp.maximum(jnp.max(s_old, axis=-1, keepdims=True), s_new), sk)
            p_old = jnp.exp(s_old - m)
            p_new = jnp.exp(s_new - m)
            den = jnp.sum(p_old, axis=-1, keepdims=True) + p_new + jnp.exp(sk - m)
            o = lax.dot_general(p_old.astype(BF16), vt.astype(BF16), (((1,), (1,)), ((), ())),
                                preferred_element_type=F32)
            o = (o + p_new * vn) / den
            for g in range(grp):
                hh = kv * grp + g
                o_ref[r:r + 1, hh * HEAD_DIM:(hh + 1) * HEAD_DIM] = o[g:g + 1, :]


def _swa_sample(q, k, v, kt, vt, bias_s, bnew_b, sink_b):
    base = N_PROMPT // SWA_RB
    rows = lambda n: pl.BlockSpec((SWA_RB, n), lambda i: (base + i, 0))
    cache = pl.BlockSpec((SWA_RB, A_KV, HEAD_DIM, WINDOW),
                         lambda i: (jnp.minimum(i, DEC_BATCH // SWA_RB - 1), 0, 0, 0))
    small = pl.BlockSpec((A_HEADS, LANES), lambda i: (0, 0))
    return pl.pallas_call(
        _swa_sample_kernel,
        grid=(TM // SWA_RB,),
        in_specs=[rows(512), rows(128), rows(128), cache, cache, small, small, small],
        out_specs=pl.BlockSpec((SWA_RB, 512), lambda i: (i, 0)),
        out_shape=jax.ShapeDtypeStruct((TM, 512), F32),
        compiler_params=_cparams("arbitrary"),
        name="swa_sample",
    )(q, k, v, kt, vt, bias_s, bnew_b, sink_b)


def _route(x1, wr_ref, br_ref, cnt_ref):
    x_hi = x1.astype(BF16)
    x_lo = (x1 - x_hi.astype(F32)).astype(BF16)
    pa = jnp.dot(x_hi, wr_ref[...], preferred_element_type=F32)
    pb = jnp.dot(x_lo, wr_ref[...], preferred_element_type=F32)
    logits = pa + pltpu.roll(pa, LANES - 32, 1) + pb + br_ref[...]
    lane = lax.broadcasted_iota(jnp.int32, logits.shape, 1)
    lane_f = lane.astype(F32)
    lane_grp = (lane >> 2).astype(F32)
    is_grp = (lane >= N_EXPERTS) & (lane < N_EXPERTS + N_GROUPS)
    big = 1e9
    gl = jnp.where(is_grp, logits, NEG_INF)
    gmax = jnp.max(gl, axis=-1, keepdims=True)
    gidx = jnp.min(jnp.where(is_grp & (logits == gmax), lane_f - N_EXPERTS, big), axis=-1, keepdims=True)
    gsum = jnp.sum(jnp.where(is_grp, jnp.exp(gl - gmax), 0.0), axis=-1, keepdims=True)
    grp_w = 1.0 / gsum
    in_grp = (lane < N_EXPERTS) & (lane_grp == gidx)
    e1 = jnp.where(in_grp, logits, NEG_INF)
    t1 = jnp.max(e1, axis=-1, keepdims=True)
    i1 = jnp.min(jnp.where(in_grp & (logits == t1), lane_f, big), axis=-1, keepdims=True)
    rest = in_grp & (lane_f != i1)
    e2 = jnp.where(rest, logits, NEG_INF)
    t2 = jnp.max(e2, axis=-1, keepdims=True)
    i2 = jnp.min(jnp.where(rest & (logits == t2), lane_f, big), axis=-1, keepdims=True)
    ex = jnp.exp(t2 - t1)
    g1 = grp_w / (1.0 + ex)
    g2 = grp_w * ex / (1.0 + ex)
    sel1 = lane_f == i1
    sel2 = lane_f == i2
    onehot = jnp.where(sel1 | sel2, 1.0, 0.0)
    rr = lax.broadcasted_iota(jnp.int32, (TM, TM), 0)
    cc = lax.broadcasted_iota(jnp.int32, (TM, TM), 1)
    below = jnp.where(rr > cc, 1.0, 0.0).astype(BF16)
    before = jnp.dot(below, onehot.astype(BF16), preferred_element_type=F32) + cnt_ref[0:1, :]
    rank1 = jnp.sum(jnp.where(sel1, before, 0.0), axis=-1, keepdims=True)
    rank2 = jnp.sum(jnp.where(sel2, before, 0.0), axis=-1, keepdims=True)
    cnt_ref[0:1, :] = cnt_ref[0:1, :] + jnp.sum(onehot, axis=0, keepdims=True)
    out = jnp.where(lane == 0, i1, 0.0)
    out = jnp.where(lane == 1, i2, out)
    out = jnp.where(lane == 2, g1, out)
    out = jnp.where(lane == 3, g2, out)
    out = jnp.where(lane == 4, rank1, out)
    out = jnp.where(lane == 5, rank2, out)
    return out


def _post_mix_kernel(has_conv, *refs):
    n_in = 3 if has_conv else 2
    att_p_ref, att_s_ref = refs[0:2]
    x_ref, w_ref, g_ref, b_ref, wr_ref, br_ref, x1_ref, r_ref, tot_ref, cnt_ref = refs[n_in:]
    i = pl.program_id(0)

    @pl.when(i == 0)
    def _():
        cnt_ref[...] = jnp.zeros_like(cnt_ref)
    att = jnp.where(i < N_PTILES, att_p_ref[...], att_s_ref[...]).astype(BF16)
    kw = att.shape[1]
    mix = jnp.dot(att, w_ref[0:kw, :], preferred_element_type=F32)
    if has_conv:
        mix = mix + jnp.dot(refs[2][...].astype(BF16), w_ref[kw:D_MODEL, :], preferred_element_type=F32)
    x1 = _layer_norm(ALPHA * x_ref[...] + mix, g_ref[...], b_ref[...])
    x1_ref[...] = x1
    r_ref[...] = _route(x1, wr_ref, br_ref, cnt_ref)
    tot_ref[...] = cnt_ref[...]


def _post_mix(att_p, att_s, conv, x, w_bf, g, b, wr_bf, br):
    kw = att_p.shape[1]
    tile = lambda n: pl.BlockSpec((TM, n), lambda i: (i, 0))
    full = lambda shape: pl.BlockSpec(shape, lambda i: (0,) * len(shape))
    mix_specs = [pl.BlockSpec((TM, kw), lambda i: (jnp.minimum(i, N_PTILES - 1), 0)), full((TM, kw))]
    mixes = [att_p, att_s]
    if conv is not None:
        mix_specs.append(tile(D_MODEL - kw))
        mixes.append(conv)
    return pl.pallas_call(
        functools.partial(_post_mix_kernel, conv is not None),
        grid=(N_TILES,),
        in_specs=mix_specs + [tile(D_MODEL), full((D_MODEL, D_MODEL)), full((1, D_MODEL)),
                              full((1, D_MODEL)), full((D_MODEL, LANES)), full((1, LANES))],
        out_specs=[tile(D_MODEL), tile(LANES), full((8, LANES))],
        out_shape=[jax.ShapeDtypeStruct((N_TOK, D_MODEL), F32),
                   jax.ShapeDtypeStruct((N_TOK, LANES), F32),
                   jax.ShapeDtypeStruct((8, LANES), F32)],
        scratch_shapes=[pltpu.VMEM((8, LANES), F32)],
        compiler_params=_cparams("arbitrary"),
        name="post_mix",
    )(*mixes, x, w_bf, g, b, wr_bf, br)


def _expert_kernel(vt_ref, ve_ref, lo_ref, hi_ref, first_ref, xs_ref, gate_ref, wg_ref, wu_ref, wd_ref, ys_ref):
    del vt_ref, ve_ref
    v = pl.program_id(0)
    lo = lo_ref[v]
    hi = hi_ref[v]

    @pl.when(hi > lo)
    def _():
        row = lax.broadcasted_iota(jnp.int32, (TE, 1), 0)
        gate = jnp.where((row >= lo) & (row < hi), gate_ref[...], 0.0)
        xs = xs_ref[...].astype(BF16)
        g = jnp.dot(xs, wg_ref[...].astype(BF16), preferred_element_type=F32)
        u = jnp.dot(xs, wu_ref[...].astype(BF16), preferred_element_type=F32)
        h = g * (1.0 / (1.0 + jnp.exp(-g))) * u * gate
        y = jnp.dot(h.astype(BF16), wd_ref[...].astype(BF16), preferred_element_type=F32)

        @pl.when(first_ref[v] == 1)
        def _():
            ys_ref[...] = y

        @pl.when(first_ref[v] == 0)
        def _():
            ys_ref[...] += y


def _experts(meta, xs, gate_sorted, wg, wu, wd):
    tile_map = lambda v, vt, ve, lo, hi, fi: (vt[v], 0)
    w_map = lambda v, vt, ve, lo, hi, fi: (ve[v], 0, 0)
    grid_spec = pltpu.PrefetchScalarGridSpec(
        num_scalar_prefetch=5,
        grid=(N_VISITS,),
        in_specs=[pl.BlockSpec((TE, D_MODEL), tile_map),
                  pl.BlockSpec((TE, 1), tile_map),
                  pl.BlockSpec((None, D_MODEL, D_EXPERT), w_map),
                  pl.BlockSpec((None, D_MODEL, D_EXPERT), w_map),
                  pl.BlockSpec((None, D_EXPERT, D_MODEL), w_map)],
        out_specs=pl.BlockSpec((TE, D_MODEL), tile_map),
    )
    return pl.pallas_call(
        _expert_kernel,
        grid_spec=grid_spec,
        out_shape=jax.ShapeDtypeStruct((N_SLOTS, D_MODEL), F32),
        compiler_params=_cparams("arbitrary"),
        name="experts",
    )(*meta, xs, gate_sorted, wg, wu, wd)


def _dispatch(rinfo, totals):
    ids = rinfo[:, 0:2].astype(jnp.int32)
    ranks = rinfo[:, 4:6].astype(jnp.int32)
    counts = totals[0, 0:N_EXPERTS].astype(jnp.int32)
    end = jnp.cumsum(counts)
    off = end - counts
    pos_of_pair = off[ids] + ranks
    tok = lax.broadcasted_iota(jnp.int32, (N_TOK, 2), 0)
    _, sorted_tok, sorted_gate = lax.sort((ids.reshape(-1), tok.reshape(-1), rinfo[:, 2:4].reshape(-1)),
                                          num_keys=1, is_stable=True)
    first_tile = off // TE
    n_vis = jnp.where(counts > 0, (end - 1) // TE - first_tile + 1, 0)
    v_end = jnp.cumsum(n_vis)
    v_start = v_end - n_vis
    v = jnp.arange(N_VISITS, dtype=jnp.int32)
    valid = v < v_end[-1]
    e_v = jnp.minimum(jnp.sum((v[:, None] >= v_end[None, :]).astype(jnp.int32), axis=1), N_EXPERTS - 1)
    tile_v = jnp.where(valid, first_tile[e_v] + v - v_start[e_v], N_SLOTS // TE - 1)
    lo = jnp.where(valid, jnp.clip(off[e_v] - tile_v * TE, 0, TE), 0)
    hi = jnp.where(valid, jnp.clip(end[e_v] - tile_v * TE, 0, TE), 0)
    prev_tile = jnp.concatenate([jnp.full((1,), -1, jnp.int32), tile_v[:-1]])
    first = (valid & (tile_v != prev_tile)).astype(jnp.int32)
    meta = (tile_v.astype(jnp.int32), e_v.astype(jnp.int32), lo.astype(jnp.int32), hi.astype(jnp.int32), first)
    return sorted_tok, sorted_gate.reshape(N_SLOTS, 1), pos_of_pair, meta


def _ln2_kernel(x1_ref, ya_ref, yb_ref, g_ref, b_ref, o_ref):
    o_ref[...] = _layer_norm(ALPHA * x1_ref[...] + (ya_ref[...] + yb_ref[...]), g_ref[...], b_ref[...])


def _ln2(x1, ya, yb, g, b):
    tile = pl.BlockSpec((TM, D_MODEL), lambda i: (i, 0))
    vec = pl.BlockSpec((1, D_MODEL), lambda i: (0, 0))
    return pl.pallas_call(
        _ln2_kernel,
        grid=(N_TILES,),
        in_specs=[tile, tile, tile, vec, vec],
        out_specs=tile,
        out_shape=jax.ShapeDtypeStruct((N_TOK, D_MODEL), F32),
        compiler_params=_cparams("arbitrary"),
        name="ln2",
    )(x1, ya, yb, g, b)


def _moe(x1, rinfo, totals, wg, wu, wd, g2, b2):
    sorted_tok, sorted_gate, pos_of_pair, meta = _dispatch(rinfo, totals)
    ys = _experts(meta, x1[sorted_tok], sorted_gate, wg, wu, wd)
    return _ln2(x1, ys[pos_of_pair[:, 0]], ys[pos_of_pair[:, 1]], g2, b2)


LOG2E = math.log2(math.e)
QSCALE = SCALE * LOG2E
AUG_ONE = HEAD_DIM
AUG_CK = HEAD_DIM + 3
L2_ONE = 48
C_GROUP = C_HEADS // C_KV


def _aug_constants():
    perm_k = np.zeros((256 + LANES, C_KV * LANES), np.float32)
    for kv in range(C_KV):
        for d in range(HEAD_DIM):
            perm_k[kv * HEAD_DIM + d, kv * LANES + d] = 1.0
        for p in range(3):
            perm_k[256 + L2_ONE, kv * LANES + AUG_ONE + p] = 1.0
            for g in range(C_GROUP):
                perm_k[256 + 16 * p + kv * C_GROUP + g, kv * LANES + AUG_CK + 4 * p + g] = 1.0
    perm_q = np.zeros((C_HEADS * HEAD_DIM, HEAD_DIM), np.float32)
    for h in range(C_HEADS):
        for p in range(3):
            perm_q[h * HEAD_DIM + p, 16 * p + h] = 1.0
            perm_q[h * HEAD_DIM + 3 + 4 * p + h % C_GROUP, L2_ONE] = -1.0
    return jnp.asarray(perm_k, BF16), jnp.asarray(perm_q, BF16)


def _split3(v):
    hi = v.astype(BF16).astype(F32)
    mid = (v - hi).astype(BF16).astype(F32)
    lo = (v - hi - mid).astype(BF16).astype(F32)
    return hi, mid, lo


def _split_layout(v, lane):
    hi, mid, lo = _split3(jnp.where(lane < C_HEADS, v, 0.0))
    return hi + pltpu.roll(mid, 16, 1) + pltpu.roll(lo, 32, 1)


def _odd_in_kernel(x_ref, wt_ref, wkf_ref, bf_ref, permk_ref, permq_ref,
                   qa_ref, ka_ref, ktp_ref, vtp_ref, lftp_ref, qts_ref, kts_ref, vts_ref, lfts_ref,
                   carry_ref, r_ref):
    i = pl.program_id(0)
    xb = x_ref[...].astype(BF16)
    nt_dims = (((1,), (1,)), ((), ()))
    hr = jnp.dot(xb, wkf_ref[...], preferred_element_type=F32)
    lane = lax.broadcasted_iota(jnp.int32, (TM, LANES), 1)
    z = hr[:, 256:384] + bf_ref[...]
    lf = -(jnp.maximum(-z, 0.0) + jnp.log1p(jnp.exp(-jnp.abs(z))))
    lf = jnp.where(lane < C_HEADS, lf, 0.0)
    lft = lf.T[0:C_HEADS, :]

    @pl.when(i < N_PTILES)
    def _():
        @pl.when(i % TILES_PER_SEQ == 0)
        def _():
            carry_ref[...] = jnp.zeros_like(carry_ref)
        ht = lax.dot_general(wt_ref[...], xb, nt_dims, preferred_element_type=F32)
        row = lax.broadcasted_iota(jnp.int32, (TM, TM), 0)
        col = lax.broadcasted_iota(jnp.int32, (TM, TM), 1)
        tri = jnp.where(row >= col, 1.0, 0.0).astype(BF16)
        r_ref[...] = jnp.dot(tri, _split_layout(lf, lane).astype(BF16), preferred_element_type=F32)
        r = r_ref[...]
        c = r + pltpu.roll(r, LANES - 16, 1) + pltpu.roll(r, LANES - 32, 1)
        c = jnp.where(lane < C_HEADS, c, 0.0) + carry_ref[0:1, :]
        carry_ref[0:1, :] = c[TM - 1:TM, :]
        l2 = _split_layout(c * LOG2E, lane) + jnp.where(lane == L2_ONE, 1.0, 0.0)
        kin = jnp.concatenate([hr[:, 0:256], l2], axis=1).astype(BF16)
        ka_ref[...] = jnp.dot(kin, permk_ref[...], preferred_element_type=F32).astype(BF16)
        l2t = l2.T[0:HEAD_DIM, :].astype(BF16)
        qextra = jnp.dot(permq_ref[...], l2t, preferred_element_type=F32)
        for h in range(C_HEADS):
            rows = slice(h * HEAD_DIM, (h + 1) * HEAD_DIM)
            qa_ref[h, 0:HEAD_DIM, :] = (ht[rows, :] * QSCALE).astype(BF16)
            qa_ref[h, HEAD_DIM:2 * HEAD_DIM, :] = qextra[rows, :].astype(BF16)
        ktp_ref[...] = ht[1024:1280, :]
        vtp_ref[...] = ht[1280:1536, :]
        lftp_ref[...] = lft

    @pl.when(i == N_PTILES)
    def _():
        ht = lax.dot_general(wt_ref[...], xb[0:DEC_BATCH, :], nt_dims, preferred_element_type=F32)
        qts_ref[...] = ht[0:1024, :]
        kts_ref[...] = ht[1024:1280, :]
        vts_ref[...] = ht[1280:1536, :]
        lfts_ref[...] = lft[:, 0:DEC_BATCH]


def _odd_in(x, wt_bf, wkf_bf, bf, perm_k, perm_q):
    full = lambda shape: pl.BlockSpec(shape, lambda i: (0,) * len(shape))
    bidx = lambda i: jnp.minimum(i // TILES_PER_SEQ, BATCH - 1)
    tidx = lambda i: jnp.where(i < N_PTILES, i % TILES_PER_SEQ, TILES_PER_SEQ - 1)
    tpose = lambda rows: pl.BlockSpec((None, rows, TM), lambda i: (bidx(i), 0, tidx(i)))
    return pl.pallas_call(
        _odd_in_kernel,
        grid=(N_TILES,),
        in_specs=[pl.BlockSpec((TM, D_MODEL), lambda i: (i, 0)), full((1536, D_MODEL)), full((D_MODEL, 384)),
                  full((1, LANES)), full((256 + LANES, C_KV * LANES)), full((C_HEADS * HEAD_DIM, HEAD_DIM))],
        out_specs=[pl.BlockSpec((None, C_HEADS, 2 * HEAD_DIM, TM), lambda i: (bidx(i), 0, 0, tidx(i))),
                   pl.BlockSpec((TM, C_KV * LANES), lambda i: (jnp.minimum(i, N_PTILES - 1), 0)),
                   tpose(256), tpose(256), tpose(C_HEADS),
                   full((1024, DEC_BATCH)), full((256, DEC_BATCH)), full((256, DEC_BATCH)),
                   full((C_HEADS, DEC_BATCH))],
        out_shape=[jax.ShapeDtypeStruct((BATCH, C_HEADS, 2 * HEAD_DIM, SEQ), BF16),
                   jax.ShapeDtypeStruct((N_PROMPT, C_KV * LANES), BF16),
                   jax.ShapeDtypeStruct((BATCH, 256, SEQ), F32),
                   jax.ShapeDtypeStruct((BATCH, 256, SEQ), F32),
                   jax.ShapeDtypeStruct((BATCH, C_HEADS, SEQ), F32),
                   jax.ShapeDtypeStruct((1024, DEC_BATCH), F32),
                   jax.ShapeDtypeStruct((256, DEC_BATCH), F32),
                   jax.ShapeDtypeStruct((256, DEC_BATCH), F32),
                   jax.ShapeDtypeStruct((C_HEADS, DEC_BATCH), F32)],
        scratch_shapes=[pltpu.VMEM((8, LANES), F32), pltpu.VMEM((TM, LANES), F32)],
        compiler_params=_cparams("arbitrary"),
        name="odd_in",
    )(x, wt_bf, wkf_bf, bf, perm_k, perm_q)


FT = 256
ACC_ROWS = HEAD_DIM + 16


def _fox_prompt_kernel(qa_ref, ka_ref, vt_ref, o_ref, *scratch):
    acc_refs = scratch[0:C_GROUP]
    st_refs = scratch[C_GROUP:2 * C_GROUP]
    srow = lax.broadcasted_iota(jnp.int32, (FT, FT), 0)
    tcol = lax.broadcasted_iota(jnp.int32, (FT, FT), 1)

    def q_tile(qi, _):
        t0 = pl.multiple_of(qi * FT, FT)
        qts = [qa_ref[g, :, pl.ds(t0, FT)] for g in range(C_GROUP)]
        for acc_ref in acc_refs:
            acc_ref[...] = jnp.zeros_like(acc_ref)

        def chunk(s0, nk, ms, masked):
            ka = ka_ref[pl.ds(s0, nk), :]
            vt = jnp.concatenate([vt_ref[:, pl.ds(s0, nk)], jnp.ones((ACC_ROWS - HEAD_DIM, nk), F32)],
                                 axis=0).astype(BF16)
            bmax = []
            for g in range(C_GROUP):
                st = jnp.dot(ka, qts[g], preferred_element_type=F32)
                if masked:
                    st = jnp.where(srow <= tcol, st, NEG_INF)
                st_refs[g][0:nk, :] = st
                bmax.append(jnp.max(st, axis=0, keepdims=True))
            out = []
            for g in range(C_GROUP):
                m_new = jnp.maximum(ms[g], bmax[g])
                a = jnp.exp2(ms[g] - m_new)
                p = jnp.exp2(st_refs[g][0:nk, :] - m_new)
                acc_refs[g][...] = a * acc_refs[g][...] + jnp.dot(vt, p.astype(BF16), preferred_element_type=F32)
                out.append(m_new)
            return tuple(out)

        init = (jnp.full((1, FT), NEG_INF, F32),) * C_GROUP
        n2 = qi // 2
        ms = lax.fori_loop(0, n2, lambda j, c: chunk(pl.multiple_of(j * 2 * FT, 2 * FT), 2 * FT, c, False), init)
        s1 = pl.multiple_of(n2 * 2 * FT, FT)
        ms = lax.fori_loop(0, qi % 2, lambda j, c: chunk(s1, FT, c, False), ms)
        chunk(t0, FT, ms, True)
        heads = []
        for g in range(C_GROUP):
            acc = acc_refs[g][...]
            heads.append(acc[0:HEAD_DIM, :] * (1.0 / acc[HEAD_DIM:HEAD_DIM + 1, :]))
        o_ref[pl.ds(t0, FT), :] = jnp.concatenate(heads, axis=0).T
        return 0

    lax.fori_loop(0, SEQ // FT, q_tile, 0)


def _fox_prompt(qa, ka, vtp):
    return pl.pallas_call(
        _fox_prompt_kernel,
        grid=(BATCH, C_KV),
        in_specs=[pl.BlockSpec((None, C_GROUP, 2 * HEAD_DIM, SEQ), lambda b, kv: (b, kv, 0, 0)),
                  pl.BlockSpec((SEQ, LANES), lambda b, kv: (b, kv)),
                  pl.BlockSpec((None, HEAD_DIM, SEQ), lambda b, kv: (b, kv, 0))],
        out_specs=pl.BlockSpec((SEQ, C_GROUP * HEAD_DIM), lambda b, kv: (b, kv)),
        out_shape=jax.ShapeDtypeStruct((N_PROMPT, C_HEADS * HEAD_DIM), F32),
        scratch_shapes=[pltpu.VMEM((ACC_ROWS, FT), F32)] * C_GROUP + [pltpu.VMEM((2 * FT, FT), F32)] * C_GROUP,
        compiler_params=_cparams("arbitrary", "arbitrary"),
        name="fox_prompt",
    )(qa, ka, vtp)


PG = 8
KVD = C_KV * HEAD_DIM


def _fox_sample_kernel(layer, pt_ref, qbd_ref, kts_ref, vts_ref, lfts_ref, kc_hbm, vc_hbm, lfc_hbm,
                       o_ref, kbuf, vbuf, lfbuf, s_ref, off_ref, sem):
    r = pl.program_id(0)
    slot = r % 2
    caches = ((kc_hbm, kbuf), (vc_hbm, vbuf), (lfc_hbm, lfbuf))

    def start_fetch(req, to_slot):
        def body(p, _):
            page = pt_ref[req * N_PAGES + p]
            for t, (src, dst) in enumerate(caches):
                pltpu.make_async_copy(src.at[layer, page], dst.at[to_slot, p], sem.at[t, to_slot]).start()
            return 0
        lax.fori_loop(0, N_PAGES, body, 0, unroll=4)

    @pl.when(r == 0)
    def _():
        start_fetch(0, 0)

    @pl.when(r + 1 < DEC_BATCH)
    def _():
        start_fetch(r + 1, 1 - slot)

    for t, (src, dst) in enumerate(caches):
        pltpu.make_async_copy(src.at[layer, pl.ds(0, N_PAGES)], dst.at[slot], sem.at[t, slot]).wait()

    qbd = qbd_ref[...].astype(BF16)
    rr = lax.broadcasted_iota(jnp.int32, (PAGE, PAGE), 0)
    cc = lax.broadcasted_iota(jnp.int32, (PAGE, PAGE), 1)
    upper = jnp.where(rr <= cc, 1.0, 0.0).astype(BF16)
    n = PG * C_HEADS
    rows = N_PAGES * C_HEADS

    tot = jnp.sum(lfbuf[slot].reshape(rows, PAGE), axis=-1, keepdims=True)
    incl = tot
    shift = C_HEADS
    while shift < rows:
        incl = incl + jnp.concatenate([jnp.zeros((shift, 1), F32), incl[0:rows - shift, :]], axis=0)
        shift *= 2
    off_ref[...] = jnp.broadcast_to(incl - tot, (rows, PAGE))
    total = incl[rows - C_HEADS:rows, :]

    def score_group(j, m):
        p0 = pl.multiple_of(j * PG, PG)
        lf_all = lfbuf[slot, pl.ds(p0, PG)].reshape(n, PAGE)
        parts = jnp.concatenate(_split3(lf_all), axis=0).astype(BF16)
        c3 = jnp.dot(parts, upper, preferred_element_type=F32)
        c_all = c3[0:n, :] + c3[n:2 * n, :] + c3[2 * n:3 * n, :] + off_ref[pl.ds(pl.multiple_of(j * n, n), n), :]
        s_parts = []
        for g in range(PG):
            kt = kbuf[slot, p0 + g].astype(BF16)
            s_parts.append(jnp.dot(qbd, kt, preferred_element_type=F32) - c_all[g * C_HEADS:(g + 1) * C_HEADS, :])
        s = jnp.concatenate(s_parts, axis=1)
        s_ref[:, pl.ds(pl.multiple_of(j * PG * PAGE, PG * PAGE), PG * PAGE)] = s
        return jnp.maximum(m, jnp.max(s, axis=-1, keepdims=True))

    m = lax.fori_loop(0, N_PAGES // PG, score_group, jnp.full((C_HEADS, 1), NEG_INF, F32))
    lane = lax.broadcasted_iota(jnp.int32, (C_HEADS, DEC_BATCH), 1)
    s_new = jnp.dot(qbd, kts_ref[...].astype(BF16), preferred_element_type=F32) - (total + lfts_ref[...])
    s_new = jnp.where(lane == r, s_new, NEG_INF)
    m = jnp.maximum(m, jnp.max(s_new, axis=-1, keepdims=True))

    def value_group(j, c):
        l, acc_t = c
        p0 = pl.multiple_of(j * PG, PG)
        p = jnp.exp(s_ref[:, pl.ds(pl.multiple_of(j * PG * PAGE, PG * PAGE), PG * PAGE)] - m)
        vt = jnp.concatenate([vbuf[slot, p0 + g] for g in range(PG)], axis=1).astype(BF16)
        acc_t = acc_t + jnp.dot(vt, p.T.astype(BF16), preferred_element_type=F32)
        return l + jnp.sum(p, axis=-1, keepdims=True), acc_t

    p_new = jnp.exp(s_new - m)
    init = (jnp.sum(p_new, axis=-1, keepdims=True),
            jnp.dot(vts_ref[...].astype(BF16), p_new.T.astype(BF16), preferred_element_type=F32))
    l, acc_t = lax.fori_loop(0, N_PAGES // PG, value_group, init)
    o_ref[...] = acc_t.T / l


def _fox_sample(layer, page_flat, qbd, kts, vts, lfts, kc, vc, lfc):
    const = lambda shape: pl.BlockSpec(shape, lambda r, pt: (0,) * len(shape))
    hbm = pl.BlockSpec(memory_space=pl.ANY)
    grid_spec = pltpu.PrefetchScalarGridSpec(
        num_scalar_prefetch=1,
        grid=(DEC_BATCH,),
        in_specs=[pl.BlockSpec((None, C_HEADS, KVD), lambda r, pt: (r, 0, 0)),
                  const((KVD, DEC_BATCH)), const((KVD, DEC_BATCH)), const((C_HEADS, DEC_BATCH)), hbm, hbm, hbm],
        out_specs=pl.BlockSpec((None, C_HEADS, KVD), lambda r, pt: (r, 0, 0)),
        scratch_shapes=[pltpu.VMEM((2, N_PAGES, KVD, PAGE), F32), pltpu.VMEM((2, N_PAGES, KVD, PAGE), F32),
                        pltpu.VMEM((2, N_PAGES, C_HEADS, PAGE), F32),
                        pltpu.VMEM((C_HEADS, (N_PAGES + 1) * PAGE), F32),
                        pltpu.VMEM((N_PAGES * C_HEADS, PAGE), F32),
                        pltpu.SemaphoreType.DMA((3, 2))],
    )
    return pl.pallas_call(
        functools.partial(_fox_sample_kernel, layer),
        grid_spec=grid_spec,
        out_shape=jax.ShapeDtypeStruct((DEC_BATCH, C_HEADS, KVD), F32),
        compiler_params=pltpu.CompilerParams(dimension_semantics=("arbitrary",), vmem_limit_bytes=VMEM_LIMIT,
                                             disable_bounds_checks=True),
        name="fox_sample",
    )(page_flat, qbd, kts, vts, lfts, kc, vc, lfc)


def _t5_bucket(dist):
    n = jnp.maximum(dist, 0)
    max_exact = N_BUCKETS // 2
    nf = jnp.maximum(n, 1).astype(F32)
    large = max_exact + (jnp.log(nf / max_exact) / math.log(MAX_DISTANCE / max_exact)
                         * (N_BUCKETS - max_exact)).astype(jnp.int32)
    large = jnp.minimum(large, N_BUCKETS - 1)
    return jnp.where(n < max_exact, n, large)


def kernel(x_prompt, x_sample, cache_swa_k, cache_swa_v, state_conv, cache_fox_k, cache_fox_v, cache_fox_logf, page_table, rel_bias_table, attn_sinks, w_in_even, conv_w, w_out_even, w_in_odd, b_forget, w_out_odd, ln_g, ln_b, w_group, b_group, w_router, b_router, w_gate, w_up, w_down):
    x = jnp.concatenate([x_prompt.reshape(N_PROMPT, D_MODEL), x_sample.reshape(DEC_BATCH, D_MODEL),
                         jnp.zeros((N_TOK - N_PROMPT - DEC_BATCH, D_MODEL), F32)], axis=0)

    qi = jnp.arange(WINDOW)[:, None]
    kj = jnp.arange(2 * WINDOW)[None, :]
    dist = WINDOW + qi - kj
    band = (dist >= 0) & (dist < WINDOW)
    onehot = (_t5_bucket(dist)[:, :, None] == jnp.arange(N_BUCKETS)[None, None, :]).astype(F32)
    bias_t = jnp.einsum('qkb,bh->hkq', onehot, rel_bias_table, precision=lax.Precision.HIGHEST)
    bias_p = jnp.stack([jnp.where((band & (kj >= WINDOW)).T[None], bias_t, NEG_INF),
                        jnp.where(band.T[None], bias_t, NEG_INF)])
    bias_s = rel_bias_table[_t5_bucket(WINDOW - jnp.arange(WINDOW))].T
    bnew_b = jnp.broadcast_to(rel_bias_table[0][:, None], (A_HEADS, LANES))

    swa_kt = jnp.transpose(cache_swa_k, (0, 1, 3, 4, 2))
    swa_vt = jnp.transpose(cache_swa_v, (0, 1, 3, 4, 2))
    pool = cache_fox_k.shape[1]
    fox_kt = jnp.transpose(cache_fox_k, (0, 1, 3, 4, 2)).reshape(DEPTH // 2, pool, KVD, PAGE)
    fox_vt = jnp.transpose(cache_fox_v, (0, 1, 3, 4, 2)).reshape(DEPTH // 2, pool, KVD, PAGE)
    fox_lft = jnp.transpose(cache_fox_logf, (0, 1, 3, 2))
    page_flat = page_table.reshape(-1)

    wr = jnp.concatenate([w_router, w_group], axis=-1)
    wr_hi = wr.astype(BF16)
    wr_lo = (wr - wr_hi.astype(F32)).astype(BF16)
    zpad = lambda n: jnp.zeros((DEPTH, D_MODEL, n), BF16)
    wr_all = jnp.concatenate([wr_hi, zpad(12), wr_lo, zpad(LANES - 52)], axis=-1)
    br_all = jnp.concatenate([b_router, b_group, jnp.zeros((DEPTH, LANES - 20), F32)], axis=-1)
    eye = jnp.eye(C_KV, dtype=F32)
    perm_k, perm_q = _aug_constants()

    outs = {n: [] for n in ("swa_kp", "swa_vp", "swa_ks", "swa_vs", "conv_p", "conv_s",
                            "fox_kp", "fox_vp", "fox_lp", "fox_ks", "fox_vs", "fox_ls")}
    for layer in range(DEPTH):
        if layer % 2 == 0:
            e = layer // 2
            s0 = state_conv[e, :, 0, :]
            s1 = state_conv[e, :, 1, :]
            q, k, v, cv, convp, us = _even_in(x, w_in_even[e].astype(BF16), conv_w[e], s0, s1)
            sink_b = jnp.broadcast_to(attn_sinks[e][:, None], (A_HEADS, LANES))
            att_p = _swa_prompt(q, k, v, bias_p, sink_b)
            att_s = _swa_sample(q, k, v, swa_kt[e], swa_vt[e], bias_s, bnew_b, sink_b)
            conv = cv
            w_out = w_out_even[e]
            last = lambda a: a[:N_PROMPT].reshape(BATCH, SEQ, 128)[:, SEQ - WINDOW:, :].reshape(
                BATCH, WINDOW, A_KV, HEAD_DIM)
            ks = k[N_PROMPT:N_PROMPT + DEC_BATCH].reshape(DEC_BATCH, 1, A_KV, HEAD_DIM)
            vs = v[N_PROMPT:N_PROMPT + DEC_BATCH].reshape(DEC_BATCH, 1, A_KV, HEAD_DIM)
            outs["swa_kp"].append(last(k))
            outs["swa_vp"].append(last(v))
            outs["swa_ks"].append(jnp.concatenate([cache_swa_k[e][:, 1:], ks], axis=1))
            outs["swa_vs"].append(jnp.concatenate([cache_swa_v[e][:, 1:], vs], axis=1))
            outs["conv_p"].append(convp)
            outs["conv_s"].append(jnp.stack([s1, us], axis=1))
        else:
            o = layer // 2
            wt_bf = jnp.transpose(w_in_odd[o][:, 0:1536]).astype(BF16)
            wkf_bf = jnp.concatenate([w_in_odd[o][:, 1024:1280], w_in_odd[o][:, 1536:1552],
                                      jnp.zeros((D_MODEL, 384 - 272), F32)], axis=-1).astype(BF16)
            bf_pad = jnp.concatenate([b_forget[o], jnp.zeros((LANES - C_HEADS,), F32)])[None, :]
            qa, ka, ktp, vtp, lftp, qts, kts, vts, lfts = _odd_in(x, wt_bf, wkf_bf, bf_pad, perm_k, perm_q)
            att_p = _fox_prompt(qa, ka, vtp)
            qs = qts.T.reshape(DEC_BATCH, C_KV, C_GROUP, 1, HEAD_DIM) * SCALE
            qbd = (qs * eye[None, :, None, :, None]).reshape(DEC_BATCH, C_HEADS, KVD)
            of = _fox_sample(o, page_flat, qbd, kts, vts, lfts, fox_kt, fox_vt, fox_lft)
            of = of.reshape(DEC_BATCH, C_KV, C_GROUP, C_KV, HEAD_DIM)
            att_s = jnp.sum(of * eye[None, :, None, :, None], axis=3).reshape(DEC_BATCH, C_HEADS * HEAD_DIM)
            att_s = jnp.concatenate([att_s, jnp.zeros((TM - DEC_BATCH, C_HEADS * HEAD_DIM), F32)], axis=0)
            conv = None
            w_out = w_out_odd[o]
            outs["fox_kp"].append(jnp.transpose(ktp.reshape(BATCH, C_KV, HEAD_DIM, SEQ), (0, 3, 1, 2)))
            outs["fox_vp"].append(jnp.transpose(vtp.reshape(BATCH, C_KV, HEAD_DIM, SEQ), (0, 3, 1, 2)))
            outs["fox_lp"].append(jnp.transpose(lftp, (0, 2, 1)))
            outs["fox_ks"].append(jnp.transpose(kts.reshape(C_KV, HEAD_DIM, DEC_BATCH), (2, 0, 1))[:, None])
            outs["fox_vs"].append(jnp.transpose(vts.reshape(C_KV, HEAD_DIM, DEC_BATCH), (2, 0, 1))[:, None])
            outs["fox_ls"].append(lfts.T[:, None, :])
        x1, rinfo, totals = _post_mix(att_p, att_s, conv, x, w_out.astype(BF16), ln_g[layer, 0][None, :], ln_b[layer, 0][None, :],
                                      wr_all[layer], br_all[layer][None, :])
        x = _moe(x1, rinfo, totals, w_gate[layer], w_up[layer], w_down[layer],
                 ln_g[layer, 1][None, :], ln_b[layer, 1][None, :])

    st = {n: jnp.stack(vl) for n, vl in outs.items()}
    y_prompt = x[:N_PROMPT].reshape(BATCH, SEQ, D_MODEL)
    y_sample = x[N_PROMPT:N_PROMPT + DEC_BATCH].reshape(DEC_BATCH, 1, D_MODEL)
    return (y_prompt, y_sample, st["swa_kp"], st["swa_vp"], st["swa_ks"], st["swa_vs"], st["conv_p"], st["conv_s"],
            st["fox_kp"], st["fox_vp"], st["fox_lp"], st["fox_ks"], st["fox_vs"], st["fox_ls"])
```

```python
import functools
import math

import jax
import jax.numpy as jnp
import numpy as np
from jax import lax
from jax.experimental import pallas as pl
from jax.experimental.pallas import tpu as pltpu

F32 = jnp.float32
BF16 = jnp.bfloat16

D_MODEL = 1024
BATCH = 4
SEQ = 4096
DEC_BATCH = 128
PAGE = 128
N_PAGES = 64
HEAD_DIM = 64
A_HEADS = 8
A_KV = 2
WINDOW = 128
B_WIDTH = 512
C_HEADS = 16
C_KV = 4
N_BUCKETS = 32
MAX_DISTANCE = 128
N_GROUPS = 4
N_EXPERTS = 16
D_EXPERT = 256
DEPTH = 4
ALPHA = (2.0 * DEPTH) ** 0.25
LN_EPS = 1e-5
NEG_INF = -1e30
SCALE = HEAD_DIM ** -0.5

N_PROMPT = BATCH * SEQ
TM = 512
N_PTILES = N_PROMPT // TM
TILES_PER_SEQ = SEQ // TM
N_TILES = N_PTILES + 1
N_TOK = N_TILES * TM
LANES = 128

TE = 512
TE_SUB = 256
N_SLOTS = 2 * N_TOK
N_VISITS = N_SLOTS // TE + N_EXPERTS - 1

VMEM_LIMIT = 56 * 1024 * 1024


def _cparams(*sem):
    return pltpu.CompilerParams(dimension_semantics=sem, vmem_limit_bytes=VMEM_LIMIT)


def _layer_norm(y, g, b):
    mu = jnp.mean(y, axis=-1, keepdims=True)
    yc = y - mu
    var = jnp.mean(yc * yc, axis=-1, keepdims=True)
    return yc * lax.rsqrt(var + LN_EPS) * g + b


def _even_in_kernel(x_ref, w_ref, cw_ref, s0_ref, s1_ref,
                    q_ref, k_ref, v_ref, cv_ref, convp_ref, us_ref, carry_ref):
    i = pl.program_id(0)
    h = jnp.dot(x_ref[...].astype(BF16), w_ref[...], preferred_element_type=F32)
    q_ref[...] = h[:, 0:512]
    k_ref[...] = h[:, 512:640]
    v_ref[...] = h[:, 640:768]
    bg = h[:, 768:1280]
    u = h[:, 1280:1792] * h[:, 1792:2304]
    w0 = cw_ref[0:1, :]
    w1 = cw_ref[1:2, :]
    w2 = cw_ref[2:3, :]

    @pl.when(i < N_PTILES)
    def _():
        @pl.when(i % TILES_PER_SEQ == 0)
        def _():
            carry_ref[...] = jnp.zeros_like(carry_ref)
        row = lax.broadcasted_iota(jnp.int32, (TM, B_WIDTH), 0)
        c2 = carry_ref[0:1, :]
        c1 = carry_ref[1:2, :]
        u1 = jnp.where(row == 0, c1, pltpu.roll(u, 1, 0))
        u2 = jnp.where(row == 0, c2, jnp.where(row == 1, c1, pltpu.roll(u, 2, 0)))
        cv_ref[...] = bg * (w0 * u2 + w1 * u1 + w2 * u)
        carry_ref[0:2, :] = u[TM - 2:TM, :]
        convp_ref[...] = u[TM - 2:TM, :]

    @pl.when(i == N_PTILES)
    def _():
        us = u[0:DEC_BATCH, :]
        z = w0 * s0_ref[...] + w1 * s1_ref[...] + w2 * us
        cv_ref[0:DEC_BATCH, :] = bg[0:DEC_BATCH, :] * z
        cv_ref[DEC_BATCH:TM, :] = jnp.zeros((TM - DEC_BATCH, B_WIDTH), F32)
        us_ref[...] = us


def _even_in(x, w_bf, cw, s0, s1):
    tile = lambda n: pl.BlockSpec((TM, n), lambda i: (i, 0))
    full = lambda shape: pl.BlockSpec(shape, lambda i: (0,) * len(shape))
    return pl.pallas_call(
        _even_in_kernel,
        grid=(N_TILES,),
        in_specs=[tile(D_MODEL), full((D_MODEL, 2304)), full((3, B_WIDTH)),
                  full((DEC_BATCH, B_WIDTH)), full((DEC_BATCH, B_WIDTH))],
        out_specs=[tile(512), tile(128), tile(128), tile(B_WIDTH),
                   pl.BlockSpec((None, 2, B_WIDTH),
                                lambda i: (jnp.minimum(i // TILES_PER_SEQ, BATCH - 1), 0, 0)),
                   full((DEC_BATCH, B_WIDTH))],
        out_shape=[jax.ShapeDtypeStruct((N_TOK, 512), F32),
                   jax.ShapeDtypeStruct((N_TOK, 128), F32),
                   jax.ShapeDtypeStruct((N_TOK, 128), F32),
                   jax.ShapeDtypeStruct((N_TOK, B_WIDTH), F32),
                   jax.ShapeDtypeStruct((BATCH, 2, B_WIDTH), F32),
                   jax.ShapeDtypeStruct((DEC_BATCH, B_WIDTH), F32)],
        scratch_shapes=[pltpu.VMEM((8, B_WIDTH), F32)],
        compiler_params=_cparams("arbitrary"),
        name="even_in",
    )(x, w_bf, cw, s0, s1)


def _swa_prompt_kernel(q_ref, kc_ref, kp_ref, vc_ref, vp_ref, bias_ref, sink_ref, o_ref, *st_refs):
    qt = (q_ref[...] * SCALE).T
    kk = jnp.concatenate([kp_ref[...], kc_ref[...]], axis=0).astype(BF16)
    vvt = jnp.concatenate([vp_ref[...], vc_ref[...]], axis=0).T
    zeros = jnp.zeros((HEAD_DIM, WINDOW), F32)
    ones = jnp.ones((SWA_ACC_ROWS - HEAD_DIM, 2 * WINDOW), F32)
    cmax = []
    for h in range(A_HEADS):
        kv = h // (A_HEADS // A_KV)
        qh = qt[h * HEAD_DIM:(h + 1) * HEAD_DIM, :]
        qh = jnp.concatenate([qh, zeros] if kv == 0 else [zeros, qh], axis=0).astype(BF16)
        st = jnp.dot(kk, qh, preferred_element_type=F32) + bias_ref[h]
        st_refs[h][...] = st
        cmax.append(jnp.max(st, axis=0, keepdims=True))
    outs = []
    for h in range(A_HEADS):
        kv = h // (A_HEADS // A_KV)
        sk = sink_ref[h:h + 1, :]
        m = jnp.maximum(cmax[h], sk)
        p = jnp.exp(st_refs[h][...] - m)
        vt = jnp.concatenate([vvt[kv * HEAD_DIM:(kv + 1) * HEAD_DIM, :], ones], axis=0).astype(BF16)
        acc = jnp.dot(vt, p.astype(BF16), preferred_element_type=F32)
        den = acc[HEAD_DIM:HEAD_DIM + 1, :] + jnp.exp(sk - m)
        outs.append(acc[0:HEAD_DIM, :] / den)
    o_ref[...] = jnp.concatenate(outs, axis=0).T


def _swa_prompt(q, k, v, bias, sink_b):
    nblk = SEQ // WINDOW
    cur = lambda b, j: (b * nblk + j, 0)
    prev = lambda b, j: (jnp.maximum(b * nblk + j - 1, 0), 0)
    return pl.pallas_call(
        _swa_prompt_kernel,
        grid=(BATCH, nblk),
        in_specs=[pl.BlockSpec((WINDOW, 512), cur),
                  pl.BlockSpec((WINDOW, 128), cur), pl.BlockSpec((WINDOW, 128), prev),
                  pl.BlockSpec((WINDOW, 128), cur), pl.BlockSpec((WINDOW, 128), prev),
                  pl.BlockSpec((None, A_HEADS, 2 * WINDOW, WINDOW), lambda b, j: (jnp.minimum(j, 1), 0, 0, 0)),
                  pl.BlockSpec((A_HEADS, LANES), lambda b, j: (0, 0))],
        out_specs=pl.BlockSpec((WINDOW, 512), cur),
        out_shape=jax.ShapeDtypeStruct((N_PROMPT, 512), F32),
        scratch_shapes=[pltpu.VMEM((2 * WINDOW, WINDOW), F32)] * A_HEADS,
        compiler_params=_cparams("arbitrary", "arbitrary"),
        name="swa_prompt",
    )(q, k, k, v, v, bias, sink_b)


SWA_RB = 8
SWA_ACC_ROWS = HEAD_DIM + 16


def _swa_sample_kernel(q_ref, k_ref, v_ref, kt_ref, vt_ref, bias_ref, bnew_ref, sink_ref, o_ref):

    @pl.when(pl.program_id(0) >= DEC_BATCH // SWA_RB)
    def _():
        o_ref[...] = jnp.zeros_like(o_ref)

    @pl.when(pl.program_id(0) < DEC_BATCH // SWA_RB)
    def _():
        _swa_sample_rows(q_ref, k_ref, v_ref, kt_ref, vt_ref, bias_ref, bnew_ref, sink_ref, o_ref)


def _swa_sample_rows(q_ref, k_ref, v_ref, kt_ref, vt_ref, bias_ref, bnew_ref, sink_ref, o_ref):
    lane = lax.broadcasted_iota(jnp.int32, (4, WINDOW), 1)
    grp = A_HEADS // A_KV
    for r in range(SWA_RB):
        for kv in range(A_KV):
            hs = slice(kv * grp, (kv + 1) * grp)
            q4 = jnp.concatenate(
                [q_ref[r:r + 1, (kv * grp + g) * HEAD_DIM:(kv * grp + g + 1) * HEAD_DIM] for g in range(grp)],
                axis=0)
            kt = kt_ref[r, kv]
            vt = vt_ref[r, kv]
            s_old = jnp.dot(q4.astype(BF16), kt.astype(BF16), preferred_element_type=F32)
            s_old = s_old * SCALE + bias_ref[hs, :]
            s_old = jnp.where(lane >= 1, s_old, NEG_INF)
            kn = k_ref[r:r + 1, kv * HEAD_DIM:(kv + 1) * HEAD_DIM]
            vn = v_ref[r:r + 1, kv * HEAD_DIM:(kv + 1) * HEAD_DIM]
            s_new = jnp.sum(q4 * kn, axis=-1, keepdims=True) * SCALE + bnew_ref[hs, 0:1]
            sk = sink_ref[hs, 0:1]
            m = jnp.maximum(jnp.maximum(jnp.max(s_old, axis=-1, keepdims=True), s_new), sk)
            p_old = jnp.exp(s_old - m)
            p_new = jnp.exp(s_new - m)
            den = jnp.sum(p_old, axis=-1, keepdims=True) + p_new + jnp.exp(sk - m)
            o = lax.dot_general(p_old.astype(BF16), vt.astype(BF16), (((1,), (1,)), ((), ())),
                                preferred_element_type=F32)
            o = (o + p_new * vn) / den
            for g in range(grp):
                hh = kv * grp + g
                o_ref[r:r + 1, hh * HEAD_DIM:(hh + 1) * HEAD_DIM] = o[g:g + 1, :]


def _swa_sample(q, k, v, kt, vt, bias_s, bnew_b, sink_b):
    base = N_PROMPT // SWA_RB
    rows = lambda n: pl.BlockSpec((SWA_RB, n), lambda i: (base + i, 0))
    cache = pl.BlockSpec((SWA_RB, A_KV, HEAD_DIM, WINDOW),
                         lambda i: (jnp.minimum(i, DEC_BATCH // SWA_RB - 1), 0, 0, 0))
    small = pl.BlockSpec((A_HEADS, LANES), lambda i: (0, 0))
    return pl.pallas_call(
        _swa_sample_kernel,
        grid=(TM // SWA_RB,),
        in_specs=[rows(512), rows(128), rows(128), cache, cache, small, small, small],
        out_specs=pl.BlockSpec((SWA_RB, 512), lambda i: (i, 0)),
        out_shape=jax.ShapeDtypeStruct((TM, 512), F32),
        compiler_params=_cparams("arbitrary"),
        name="swa_sample",
    )(q, k, v, kt, vt, bias_s, bnew_b, sink_b)


def _route(x1, wr_ref, br_ref, cnt_ref):
    x_hi = x1.astype(BF16)
    x_lo = (x1 - x_hi.astype(F32)).astype(BF16)
    pa = jnp.dot(x_hi, wr_ref[...], preferred_element_type=F32)
    pb = jnp.dot(x_lo, wr_ref[...], preferred_element_type=F32)
    logits = pa + pltpu.roll(pa, LANES - 32, 1) + pb + br_ref[...]
    lane = lax.broadcasted_iota(jnp.int32, logits.shape, 1)
    lane_f = lane.astype(F32)
    lane_grp = (lane >> 2).astype(F32)
    is_grp = (lane >= N_EXPERTS) & (lane < N_EXPERTS + N_GROUPS)
    big = 1e9
    gl = jnp.where(is_grp, logits, NEG_INF)
    gmax = jnp.max(gl, axis=-1, keepdims=True)
    gidx = jnp.min(jnp.where(is_grp & (logits == gmax), lane_f - N_EXPERTS, big), axis=-1, keepdims=True)
    gsum = jnp.sum(jnp.where(is_grp, jnp.exp(gl - gmax), 0.0), axis=-1, keepdims=True)
    grp_w = 1.0 / gsum
    in_grp = (lane < N_EXPERTS) & (lane_grp == gidx)
    e1 = jnp.where(in_grp, logits, NEG_INF)
    t1 = jnp.max(e1, axis=-1, keepdims=True)
    i1 = jnp.min(jnp.where(in_grp & (logits == t1), lane_f, big), axis=-1, keepdims=True)
    rest = in_grp & (lane_f != i1)
    e2 = jnp.where(rest, logits, NEG_INF)
    t2 = jnp.max(e2, axis=-1, keepdims=True)
    i2 = jnp.min(jnp.where(rest & (logits == t2), lane_f, big), axis=-1, keepdims=True)
    ex = jnp.exp(t2 - t1)
    g1 = grp_w / (1.0 + ex)
    g2 = grp_w * ex / (1.0 + ex)
    sel1 = lane_f == i1
    sel2 = lane_f == i2
    onehot = jnp.where(sel1 | sel2, 1.0, 0.0)
    rr = lax.broadcasted_iota(jnp.int32, (TM, TM), 0)
    cc = lax.broadcasted_iota(jnp.int32, (TM, TM), 1)
    below = jnp.where(rr > cc, 1.0, 0.0).astype(BF16)
    before = jnp.dot(below, onehot.astype(BF16), preferred_element_type=F32) + cnt_ref[0:1, :]
    rank1 = jnp.sum(jnp.where(sel1, before, 0.0), axis=-1, keepdims=True)
    rank2 = jnp.sum(jnp.where(sel2, before, 0.0), axis=-1, keepdims=True)
    cnt_ref[0:1, :] = cnt_ref[0:1, :] + jnp.sum(onehot, axis=0, keepdims=True)
    out = jnp.where(lane == 0, i1, 0.0)
    out = jnp.where(lane == 1, i2, out)
    out = jnp.where(lane == 2, g1, out)
    out = jnp.where(lane == 3, g2, out)
    out = jnp.where(lane == 4, rank1, out)
    out = jnp.where(lane == 5, rank2, out)
    return out


def _post_mix_kernel(has_conv, *refs):
    n_in = 3 if has_conv else 2
    att_p_ref, att_s_ref = refs[0:2]
    x_ref, w_ref, g_ref, b_ref, wr_ref, br_ref, x1_ref, r_ref, tot_ref, cnt_ref = refs[n_in:]
    i = pl.program_id(0)

    @pl.when(i == 0)
    def _():
        cnt_ref[...] = jnp.zeros_like(cnt_ref)
    att = jnp.where(i < N_PTILES, att_p_ref[...], att_s_ref[...]).astype(BF16)
    kw = att.shape[1]
    mix = jnp.dot(att, w_ref[0:kw, :], preferred_element_type=F32)
    if has_conv:
        mix = mix + jnp.dot(refs[2][...].astype(BF16), w_ref[kw:D_MODEL, :], preferred_element_type=F32)
    x1 = _layer_norm(ALPHA * x_ref[...] + mix, g_ref[...], b_ref[...])
    x1_ref[...] = x1
    r_ref[...] = _route(x1, wr_ref, br_ref, cnt_ref)
    tot_ref[...] = cnt_ref[...]


def _post_mix(att_p, att_s, conv, x, w_bf, g, b, wr_bf, br):
    kw = att_p.shape[1]
    tile = lambda n: pl.BlockSpec((TM, n), lambda i: (i, 0))
    full = lambda shape: pl.BlockSpec(shape, lambda i: (0,) * len(shape))
    mix_specs = [pl.BlockSpec((TM, kw), lambda i: (jnp.minimum(i, N_PTILES - 1), 0)), full((TM, kw))]
    mixes = [att_p, att_s]
    if conv is not None:
        mix_specs.append(tile(D_MODEL - kw))
        mixes.append(conv)
    return pl.pallas_call(
        functools.partial(_post_mix_kernel, conv is not None),
        grid=(N_TILES,),
        in_specs=mix_specs + [tile(D_MODEL), full((D_MODEL, D_MODEL)), full((1, D_MODEL)),
                              full((1, D_MODEL)), full((D_MODEL, LANES)), full((1, LANES))],
        out_specs=[tile(D_MODEL), tile(LANES), full((8, LANES))],
        out_shape=[jax.ShapeDtypeStruct((N_TOK, D_MODEL), F32),
                   jax.ShapeDtypeStruct((N_TOK, LANES), F32),
                   jax.ShapeDtypeStruct((8, LANES), F32)],
        scratch_shapes=[pltpu.VMEM((8, LANES), F32)],
        compiler_params=_cparams("arbitrary"),
        name="post_mix",
    )(*mixes, x, w_bf, g, b, wr_bf, br)


def _expert_kernel(vt_ref, ve_ref, lo_ref, hi_ref, first_ref, xs_ref, gate_ref, wg_ref, wu_ref, wd_ref, ys_ref):
    del vt_ref, ve_ref
    v = pl.program_id(0)
    lo = lo_ref[v]
    hi = hi_ref[v]

    @pl.when(hi > lo)
    def _():
        row = lax.broadcasted_iota(jnp.int32, (TE, 1), 0)
        gate = jnp.where((row >= lo) & (row < hi), gate_ref[...], 0.0)
        wg = wg_ref[...].astype(BF16)
        wu = wu_ref[...].astype(BF16)
        wd = wd_ref[...].astype(BF16)
        parts = []
        for s in range(TE // TE_SUB):
            rows = slice(s * TE_SUB, (s + 1) * TE_SUB)
            xs = xs_ref[rows, :].astype(BF16)
            g = jnp.dot(xs, wg, preferred_element_type=F32)
            u = jnp.dot(xs, wu, preferred_element_type=F32)
            h = g * (1.0 / (1.0 + jnp.exp(-g))) * u * gate[rows, :]
            parts.append(jnp.dot(h.astype(BF16), wd, preferred_element_type=F32))
        y = jnp.concatenate(parts, axis=0)

        @pl.when(first_ref[v] == 1)
        def _():
            ys_ref[...] = y

        @pl.when(first_ref[v] == 0)
        def _():
            ys_ref[...] += y


def _experts(meta, xs, gate_sorted, wg, wu, wd):
    tile_map = lambda v, vt, ve, lo, hi, fi: (vt[v], 0)
    w_map = lambda v, vt, ve, lo, hi, fi: (ve[v], 0, 0)
    grid_spec = pltpu.PrefetchScalarGridSpec(
        num_scalar_prefetch=5,
        grid=(N_VISITS,),
        in_specs=[pl.BlockSpec((TE, D_MODEL), tile_map),
                  pl.BlockSpec((TE, 1), tile_map),
                  pl.BlockSpec((None, D_MODEL, D_EXPERT), w_map),
                  pl.BlockSpec((None, D_MODEL, D_EXPERT), w_map),
                  pl.BlockSpec((None, D_EXPERT, D_MODEL), w_map)],
        out_specs=pl.BlockSpec((TE, D_MODEL), tile_map),
    )
    return pl.pallas_call(
        _expert_kernel,
        grid_spec=grid_spec,
        out_shape=jax.ShapeDtypeStruct((N_SLOTS, D_MODEL), F32),
        compiler_params=_cparams("arbitrary"),
        name="experts",
    )(*meta, xs, gate_sorted, wg, wu, wd)


def _dispatch(rinfo, totals):
    ids = rinfo[:, 0:2].astype(jnp.int32)
    ranks = rinfo[:, 4:6].astype(jnp.int32)
    counts = totals[0, 0:N_EXPERTS].astype(jnp.int32)
    end = jnp.cumsum(counts)
    off = end - counts
    pos_of_pair = off[ids] + ranks
    tok = lax.broadcasted_iota(jnp.int32, (N_TOK, 2), 0)
    _, sorted_tok, sorted_gate = lax.sort((ids.reshape(-1), tok.reshape(-1), rinfo[:, 2:4].reshape(-1)),
                                          num_keys=1, is_stable=True)
    first_tile = off // TE
    n_vis = jnp.where(counts > 0, (end - 1) // TE - first_tile + 1, 0)
    v_end = jnp.cumsum(n_vis)
    v_start = v_end - n_vis
    v = jnp.arange(N_VISITS, dtype=jnp.int32)
    valid = v < v_end[-1]
    e_v = jnp.minimum(jnp.sum((v[:, None] >= v_end[None, :]).astype(jnp.int32), axis=1), N_EXPERTS - 1)
    tile_v = jnp.where(valid, first_tile[e_v] + v - v_start[e_v], N_SLOTS // TE - 1)
    lo = jnp.where(valid, jnp.clip(off[e_v] - tile_v * TE, 0, TE), 0)
    hi = jnp.where(valid, jnp.clip(end[e_v] - tile_v * TE, 0, TE), 0)
    prev_tile = jnp.concatenate([jnp.full((1,), -1, jnp.int32), tile_v[:-1]])
    first = (valid & (tile_v != prev_tile)).astype(jnp.int32)
    meta = (tile_v.astype(jnp.int32), e_v.astype(jnp.int32), lo.astype(jnp.int32), hi.astype(jnp.int32), first)
    return sorted_tok, sorted_gate.reshape(N_SLOTS, 1), pos_of_pair, meta


def _ln2_kernel(x1_ref, ya_ref, yb_ref, g_ref, b_ref, o_ref):
    o_ref[...] = _layer_norm(ALPHA * x1_ref[...] + (ya_ref[...] + yb_ref[...]), g_ref[...], b_ref[...])


def _ln2_final_kernel(x1_ref, ya_ref, yb_ref, g_ref, b_ref, yp_ref, ys_ref):
    i = pl.program_id(0)
    y = _layer_norm(ALPHA * x1_ref[...] + (ya_ref[...] + yb_ref[...]), g_ref[...], b_ref[...])

    @pl.when(i < N_PTILES)
    def _():
        yp_ref[...] = y

    @pl.when(i == N_PTILES)
    def _():
        ys_ref[...] = y[0:DEC_BATCH, :]


def _ln2(x1, ya, yb, g, b, final):
    tile = pl.BlockSpec((TM, D_MODEL), lambda i: (i, 0))
    vec = pl.BlockSpec((1, D_MODEL), lambda i: (0, 0))
    if final:
        out_specs = [pl.BlockSpec((TM, D_MODEL), lambda i: (jnp.minimum(i, N_PTILES - 1), 0)),
                     pl.BlockSpec((DEC_BATCH, D_MODEL), lambda i: (0, 0))]
        out_shape = [jax.ShapeDtypeStruct((N_PROMPT, D_MODEL), F32), jax.ShapeDtypeStruct((DEC_BATCH, D_MODEL), F32)]
    else:
        out_specs, out_shape = tile, jax.ShapeDtypeStruct((N_TOK, D_MODEL), F32)
    return pl.pallas_call(
        _ln2_final_kernel if final else _ln2_kernel,
        grid=(N_TILES,),
        in_specs=[tile, tile, tile, vec, vec],
        out_specs=out_specs,
        out_shape=out_shape,
        compiler_params=_cparams("arbitrary"),
        name="ln2",
    )(x1, ya, yb, g, b)


def _moe(x1, rinfo, totals, wg, wu, wd, g2, b2, final):
    sorted_tok, sorted_gate, pos_of_pair, meta = _dispatch(rinfo, totals)
    ys = _experts(meta, x1[sorted_tok], sorted_gate, wg, wu, wd)
    return _ln2(x1, ys[pos_of_pair[:, 0]], ys[pos_of_pair[:, 1]], g2, b2, final)


LOG2E = math.log2(math.e)
QSCALE = SCALE * LOG2E
AUG_ONE = HEAD_DIM
AUG_CK = HEAD_DIM + 3
L2_ONE = 48
C_GROUP = C_HEADS // C_KV


def _aug_constants():
    perm_k = np.zeros((256 + LANES, C_KV * LANES), np.float32)
    for kv in range(C_KV):
        for d in range(HEAD_DIM):
            perm_k[kv * HEAD_DIM + d, kv * LANES + d] = 1.0
        for p in range(3):
            perm_k[256 + L2_ONE, kv * LANES + AUG_ONE + p] = 1.0
            for g in range(C_GROUP):
                perm_k[256 + 16 * p + kv * C_GROUP + g, kv * LANES + AUG_CK + 4 * p + g] = 1.0
    perm_q = np.zeros((C_HEADS * HEAD_DIM, HEAD_DIM), np.float32)
    for h in range(C_HEADS):
        for p in range(3):
            perm_q[h * HEAD_DIM + p, 16 * p + h] = 1.0
            perm_q[h * HEAD_DIM + 3 + 4 * p + h % C_GROUP, L2_ONE] = -1.0
    return jnp.asarray(perm_k, BF16), jnp.asarray(perm_q, BF16)


def _split3(v):
    hi = v.astype(BF16).astype(F32)
    mid = (v - hi).astype(BF16).astype(F32)
    lo = (v - hi - mid).astype(BF16).astype(F32)
    return hi, mid, lo


def _split_layout(v, lane):
    hi, mid, lo = _split3(jnp.where(lane < C_HEADS, v, 0.0))
    return hi + pltpu.roll(mid, 16, 1) + pltpu.roll(lo, 32, 1)


def _odd_in_kernel(x_ref, wt_ref, wkf_ref, bf_ref, permk_ref, permq_ref,
                   qa_ref, ka_ref, ktp_ref, vtp_ref, lftp_ref, qts_ref, kts_ref, vts_ref, lfts_ref,
                   carry_ref, r_ref):
    i = pl.program_id(0)
    xb = x_ref[...].astype(BF16)
    nt_dims = (((1,), (1,)), ((), ()))
    hr = jnp.dot(xb, wkf_ref[...], preferred_element_type=F32)
    lane = lax.broadcasted_iota(jnp.int32, (TM, LANES), 1)
    z = hr[:, 256:384] + bf_ref[...]
    lf = -(jnp.maximum(-z, 0.0) + jnp.log1p(jnp.exp(-jnp.abs(z))))
    lf = jnp.where(lane < C_HEADS, lf, 0.0)
    lft = lf.T[0:C_HEADS, :]

    @pl.when(i < N_PTILES)
    def _():
        @pl.when(i % TILES_PER_SEQ == 0)
        def _():
            carry_ref[...] = jnp.zeros_like(carry_ref)
        ht = lax.dot_general(wt_ref[...], xb, nt_dims, preferred_element_type=F32)
        row = lax.broadcasted_iota(jnp.int32, (TM, TM), 0)
        col = lax.broadcasted_iota(jnp.int32, (TM, TM), 1)
        tri = jnp.where(row >= col, 1.0, 0.0).astype(BF16)
        r_ref[...] = jnp.dot(tri, _split_layout(lf, lane).astype(BF16), preferred_element_type=F32)
        r = r_ref[...]
        c = r + pltpu.roll(r, LANES - 16, 1) + pltpu.roll(r, LANES - 32, 1)
        c = jnp.where(lane < C_HEADS, c, 0.0) + carry_ref[0:1, :]
        carry_ref[0:1, :] = c[TM - 1:TM, :]
        l2 = _split_layout(c * LOG2E, lane) + jnp.where(lane == L2_ONE, 1.0, 0.0)
        kin = jnp.concatenate([hr[:, 0:256], l2], axis=1).astype(BF16)
        ka_ref[...] = jnp.dot(kin, permk_ref[...], preferred_element_type=F32).astype(BF16)
        l2t = l2.T[0:HEAD_DIM, :].astype(BF16)
        qextra = jnp.dot(permq_ref[...], l2t, preferred_element_type=F32)
        for h in range(C_HEADS):
            rows = slice(h * HEAD_DIM, (h + 1) * HEAD_DIM)
            qa_ref[h, 0:HEAD_DIM, :] = (ht[rows, :] * QSCALE).astype(BF16)
            qa_ref[h, HEAD_DIM:2 * HEAD_DIM, :] = qextra[rows, :].astype(BF16)
        ktp_ref[...] = ht[1024:1280, :]
        vtp_ref[...] = ht[1280:1536, :]
        lftp_ref[...] = lft

    @pl.when(i == N_PTILES)
    def _():
        ht = lax.dot_general(wt_ref[...], xb[0:DEC_BATCH, :], nt_dims, preferred_element_type=F32)
        qts_ref[...] = ht[0:1024, :]
        kts_ref[...] = ht[1024:1280, :]
        vts_ref[...] = ht[1280:1536, :]
        lfts_ref[...] = lft[:, 0:DEC_BATCH]


def _odd_in(x, wt_bf, wkf_bf, bf, perm_k, perm_q):
    full = lambda shape: pl.BlockSpec(shape, lambda i: (0,) * len(shape))
    bidx = lambda i: jnp.minimum(i // TILES_PER_SEQ, BATCH - 1)
    tidx = lambda i: jnp.where(i < N_PTILES, i % TILES_PER_SEQ, TILES_PER_SEQ - 1)
    tpose = lambda rows: pl.BlockSpec((None, rows, TM), lambda i: (bidx(i), 0, tidx(i)))
    return pl.pallas_call(
        _odd_in_kernel,
        grid=(N_TILES,),
        in_specs=[pl.BlockSpec((TM, D_MODEL), lambda i: (i, 0)), full((1536, D_MODEL)), full((D_MODEL, 384)),
                  full((1, LANES)), full((256 + LANES, C_KV * LANES)), full((C_HEADS * HEAD_DIM, HEAD_DIM))],
        out_specs=[pl.BlockSpec((None, C_HEADS, 2 * HEAD_DIM, TM), lambda i: (bidx(i), 0, 0, tidx(i))),
                   pl.BlockSpec((TM, C_KV * LANES), lambda i: (jnp.minimum(i, N_PTILES - 1), 0)),
                   tpose(256), tpose(256), tpose(C_HEADS),
                   full((1024, DEC_BATCH)), full((256, DEC_BATCH)), full((256, DEC_BATCH)),
                   full((C_HEADS, DEC_BATCH))],
        out_shape=[jax.ShapeDtypeStruct((BATCH, C_HEADS, 2 * HEAD_DIM, SEQ), BF16),
                   jax.ShapeDtypeStruct((N_PROMPT, C_KV * LANES), BF16),
                   jax.ShapeDtypeStruct((BATCH, 256, SEQ), F32),
                   jax.ShapeDtypeStruct((BATCH, 256, SEQ), F32),
                   jax.ShapeDtypeStruct((BATCH, C_HEADS, SEQ), F32),
                   jax.ShapeDtypeStruct((1024, DEC_BATCH), F32),
                   jax.ShapeDtypeStruct((256, DEC_BATCH), F32),
                   jax.ShapeDtypeStruct((256, DEC_BATCH), F32),
                   jax.ShapeDtypeStruct((C_HEADS, DEC_BATCH), F32)],
        scratch_shapes=[pltpu.VMEM((8, LANES), F32), pltpu.VMEM((TM, LANES), F32)],
        compiler_params=_cparams("arbitrary"),
        name="odd_in",
    )(x, wt_bf, wkf_bf, bf, perm_k, perm_q)


FT = 256
ACC_ROWS = HEAD_DIM + 16


def _fox_prompt_kernel(qa_ref, ka_ref, vt_ref, o_ref, *scratch):
    acc_refs = scratch[0:C_GROUP]
    st_refs = scratch[C_GROUP:2 * C_GROUP]
    srow = lax.broadcasted_iota(jnp.int32, (FT, FT), 0)
    tcol = lax.broadcasted_iota(jnp.int32, (FT, FT), 1)

    def q_tile(qi, _):
        t0 = pl.multiple_of(qi * FT, FT)
        qts = [qa_ref[g, :, pl.ds(t0, FT)] for g in range(C_GROUP)]
        for acc_ref in acc_refs:
            acc_ref[...] = jnp.zeros_like(acc_ref)

        def chunk(s0, nk, ms, masked):
            ka = ka_ref[pl.ds(s0, nk), :]
            vt = jnp.concatenate([vt_ref[:, pl.ds(s0, nk)], jnp.ones((ACC_ROWS - HEAD_DIM, nk), F32)],
                                 axis=0).astype(BF16)
            bmax = []
            for g in range(C_GROUP):
                st = jnp.dot(ka, qts[g], preferred_element_type=F32)
                if masked:
                    st = jnp.where(srow <= tcol, st, NEG_INF)
                st_refs[g][0:nk, :] = st
                bmax.append(jnp.max(st, axis=0, keepdims=True))
            out = []
            for g in range(C_GROUP):
                m_new = jnp.maximum(ms[g], bmax[g])
                a = jnp.exp2(ms[g] - m_new)
                p = jnp.exp2(st_refs[g][0:nk, :] - m_new)
                acc_refs[g][...] = a * acc_refs[g][...] + jnp.dot(vt, p.astype(BF16), preferred_element_type=F32)
                out.append(m_new)
            return tuple(out)

        init = (jnp.full((1, FT), NEG_INF, F32),) * C_GROUP
        n4 = qi // 4
        rem = qi % 4
        ms = lax.fori_loop(0, n4, lambda j, c: chunk(pl.multiple_of(j * 4 * FT, 4 * FT), 4 * FT, c, False), init)
        s2 = pl.multiple_of(n4 * 4 * FT, FT)
        ms = lax.fori_loop(0, rem // 2, lambda j, c: chunk(s2, 2 * FT, c, False), ms)
        s1 = pl.multiple_of(s2 + (rem // 2) * 2 * FT, FT)
        ms = lax.fori_loop(0, rem % 2, lambda j, c: chunk(s1, FT, c, False), ms)
        chunk(t0, FT, ms, True)
        heads = []
        for g in range(C_GROUP):
            acc = acc_refs[g][...]
            heads.append(acc[0:HEAD_DIM, :] * (1.0 / acc[HEAD_DIM:HEAD_DIM + 1, :]))
        o_ref[pl.ds(t0, FT), :] = jnp.concatenate(heads, axis=0).T
        return 0

    lax.fori_loop(0, SEQ // FT, q_tile, 0)


def _fox_prompt(qa, ka, vtp):
    return pl.pallas_call(
        _fox_prompt_kernel,
        grid=(BATCH, C_KV),
        in_specs=[pl.BlockSpec((None, C_GROUP, 2 * HEAD_DIM, SEQ), lambda b, kv: (b, kv, 0, 0)),
                  pl.BlockSpec((SEQ, LANES), lambda b, kv: (b, kv)),
                  pl.BlockSpec((None, HEAD_DIM, SEQ), lambda b, kv: (b, kv, 0))],
        out_specs=pl.BlockSpec((SEQ, C_GROUP * HEAD_DIM), lambda b, kv: (b, kv)),
        out_shape=jax.ShapeDtypeStruct((N_PROMPT, C_HEADS * HEAD_DIM), F32),
        scratch_shapes=[pltpu.VMEM((ACC_ROWS, FT), F32)] * C_GROUP + [pltpu.VMEM((4 * FT, FT), F32)] * C_GROUP,
        compiler_params=_cparams("arbitrary", "arbitrary"),
        name="fox_prompt",
    )(qa, ka, vtp)


PG = 16
KVD = C_KV * HEAD_DIM


def _fox_sample_kernel(layer, pt_ref, qbd_ref, kts_ref, vts_ref, lfts_ref, kc_hbm, vc_hbm, lfc_hbm,
                       o_ref, kbuf, vbuf, lfbuf, s_ref, off_ref, sem):
    r = pl.program_id(0)
    slot = r % 2
    caches = ((kc_hbm, kbuf), (vc_hbm, vbuf), (lfc_hbm, lfbuf))

    def start_fetch(req, to_slot):
        def body(p, _):
            page = pt_ref[req * N_PAGES + p]
            for t, (src, dst) in enumerate(caches):
                pltpu.make_async_copy(src.at[layer, page], dst.at[to_slot, p], sem.at[t, to_slot]).start()
            return 0
        lax.fori_loop(0, N_PAGES, body, 0, unroll=4)

    @pl.when(r == 0)
    def _():
        start_fetch(0, 0)

    @pl.when(r + 1 < DEC_BATCH)
    def _():
        start_fetch(r + 1, 1 - slot)

    for t, (src, dst) in enumerate(caches):
        pltpu.make_async_copy(src.at[layer, pl.ds(0, N_PAGES)], dst.at[slot], sem.at[t, slot]).wait()

    qbd = qbd_ref[...].astype(BF16)
    rr = lax.broadcasted_iota(jnp.int32, (PAGE, PAGE), 0)
    cc = lax.broadcasted_iota(jnp.int32, (PAGE, PAGE), 1)
    upper = jnp.where(rr <= cc, 1.0, 0.0).astype(BF16)
    n = PG * C_HEADS
    rows = N_PAGES * C_HEADS

    tot = jnp.sum(lfbuf[slot].reshape(rows, PAGE), axis=-1, keepdims=True)
    incl = tot
    shift = C_HEADS
    while shift < rows:
        incl = incl + jnp.concatenate([jnp.zeros((shift, 1), F32), incl[0:rows - shift, :]], axis=0)
        shift *= 2
    off_ref[...] = jnp.broadcast_to(incl - tot, (rows, PAGE))
    total = incl[rows - C_HEADS:rows, :]

    def score_group(j, m):
        p0 = pl.multiple_of(j * PG, PG)
        lf_all = lfbuf[slot, pl.ds(p0, PG)].reshape(n, PAGE)
        parts = jnp.concatenate(_split3(lf_all), axis=0).astype(BF16)
        c3 = jnp.dot(parts, upper, preferred_element_type=F32)
        c_all = c3[0:n, :] + c3[n:2 * n, :] + c3[2 * n:3 * n, :] + off_ref[pl.ds(pl.multiple_of(j * n, n), n), :]
        s_parts = []
        for g in range(PG):
            kt = kbuf[slot, p0 + g].astype(BF16)
            s_parts.append(jnp.dot(qbd, kt, preferred_element_type=F32) - c_all[g * C_HEADS:(g + 1) * C_HEADS, :])
        s = jnp.concatenate(s_parts, axis=1)
        s_ref[:, pl.ds(pl.multiple_of(j * PG * PAGE, PG * PAGE), PG * PAGE)] = s
        return jnp.maximum(m, jnp.max(s, axis=-1, keepdims=True))

    m = lax.fori_loop(0, N_PAGES // PG, score_group, jnp.full((C_HEADS, 1), NEG_INF, F32))
    lane = lax.broadcasted_iota(jnp.int32, (C_HEADS, DEC_BATCH), 1)
    s_new = jnp.dot(qbd, kts_ref[...].astype(BF16), preferred_element_type=F32) - (total + lfts_ref[...])
    s_new = jnp.where(lane == r, s_new, NEG_INF)
    m = jnp.maximum(m, jnp.max(s_new, axis=-1, keepdims=True))

    def value_group(j, c):
        l, acc_t = c
        p0 = pl.multiple_of(j * PG, PG)
        p = jnp.exp(s_ref[:, pl.ds(pl.multiple_of(j * PG * PAGE, PG * PAGE), PG * PAGE)] - m)
        vt = jnp.concatenate([vbuf[slot, p0 + g] for g in range(PG)], axis=1).astype(BF16)
        acc_t = acc_t + jnp.dot(vt, p.T.astype(BF16), preferred_element_type=F32)
        return l + jnp.sum(p, axis=-1, keepdims=True), acc_t

    p_new = jnp.exp(s_new - m)
    init = (jnp.sum(p_new, axis=-1, keepdims=True),
            jnp.dot(vts_ref[...].astype(BF16), p_new.T.astype(BF16), preferred_element_type=F32))
    l, acc_t = lax.fori_loop(0, N_PAGES // PG, value_group, init)
    o_ref[...] = acc_t.T / l


def _fox_sample(layer, page_flat, qbd, kts, vts, lfts, kc, vc, lfc):
    const = lambda shape: pl.BlockSpec(shape, lambda r, pt: (0,) * len(shape))
    hbm = pl.BlockSpec(memory_space=pl.ANY)
    grid_spec = pltpu.PrefetchScalarGridSpec(
        num_scalar_prefetch=1,
        grid=(DEC_BATCH,),
        in_specs=[pl.BlockSpec((None, C_HEADS, KVD), lambda r, pt: (r, 0, 0)),
                  const((KVD, DEC_BATCH)), const((KVD, DEC_BATCH)), const((C_HEADS, DEC_BATCH)), hbm, hbm, hbm],
        out_specs=pl.BlockSpec((None, C_HEADS, KVD), lambda r, pt: (r, 0, 0)),
        scratch_shapes=[pltpu.VMEM((2, N_PAGES, KVD, PAGE), F32), pltpu.VMEM((2, N_PAGES, KVD, PAGE), F32),
                        pltpu.VMEM((2, N_PAGES, C_HEADS, PAGE), F32),
                        pltpu.VMEM((C_HEADS, (N_PAGES + 1) * PAGE), F32),
                        pltpu.VMEM((N_PAGES * C_HEADS, PAGE), F32),
                        pltpu.SemaphoreType.DMA((3, 2))],
    )
    return pl.pallas_call(
        functools.partial(_fox_sample_kernel, layer),
        grid_spec=grid_spec,
        out_shape=jax.ShapeDtypeStruct((DEC_BATCH, C_HEADS, KVD), F32),
        compiler_params=pltpu.CompilerParams(dimension_semantics=("arbitrary",), vmem_limit_bytes=VMEM_LIMIT,
                                             disable_bounds_checks=True),
        name="fox_sample",
    )(page_flat, qbd, kts, vts, lfts, kc, vc, lfc)


def _t5_bucket(dist):
    n = jnp.maximum(dist, 0)
    max_exact = N_BUCKETS // 2
    nf = jnp.maximum(n, 1).astype(F32)
    large = max_exact + (jnp.log(nf / max_exact) / math.log(MAX_DISTANCE / max_exact)
                         * (N_BUCKETS - max_exact)).astype(jnp.int32)
    large = jnp.minimum(large, N_BUCKETS - 1)
    return jnp.where(n < max_exact, n, large)


def kernel(x_prompt, x_sample, cache_swa_k, cache_swa_v, state_conv, cache_fox_k, cache_fox_v, cache_fox_logf, page_table, rel_bias_table, attn_sinks, w_in_even, conv_w, w_out_even, w_in_odd, b_forget, w_out_odd, ln_g, ln_b, w_group, b_group, w_router, b_router, w_gate, w_up, w_down):
    x = jnp.concatenate([x_prompt.reshape(N_PROMPT, D_MODEL), x_sample.reshape(DEC_BATCH, D_MODEL),
                         jnp.zeros((N_TOK - N_PROMPT - DEC_BATCH, D_MODEL), F32)], axis=0)

    qi = jnp.arange(WINDOW)[:, None]
    kj = jnp.arange(2 * WINDOW)[None, :]
    dist = WINDOW + qi - kj
    band = (dist >= 0) & (dist < WINDOW)
    onehot = (_t5_bucket(dist)[:, :, None] == jnp.arange(N_BUCKETS)[None, None, :]).astype(F32)
    bias_t = jnp.einsum('qkb,bh->hkq', onehot, rel_bias_table, precision=lax.Precision.HIGHEST)
    bias_p = jnp.stack([jnp.where((band & (kj >= WINDOW)).T[None], bias_t, NEG_INF),
                        jnp.where(band.T[None], bias_t, NEG_INF)])
    bias_s = rel_bias_table[_t5_bucket(WINDOW - jnp.arange(WINDOW))].T
    bnew_b = jnp.broadcast_to(rel_bias_table[0][:, None], (A_HEADS, LANES))

    swa_kt = jnp.transpose(cache_swa_k, (0, 1, 3, 4, 2))
    swa_vt = jnp.transpose(cache_swa_v, (0, 1, 3, 4, 2))
    pool = cache_fox_k.shape[1]
    fox_kt = jnp.transpose(cache_fox_k, (0, 1, 3, 4, 2)).reshape(DEPTH // 2, pool, KVD, PAGE)
    fox_vt = jnp.transpose(cache_fox_v, (0, 1, 3, 4, 2)).reshape(DEPTH // 2, pool, KVD, PAGE)
    fox_lft = jnp.transpose(cache_fox_logf, (0, 1, 3, 2))
    page_flat = page_table.reshape(-1)

    wr = jnp.concatenate([w_router, w_group], axis=-1)
    wr_hi = wr.astype(BF16)
    wr_lo = (wr - wr_hi.astype(F32)).astype(BF16)
    zpad = lambda n: jnp.zeros((DEPTH, D_MODEL, n), BF16)
    wr_all = jnp.concatenate([wr_hi, zpad(12), wr_lo, zpad(LANES - 52)], axis=-1)
    br_all = jnp.concatenate([b_router, b_group, jnp.zeros((DEPTH, LANES - 20), F32)], axis=-1)
    eye = jnp.eye(C_KV, dtype=F32)
    perm_k, perm_q = _aug_constants()

    outs = {n: [] for n in ("swa_kp", "swa_vp", "swa_ks", "swa_vs", "conv_p", "conv_s",
                            "fox_kp", "fox_vp", "fox_lp", "fox_ks", "fox_vs", "fox_ls")}
    for layer in range(DEPTH):
        if layer % 2 == 0:
            e = layer // 2
            s0 = state_conv[e, :, 0, :]
            s1 = state_conv[e, :, 1, :]
            q, k, v, cv, convp, us = _even_in(x, w_in_even[e].astype(BF16), conv_w[e], s0, s1)
            sink_b = jnp.broadcast_to(attn_sinks[e][:, None], (A_HEADS, LANES))
            att_p = _swa_prompt(q, k, v, bias_p, sink_b)
            att_s = _swa_sample(q, k, v, swa_kt[e], swa_vt[e], bias_s, bnew_b, sink_b)
            conv = cv
            w_out = w_out_even[e]
            last = lambda a: a[:N_PROMPT].reshape(BATCH, SEQ, 128)[:, SEQ - WINDOW:, :].reshape(
                BATCH, WINDOW, A_KV, HEAD_DIM)
            ks = k[N_PROMPT:N_PROMPT + DEC_BATCH].reshape(DEC_BATCH, 1, A_KV, HEAD_DIM)
            vs = v[N_PROMPT:N_PROMPT + DEC_BATCH].reshape(DEC_BATCH, 1, A_KV, HEAD_DIM)
            outs["swa_kp"].append(last(k))
            outs["swa_vp"].append(last(v))
            outs["swa_ks"].append(jnp.concatenate([cache_swa_k[e][:, 1:], ks], axis=1))
            outs["swa_vs"].append(jnp.concatenate([cache_swa_v[e][:, 1:], vs], axis=1))
            outs["conv_p"].append(convp)
            outs["conv_s"].append(jnp.stack([s1, us], axis=1))
        else:
            o = layer // 2
            wt_bf = jnp.transpose(w_in_odd[o][:, 0:1536]).astype(BF16)
            wkf_bf = jnp.concatenate([w_in_odd[o][:, 1024:1280], w_in_odd[o][:, 1536:1552],
                                      jnp.zeros((D_MODEL, 384 - 272), F32)], axis=-1).astype(BF16)
            bf_pad = jnp.concatenate([b_forget[o], jnp.zeros((LANES - C_HEADS,), F32)])[None, :]
            qa, ka, ktp, vtp, lftp, qts, kts, vts, lfts = _odd_in(x, wt_bf, wkf_bf, bf_pad, perm_k, perm_q)
            att_p = _fox_prompt(qa, ka, vtp)
            qs = qts.T.reshape(DEC_BATCH, C_KV, C_GROUP, 1, HEAD_DIM) * SCALE
            qbd = (qs * eye[None, :, None, :, None]).reshape(DEC_BATCH, C_HEADS, KVD)
            of = _fox_sample(o, page_flat, qbd, kts, vts, lfts, fox_kt, fox_vt, fox_lft)
            of = of.reshape(DEC_BATCH, C_KV, C_GROUP, C_KV, HEAD_DIM)
            att_s = jnp.sum(of * eye[None, :, None, :, None], axis=3).reshape(DEC_BATCH, C_HEADS * HEAD_DIM)
            att_s = jnp.concatenate([att_s, jnp.zeros((TM - DEC_BATCH, C_HEADS * HEAD_DIM), F32)], axis=0)
            conv = None
            w_out = w_out_odd[o]
            outs["fox_kp"].append(jnp.transpose(ktp.reshape(BATCH, C_KV, HEAD_DIM, SEQ), (0, 3, 1, 2)))
            outs["fox_vp"].append(jnp.transpose(vtp.reshape(BATCH, C_KV, HEAD_DIM, SEQ), (0, 3, 1, 2)))
            outs["fox_lp"].append(jnp.transpose(lftp, (0, 2, 1)))
            outs["fox_ks"].append(jnp.transpose(kts.reshape(C_KV, HEAD_DIM, DEC_BATCH), (2, 0, 1))[:, None])
            outs["fox_vs"].append(jnp.transpose(vts.reshape(C_KV, HEAD_DIM, DEC_BATCH), (2, 0, 1))[:, None])
            outs["fox_ls"].append(lfts.T[:, None, :])
        x1, rinfo, totals = _post_mix(att_p, att_s, conv, x, w_out.astype(BF16), ln_g[layer, 0][None, :], ln_b[layer, 0][None, :],
                                      wr_all[layer], br_all[layer][None, :])
        x = _moe(x1, rinfo, totals, w_gate[layer], w_up[layer], w_down[layer],
                 ln_g[layer, 1][None, :], ln_b[layer, 1][None, :], layer == DEPTH - 1)

    st = {n: jnp.stack(vl) for n, vl in outs.items()}
    y_prompt = x[0].reshape(BATCH, SEQ, D_MODEL)
    y_sample = x[1].reshape(DEC_BATCH, 1, D_MODEL)
    return (y_prompt, y_sample, st["swa_kp"], st["swa_vp"], st["swa_ks"], st["swa_vs"], st["conv_p"], st["conv_s"],
            st["fox_kp"], st["fox_vp"], st["fox_lp"], st["fox_ks"], st["fox_vs"], st["fox_ls"])
```

```python
import functools
import math

import jax
import jax.numpy as jnp
import numpy as np
from jax import lax
from jax.experimental import pallas as pl
from jax.experimental.pallas import tpu as pltpu

F32 = jnp.float32
BF16 = jnp.bfloat16

D_MODEL = 1024
BATCH = 4
SEQ = 4096
DEC_BATCH = 128
PAGE = 128
N_PAGES = 64
HEAD_DIM = 64
A_HEADS = 8
A_KV = 2
WINDOW = 128
B_WIDTH = 512
C_HEADS = 16
C_KV = 4
N_BUCKETS = 32
MAX_DISTANCE = 128
N_GROUPS = 4
N_EXPERTS = 16
D_EXPERT = 256
DEPTH = 4
ALPHA = (2.0 * DEPTH) ** 0.25
LN_EPS = 1e-5
NEG_INF = -1e30
SCALE = HEAD_DIM ** -0.5

N_PROMPT = BATCH * SEQ
TM = 512
N_PTILES = N_PROMPT // TM
TILES_PER_SEQ = SEQ // TM
N_TILES = N_PTILES + 1
N_TOK = N_TILES * TM
LANES = 128

TE = 512
TE_SUB = 256
N_SLOTS = 2 * N_TOK
N_VISITS = N_SLOTS // TE + N_EXPERTS - 1

VMEM_LIMIT = 56 * 1024 * 1024


def _cparams(*sem):
    return pltpu.CompilerParams(dimension_semantics=sem, vmem_limit_bytes=VMEM_LIMIT)


def _layer_norm(y, g, b):
    mu = jnp.mean(y, axis=-1, keepdims=True)
    yc = y - mu
    var = jnp.mean(yc * yc, axis=-1, keepdims=True)
    return yc * lax.rsqrt(var + LN_EPS) * g + b


def _even_in_kernel(x_ref, w_ref, cw_ref, s0_ref, s1_ref,
                    q_ref, k_ref, v_ref, cv_ref, convp_ref, us_ref, klt_ref, vlt_ref, carry_ref):
    i = pl.program_id(0)
    h = jnp.dot(x_ref[...].astype(BF16), w_ref[...], preferred_element_type=F32)
    q_ref[...] = h[:, 0:512]
    k_ref[...] = h[:, 512:640]
    v_ref[...] = h[:, 640:768]
    bg = h[:, 768:1280]
    u = h[:, 1280:1792] * h[:, 1792:2304]
    w0 = cw_ref[0:1, :]
    w1 = cw_ref[1:2, :]
    w2 = cw_ref[2:3, :]

    @pl.when(i < N_PTILES)
    def _():
        @pl.when(i % TILES_PER_SEQ == 0)
        def _():
            carry_ref[...] = jnp.zeros_like(carry_ref)
        row = lax.broadcasted_iota(jnp.int32, (TM, B_WIDTH), 0)
        c2 = carry_ref[0:1, :]
        c1 = carry_ref[1:2, :]
        u1 = jnp.where(row == 0, c1, pltpu.roll(u, 1, 0))
        u2 = jnp.where(row == 0, c2, jnp.where(row == 1, c1, pltpu.roll(u, 2, 0)))
        cv_ref[...] = bg * (w0 * u2 + w1 * u1 + w2 * u)
        carry_ref[0:2, :] = u[TM - 2:TM, :]
        convp_ref[...] = u[TM - 2:TM, :]

        @pl.when(i % TILES_PER_SEQ == TILES_PER_SEQ - 1)
        def _():
            klt_ref[...] = h[TM - WINDOW:TM, 512:640].T
            vlt_ref[...] = h[TM - WINDOW:TM, 640:768].T

    @pl.when(i == N_PTILES)
    def _():
        us = u[0:DEC_BATCH, :]
        z = w0 * s0_ref[...] + w1 * s1_ref[...] + w2 * us
        cv_ref[0:DEC_BATCH, :] = bg[0:DEC_BATCH, :] * z
        cv_ref[DEC_BATCH:TM, :] = jnp.zeros((TM - DEC_BATCH, B_WIDTH), F32)
        us_ref[...] = us


def _even_in(x, w_bf, cw, s0, s1):
    tile = lambda n: pl.BlockSpec((TM, n), lambda i: (i, 0))
    full = lambda shape: pl.BlockSpec(shape, lambda i: (0,) * len(shape))
    return pl.pallas_call(
        _even_in_kernel,
        grid=(N_TILES,),
        in_specs=[tile(D_MODEL), full((D_MODEL, 2304)), full((3, B_WIDTH)),
                  full((DEC_BATCH, B_WIDTH)), full((DEC_BATCH, B_WIDTH))],
        out_specs=[tile(512), tile(128), tile(128), tile(B_WIDTH),
                   pl.BlockSpec((None, 2, B_WIDTH),
                                lambda i: (jnp.minimum(i // TILES_PER_SEQ, BATCH - 1), 0, 0)),
                   full((DEC_BATCH, B_WIDTH)),
                   pl.BlockSpec((None, 128, WINDOW), lambda i: (jnp.minimum(i // TILES_PER_SEQ, BATCH - 1), 0, 0)),
                   pl.BlockSpec((None, 128, WINDOW), lambda i: (jnp.minimum(i // TILES_PER_SEQ, BATCH - 1), 0, 0))],
        out_shape=[jax.ShapeDtypeStruct((N_TOK, 512), F32),
                   jax.ShapeDtypeStruct((N_TOK, 128), F32),
                   jax.ShapeDtypeStruct((N_TOK, 128), F32),
                   jax.ShapeDtypeStruct((N_TOK, B_WIDTH), F32),
                   jax.ShapeDtypeStruct((BATCH, 2, B_WIDTH), F32),
                   jax.ShapeDtypeStruct((DEC_BATCH, B_WIDTH), F32),
                   jax.ShapeDtypeStruct((BATCH, 128, WINDOW), F32),
                   jax.ShapeDtypeStruct((BATCH, 128, WINDOW), F32)],
        scratch_shapes=[pltpu.VMEM((8, B_WIDTH), F32)],
        compiler_params=_cparams("arbitrary"),
        name="even_in",
    )(x, w_bf, cw, s0, s1)


def _swa_prompt_kernel(q_ref, kc_ref, kp_ref, vc_ref, vp_ref, bias_ref, sink_ref, o_ref, *st_refs):
    qt = (q_ref[...] * SCALE).T
    kk = jnp.concatenate([kp_ref[...], kc_ref[...]], axis=0).astype(BF16)
    vvt = jnp.concatenate([vp_ref[...], vc_ref[...]], axis=0).T
    zeros = jnp.zeros((HEAD_DIM, WINDOW), F32)
    ones = jnp.ones((SWA_ACC_ROWS - HEAD_DIM, 2 * WINDOW), F32)
    cmax = []
    for h in range(A_HEADS):
        kv = h // (A_HEADS // A_KV)
        qh = qt[h * HEAD_DIM:(h + 1) * HEAD_DIM, :]
        qh = jnp.concatenate([qh, zeros] if kv == 0 else [zeros, qh], axis=0).astype(BF16)
        st = jnp.dot(kk, qh, preferred_element_type=F32) + bias_ref[h]
        st_refs[h][...] = st
        cmax.append(jnp.max(st, axis=0, keepdims=True))
    outs = []
    for h in range(A_HEADS):
        kv = h // (A_HEADS // A_KV)
        sk = sink_ref[h:h + 1, :]
        m = jnp.maximum(cmax[h], sk)
        p = jnp.exp(st_refs[h][...] - m)
        vt = jnp.concatenate([vvt[kv * HEAD_DIM:(kv + 1) * HEAD_DIM, :], ones], axis=0).astype(BF16)
        acc = jnp.dot(vt, p.astype(BF16), preferred_element_type=F32)
        den = acc[HEAD_DIM:HEAD_DIM + 1, :] + jnp.exp(sk - m)
        outs.append(acc[0:HEAD_DIM, :] / den)
    o_ref[...] = jnp.concatenate(outs, axis=0).T


def _swa_prompt(q, k, v, bias, sink_b):
    nblk = SEQ // WINDOW
    cur = lambda b, j: (b * nblk + j, 0)
    prev = lambda b, j: (jnp.maximum(b * nblk + j - 1, 0), 0)
    return pl.pallas_call(
        _swa_prompt_kernel,
        grid=(BATCH, nblk),
        in_specs=[pl.BlockSpec((WINDOW, 512), cur),
                  pl.BlockSpec((WINDOW, 128), cur), pl.BlockSpec((WINDOW, 128), prev),
                  pl.BlockSpec((WINDOW, 128), cur), pl.BlockSpec((WINDOW, 128), prev),
                  pl.BlockSpec((None, A_HEADS, 2 * WINDOW, WINDOW), lambda b, j: (jnp.minimum(j, 1), 0, 0, 0)),
                  pl.BlockSpec((A_HEADS, LANES), lambda b, j: (0, 0))],
        out_specs=pl.BlockSpec((WINDOW, 512), cur),
        out_shape=jax.ShapeDtypeStruct((N_PROMPT, 512), F32),
        scratch_shapes=[pltpu.VMEM((2 * WINDOW, WINDOW), F32)] * A_HEADS,
        compiler_params=_cparams("arbitrary", "arbitrary"),
        name="swa_prompt",
    )(q, k, k, v, v, bias, sink_b)


SWA_RB = 8
SWA_ACC_ROWS = HEAD_DIM + 16


def _swa_sample_kernel(q_ref, k_ref, v_ref, kt_ref, vt_ref, bias_ref, bnew_ref, sink_ref, o_ref):

    @pl.when(pl.program_id(0) >= DEC_BATCH // SWA_RB)
    def _():
        o_ref[...] = jnp.zeros_like(o_ref)

    @pl.when(pl.program_id(0) < DEC_BATCH // SWA_RB)
    def _():
        _swa_sample_rows(q_ref, k_ref, v_ref, kt_ref, vt_ref, bias_ref, bnew_ref, sink_ref, o_ref)


def _swa_sample_rows(q_ref, k_ref, v_ref, kt_ref, vt_ref, bias_ref, bnew_ref, sink_ref, o_ref):
    lane = lax.broadcasted_iota(jnp.int32, (4, WINDOW), 1)
    grp = A_HEADS // A_KV
    for r in range(SWA_RB):
        for kv in range(A_KV):
            hs = slice(kv * grp, (kv + 1) * grp)
            q4 = jnp.concatenate(
                [q_ref[r:r + 1, (kv * grp + g) * HEAD_DIM:(kv * grp + g + 1) * HEAD_DIM] for g in range(grp)],
                axis=0)
            kt = kt_ref[r, kv]
            vt = vt_ref[r, kv]
            s_old = jnp.dot(q4.astype(BF16), kt.astype(BF16), preferred_element_type=F32)
            s_old = s_old * SCALE + bias_ref[hs, :]
            s_old = jnp.where(lane >= 1, s_old, NEG_INF)
            kn = k_ref[r:r + 1, kv * HEAD_DIM:(kv + 1) * HEAD_DIM]
            vn = v_ref[r:r + 1, kv * HEAD_DIM:(kv + 1) * HEAD_DIM]
            s_new = jnp.sum(q4 * kn, axis=-1, keepdims=True) * SCALE + bnew_ref[hs, 0:1]
            sk = sink_ref[hs, 0:1]
            m = jnp.maximum(jnp.maximum(jnp.max(s_old, axis=-1, keepdims=True), s_new), sk)
            p_old = jnp.exp(s_old - m)
            p_new = jnp.exp(s_new - m)
            den = jnp.sum(p_old, axis=-1, keepdims=True) + p_new + jnp.exp(sk - m)
            o = lax.dot_general(p_old.astype(BF16), vt.astype(BF16), (((1,), (1,)), ((), ())),
                                preferred_element_type=F32)
            o = (o + p_new * vn) / den
            for g in range(grp):
                hh = kv * grp + g
                o_ref[r:r + 1, hh * HEAD_DIM:(hh + 1) * HEAD_DIM] = o[g:g + 1, :]


def _swa_sample(q, k, v, kt, vt, bias_s, bnew_b, sink_b):
    base = N_PROMPT // SWA_RB
    rows = lambda n: pl.BlockSpec((SWA_RB, n), lambda i: (base + i, 0))
    cache = pl.BlockSpec((SWA_RB, A_KV, HEAD_DIM, WINDOW),
                         lambda i: (jnp.minimum(i, DEC_BATCH // SWA_RB - 1), 0, 0, 0))
    small = pl.BlockSpec((A_HEADS, LANES), lambda i: (0, 0))
    return pl.pallas_call(
        _swa_sample_kernel,
        grid=(TM // SWA_RB,),
        in_specs=[rows(512), rows(128), rows(128), cache, cache, small, small, small],
        out_specs=pl.BlockSpec((SWA_RB, 512), lambda i: (i, 0)),
        out_shape=jax.ShapeDtypeStruct((TM, 512), F32),
        compiler_params=_cparams("arbitrary"),
        name="swa_sample",
    )(q, k, v, kt, vt, bias_s, bnew_b, sink_b)


def _route(x1, wr_ref, br_ref, below_ref, cnt_ref):
    x_hi = x1.astype(BF16)
    x_lo = (x1 - x_hi.astype(F32)).astype(BF16)
    pa = jnp.dot(x_hi, wr_ref[...], preferred_element_type=F32)
    pb = jnp.dot(x_lo, wr_ref[...], preferred_element_type=F32)
    logits = pa + pltpu.roll(pa, LANES - 32, 1) + pb + br_ref[...]
    lane = lax.broadcasted_iota(jnp.int32, logits.shape, 1)
    lane_f = lane.astype(F32)
    lane_grp = (lane >> 2).astype(F32)
    is_grp = (lane >= N_EXPERTS) & (lane < N_EXPERTS + N_GROUPS)
    big = 1e9
    gl = jnp.where(is_grp, logits, NEG_INF)
    gmax = jnp.max(gl, axis=-1, keepdims=True)
    gidx = jnp.min(jnp.where(is_grp & (logits == gmax), lane_f - N_EXPERTS, big), axis=-1, keepdims=True)
    gsum = jnp.sum(jnp.where(is_grp, jnp.exp(gl - gmax), 0.0), axis=-1, keepdims=True)
    grp_w = 1.0 / gsum
    in_grp = (lane < N_EXPERTS) & (lane_grp == gidx)
    e1 = jnp.where(in_grp, logits, NEG_INF)
    t1 = jnp.max(e1, axis=-1, keepdims=True)
    i1 = jnp.min(jnp.where(in_grp & (logits == t1), lane_f, big), axis=-1, keepdims=True)
    rest = in_grp & (lane_f != i1)
    e2 = jnp.where(rest, logits, NEG_INF)
    t2 = jnp.max(e2, axis=-1, keepdims=True)
    i2 = jnp.min(jnp.where(rest & (logits == t2), lane_f, big), axis=-1, keepdims=True)
    ex = jnp.exp(t2 - t1)
    g1 = grp_w / (1.0 + ex)
    g2 = grp_w * ex / (1.0 + ex)
    sel1 = lane_f == i1
    sel2 = lane_f == i2
    onehot = jnp.where(sel1 | sel2, 1.0, 0.0)
    before = jnp.dot(below_ref[...], onehot.astype(BF16), preferred_element_type=F32) + cnt_ref[0:1, :]
    rank1 = jnp.sum(jnp.where(sel1, before, 0.0), axis=-1, keepdims=True)
    rank2 = jnp.sum(jnp.where(sel2, before, 0.0), axis=-1, keepdims=True)
    cnt_ref[0:1, :] = cnt_ref[0:1, :] + jnp.sum(onehot, axis=0, keepdims=True)
    out = jnp.where(lane == 0, i1, 0.0)
    out = jnp.where(lane == 1, i2, out)
    out = jnp.where(lane == 2, g1, out)
    out = jnp.where(lane == 3, g2, out)
    out = jnp.where(lane == 4, rank1, out)
    out = jnp.where(lane == 5, rank2, out)
    return out


def _post_mix_kernel(has_conv, *refs):
    n_in = 3 if has_conv else 2
    att_p_ref, att_s_ref = refs[0:2]
    x_ref, w_ref, g_ref, b_ref, wr_ref, br_ref, below_ref, x1_ref, r_ref, tot_ref, cnt_ref = refs[n_in:]
    i = pl.program_id(0)

    @pl.when(i == 0)
    def _():
        cnt_ref[...] = jnp.zeros_like(cnt_ref)
    att = jnp.where(i < N_PTILES, att_p_ref[...], att_s_ref[...]).astype(BF16)
    kw = att.shape[1]
    mix = jnp.dot(att, w_ref[0:kw, :], preferred_element_type=F32)
    if has_conv:
        mix = mix + jnp.dot(refs[2][...].astype(BF16), w_ref[kw:D_MODEL, :], preferred_element_type=F32)
    x1 = _layer_norm(ALPHA * x_ref[...] + mix, g_ref[...], b_ref[...])
    x1_ref[...] = x1
    r_ref[...] = _route(x1, wr_ref, br_ref, below_ref, cnt_ref)
    tot_ref[...] = cnt_ref[...]


def _post_mix(att_p, att_s, conv, x, w_bf, g, b, wr_bf, br):
    kw = att_p.shape[1]
    tile = lambda n: pl.BlockSpec((TM, n), lambda i: (i, 0))
    full = lambda shape: pl.BlockSpec(shape, lambda i: (0,) * len(shape))
    mix_specs = [pl.BlockSpec((TM, kw), lambda i: (jnp.minimum(i, N_PTILES - 1), 0)), full((TM, kw))]
    mixes = [att_p, att_s]
    if conv is not None:
        mix_specs.append(tile(D_MODEL - kw))
        mixes.append(conv)
    return pl.pallas_call(
        functools.partial(_post_mix_kernel, conv is not None),
        grid=(N_TILES,),
        in_specs=mix_specs + [tile(D_MODEL), full((D_MODEL, D_MODEL)), full((1, D_MODEL)),
                              full((1, D_MODEL)), full((D_MODEL, LANES)), full((1, LANES)), full((TM, TM))],
        out_specs=[tile(D_MODEL), tile(LANES), full((8, LANES))],
        out_shape=[jax.ShapeDtypeStruct((N_TOK, D_MODEL), F32),
                   jax.ShapeDtypeStruct((N_TOK, LANES), F32),
                   jax.ShapeDtypeStruct((8, LANES), F32)],
        scratch_shapes=[pltpu.VMEM((8, LANES), F32)],
        compiler_params=_cparams("arbitrary"),
        name="post_mix",
    )(*mixes, x, w_bf, g, b, wr_bf, br, jnp.asarray(np.tril(np.ones((TM, TM), np.float32), -1), BF16))


def _expert_kernel(vt_ref, ve_ref, lo_ref, hi_ref, first_ref, newexp_ref, xs_ref, wg_ref, wu_ref, wd_ref, ys_ref,
                   wgb_ref, wub_ref, wdb_ref):
    del vt_ref, ve_ref
    v = pl.program_id(0)
    lo = lo_ref[v]
    hi = hi_ref[v]

    @pl.when((hi > lo) & (newexp_ref[v] == 1))
    def _():
        wgb_ref[...] = wg_ref[...].astype(BF16)
        wub_ref[...] = wu_ref[...].astype(BF16)
        wdb_ref[...] = wd_ref[...].astype(BF16)

    @pl.when(hi > lo)
    def _():
        row = lax.broadcasted_iota(jnp.int32, (TE, 1), 0)
        mine = jnp.where((row >= lo) & (row < hi), 1.0, 0.0)
        wg = wgb_ref[...]
        wu = wub_ref[...]
        wd = wdb_ref[...]
        parts = []
        for s in range(TE // TE_SUB):
            rows = slice(s * TE_SUB, (s + 1) * TE_SUB)
            xs = xs_ref[rows, :].astype(BF16)
            g = jnp.dot(xs, wg, preferred_element_type=F32)
            u = jnp.dot(xs, wu, preferred_element_type=F32)
            h = g * (1.0 / (1.0 + jnp.exp(-g))) * u * mine[rows, :]
            parts.append(jnp.dot(h.astype(BF16), wd, preferred_element_type=F32))
        y = jnp.concatenate(parts, axis=0)

        @pl.when(first_ref[v] == 1)
        def _():
            ys_ref[...] = y

        @pl.when(first_ref[v] == 0)
        def _():
            ys_ref[...] += y


def _experts(meta, xs, layer, wg, wu, wd):
    tile_map = lambda v, vt, ve, lo, hi, fi, ne: (vt[v], 0)
    w_map = lambda v, vt, ve, lo, hi, fi, ne: (layer, ve[v], 0, 0)
    grid_spec = pltpu.PrefetchScalarGridSpec(
        num_scalar_prefetch=6,
        grid=(N_VISITS,),
        in_specs=[pl.BlockSpec((TE, D_MODEL), tile_map),
                  pl.BlockSpec((None, None, D_MODEL, D_EXPERT), w_map),
                  pl.BlockSpec((None, None, D_MODEL, D_EXPERT), w_map),
                  pl.BlockSpec((None, None, D_EXPERT, D_MODEL), w_map)],
        out_specs=pl.BlockSpec((TE, D_MODEL), tile_map),
        scratch_shapes=[pltpu.VMEM((D_MODEL, D_EXPERT), BF16), pltpu.VMEM((D_MODEL, D_EXPERT), BF16),
                        pltpu.VMEM((D_EXPERT, D_MODEL), BF16)],
    )
    return pl.pallas_call(
        _expert_kernel,
        grid_spec=grid_spec,
        out_shape=jax.ShapeDtypeStruct((N_SLOTS, D_MODEL), F32),
        compiler_params=_cparams("arbitrary"),
        name="experts",
    )(*meta, xs, wg, wu, wd)


def _dispatch(rinfo, totals):
    ids = rinfo[:, 0:2].astype(jnp.int32)
    ranks = rinfo[:, 4:6].astype(jnp.int32)
    counts = totals[0, 0:N_EXPERTS].astype(jnp.int32)
    e_iota = jnp.arange(N_EXPERTS, dtype=jnp.int32)
    incl = (e_iota[None, :] <= e_iota[:, None]).astype(jnp.int32)
    end = jnp.sum(incl * counts[None, :], axis=1)
    off = end - counts
    pos_of_pair = jnp.sum(jnp.where(e_iota[None, None, :] < ids[:, :, None], counts[None, None, :], 0), axis=2) + ranks
    tok = lax.broadcasted_iota(jnp.int32, (N_TOK, 2), 0)
    _, sorted_tok = lax.sort((ids.reshape(-1), tok.reshape(-1)), num_keys=1, is_stable=True)
    first_tile = off // TE
    n_vis = jnp.where(counts > 0, (end - 1) // TE - first_tile + 1, 0)
    v_end = jnp.sum(incl * n_vis[None, :], axis=1)
    v = jnp.arange(N_VISITS, dtype=jnp.int32)
    valid = v < v_end[N_EXPERTS - 1]
    e_v = jnp.minimum(jnp.sum((v[:, None] >= v_end[None, :]).astype(jnp.int32), axis=1), N_EXPERTS - 1)
    pick = (e_v[:, None] == e_iota[None, :]).astype(jnp.int32)
    sel = lambda a: jnp.sum(pick * a[None, :], axis=1)
    tile_v = jnp.where(valid, sel(first_tile) + v - sel(v_end - n_vis), N_SLOTS // TE - 1)
    lo = jnp.where(valid, jnp.clip(sel(off) - tile_v * TE, 0, TE), 0)
    hi = jnp.where(valid, jnp.clip(sel(end) - tile_v * TE, 0, TE), 0)
    prev_tile = jnp.concatenate([jnp.full((1,), -1, jnp.int32), tile_v[:-1]])
    prev_e = jnp.concatenate([jnp.full((1,), -1, jnp.int32), e_v[:-1]])
    first = (valid & (tile_v != prev_tile)).astype(jnp.int32)
    newexp = (valid & (e_v != prev_e)).astype(jnp.int32)
    meta = (tile_v.astype(jnp.int32), e_v.astype(jnp.int32), lo.astype(jnp.int32), hi.astype(jnp.int32), first, newexp)
    return sorted_tok, pos_of_pair, meta


def _combine_ln(x1_ref, ya_ref, yb_ref, r_ref, g_ref, b_ref):
    r = r_ref[...]
    moe = r[:, 2:3] * ya_ref[...] + r[:, 3:4] * yb_ref[...]
    return _layer_norm(ALPHA * x1_ref[...] + moe, g_ref[...], b_ref[...])


def _ln2_kernel(x1_ref, ya_ref, yb_ref, r_ref, g_ref, b_ref, o_ref):
    o_ref[...] = _combine_ln(x1_ref, ya_ref, yb_ref, r_ref, g_ref, b_ref)


def _ln2_final_kernel(x1_ref, ya_ref, yb_ref, r_ref, g_ref, b_ref, yp_ref, ys_ref):
    i = pl.program_id(0)
    y = _combine_ln(x1_ref, ya_ref, yb_ref, r_ref, g_ref, b_ref)

    @pl.when(i < N_PTILES)
    def _():
        yp_ref[...] = y

    @pl.when(i == N_PTILES)
    def _():
        ys_ref[...] = y[0:DEC_BATCH, :]


def _ln2(x1, ya, yb, rinfo, g, b, final):
    tile = pl.BlockSpec((TM, D_MODEL), lambda i: (i, 0))
    rtile = pl.BlockSpec((TM, LANES), lambda i: (i, 0))
    vec = pl.BlockSpec((1, D_MODEL), lambda i: (0, 0))
    if final:
        out_specs = [pl.BlockSpec((TM, D_MODEL), lambda i: (jnp.minimum(i, N_PTILES - 1), 0)),
                     pl.BlockSpec((DEC_BATCH, D_MODEL), lambda i: (0, 0))]
        out_shape = [jax.ShapeDtypeStruct((N_PROMPT, D_MODEL), F32), jax.ShapeDtypeStruct((DEC_BATCH, D_MODEL), F32)]
    else:
        out_specs, out_shape = tile, jax.ShapeDtypeStruct((N_TOK, D_MODEL), F32)
    return pl.pallas_call(
        _ln2_final_kernel if final else _ln2_kernel,
        grid=(N_TILES,),
        in_specs=[tile, tile, tile, rtile, vec, vec],
        out_specs=out_specs,
        out_shape=out_shape,
        compiler_params=_cparams("arbitrary"),
        name="ln2",
    )(x1, ya, yb, rinfo, g, b)


def _moe(x1, rinfo, totals, layer, wg, wu, wd, g2, b2, final):
    sorted_tok, pos_of_pair, meta = _dispatch(rinfo, totals)
    ys = _experts(meta, x1[sorted_tok], layer, wg, wu, wd)
    return _ln2(x1, ys[pos_of_pair[:, 0]], ys[pos_of_pair[:, 1]], rinfo, g2, b2, final)


LOG2E = math.log2(math.e)
QSCALE = SCALE * LOG2E
AUG_ONE = HEAD_DIM
AUG_CK = HEAD_DIM + 3
L2_ONE = 48
C_GROUP = C_HEADS // C_KV


def _aug_constants():
    perm_k = np.zeros((256 + LANES, C_KV * LANES), np.float32)
    for kv in range(C_KV):
        for d in range(HEAD_DIM):
            perm_k[kv * HEAD_DIM + d, kv * LANES + d] = 1.0
        for p in range(3):
            perm_k[256 + L2_ONE, kv * LANES + AUG_ONE + p] = 1.0
            for g in range(C_GROUP):
                perm_k[256 + 16 * p + kv * C_GROUP + g, kv * LANES + AUG_CK + 4 * p + g] = 1.0
    perm_q = np.zeros((C_HEADS * HEAD_DIM, HEAD_DIM), np.float32)
    for h in range(C_HEADS):
        for p in range(3):
            perm_q[h * HEAD_DIM + p, 16 * p + h] = 1.0
            perm_q[h * HEAD_DIM + 3 + 4 * p + h % C_GROUP, L2_ONE] = -1.0
    return jnp.asarray(perm_k, BF16), jnp.asarray(perm_q, BF16)


def _split3(v):
    hi = v.astype(BF16).astype(F32)
    mid = (v - hi).astype(BF16).astype(F32)
    lo = (v - hi - mid).astype(BF16).astype(F32)
    return hi, mid, lo


def _split_layout(v, lane):
    hi, mid, lo = _split3(jnp.where(lane < C_HEADS, v, 0.0))
    return hi + pltpu.roll(mid, 16, 1) + pltpu.roll(lo, 32, 1)


def _odd_in_kernel(x_ref, wt_ref, wkf_ref, bf_ref, permk_ref, permq_ref,
                   qa_ref, ka_ref, ktp_ref, vtp_ref, lftp_ref, qts_ref, kts_ref, vts_ref, lfts_ref,
                   carry_ref, r_ref):
    i = pl.program_id(0)
    xb = x_ref[...].astype(BF16)
    nt_dims = (((1,), (1,)), ((), ()))
    hr = jnp.dot(xb, wkf_ref[...], preferred_element_type=F32)
    lane = lax.broadcasted_iota(jnp.int32, (TM, LANES), 1)
    z = hr[:, 256:384] + bf_ref[...]
    lf = -(jnp.maximum(-z, 0.0) + jnp.log1p(jnp.exp(-jnp.abs(z))))
    lf = jnp.where(lane < C_HEADS, lf, 0.0)
    lft = lf.T[0:C_HEADS, :]

    @pl.when(i < N_PTILES)
    def _():
        @pl.when(i % TILES_PER_SEQ == 0)
        def _():
            carry_ref[...] = jnp.zeros_like(carry_ref)
        ht = lax.dot_general(wt_ref[...], xb, nt_dims, preferred_element_type=F32)
        row = lax.broadcasted_iota(jnp.int32, (TM, TM), 0)
        col = lax.broadcasted_iota(jnp.int32, (TM, TM), 1)
        tri = jnp.where(row >= col, 1.0, 0.0).astype(BF16)
        r_ref[...] = jnp.dot(tri, _split_layout(lf, lane).astype(BF16), preferred_element_type=F32)
        r = r_ref[...]
        c = r + pltpu.roll(r, LANES - 16, 1) + pltpu.roll(r, LANES - 32, 1)
        c = jnp.where(lane < C_HEADS, c, 0.0) + carry_ref[0:1, :]
        carry_ref[0:1, :] = c[TM - 1:TM, :]
        l2 = _split_layout(c * LOG2E, lane) + jnp.where(lane == L2_ONE, 1.0, 0.0)
        kin = jnp.concatenate([hr[:, 0:256], l2], axis=1).astype(BF16)
        ka_ref[...] = jnp.dot(kin, permk_ref[...], preferred_element_type=F32).astype(BF16)
        l2t = l2.T[0:HEAD_DIM, :].astype(BF16)
        qextra = jnp.dot(permq_ref[...], l2t, preferred_element_type=F32)
        for h in range(C_HEADS):
            rows = slice(h * HEAD_DIM, (h + 1) * HEAD_DIM)
            qa_ref[h, 0:HEAD_DIM, :] = (ht[rows, :] * QSCALE).astype(BF16)
            qa_ref[h, HEAD_DIM:2 * HEAD_DIM, :] = qextra[rows, :].astype(BF16)
        ktp_ref[...] = ht[1024:1280, :]
        vtp_ref[...] = ht[1280:1536, :]
        lftp_ref[...] = lft

    @pl.when(i == N_PTILES)
    def _():
        ht = lax.dot_general(wt_ref[...], xb[0:DEC_BATCH, :], nt_dims, preferred_element_type=F32)
        qts_ref[...] = ht[0:1024, :]
        kts_ref[...] = ht[1024:1280, :]
        vts_ref[...] = ht[1280:1536, :]
        lfts_ref[...] = lft[:, 0:DEC_BATCH]


def _odd_in(x, wt_bf, wkf_bf, bf, perm_k, perm_q):
    full = lambda shape: pl.BlockSpec(shape, lambda i: (0,) * len(shape))
    bidx = lambda i: jnp.minimum(i // TILES_PER_SEQ, BATCH - 1)
    tidx = lambda i: jnp.where(i < N_PTILES, i % TILES_PER_SEQ, TILES_PER_SEQ - 1)
    tpose = lambda rows: pl.BlockSpec((None, rows, TM), lambda i: (bidx(i), 0, tidx(i)))
    return pl.pallas_call(
        _odd_in_kernel,
        grid=(N_TILES,),
        in_specs=[pl.BlockSpec((TM, D_MODEL), lambda i: (i, 0)), full((1536, D_MODEL)), full((D_MODEL, 384)),
                  full((1, LANES)), full((256 + LANES, C_KV * LANES)), full((C_HEADS * HEAD_DIM, HEAD_DIM))],
        out_specs=[pl.BlockSpec((None, C_HEADS, 2 * HEAD_DIM, TM), lambda i: (bidx(i), 0, 0, tidx(i))),
                   pl.BlockSpec((TM, C_KV * LANES), lambda i: (jnp.minimum(i, N_PTILES - 1), 0)),
                   tpose(256), tpose(256), tpose(C_HEADS),
                   full((1024, DEC_BATCH)), full((256, DEC_BATCH)), full((256, DEC_BATCH)),
                   full((C_HEADS, DEC_BATCH))],
        out_shape=[jax.ShapeDtypeStruct((BATCH, C_HEADS, 2 * HEAD_DIM, SEQ), BF16),
                   jax.ShapeDtypeStruct((N_PROMPT, C_KV * LANES), BF16),
                   jax.ShapeDtypeStruct((BATCH, 256, SEQ), F32),
                   jax.ShapeDtypeStruct((BATCH, 256, SEQ), F32),
                   jax.ShapeDtypeStruct((BATCH, C_HEADS, SEQ), F32),
                   jax.ShapeDtypeStruct((1024, DEC_BATCH), F32),
                   jax.ShapeDtypeStruct((256, DEC_BATCH), F32),
                   jax.ShapeDtypeStruct((256, DEC_BATCH), F32),
                   jax.ShapeDtypeStruct((C_HEADS, DEC_BATCH), F32)],
        scratch_shapes=[pltpu.VMEM((8, LANES), F32), pltpu.VMEM((TM, LANES), F32)],
        compiler_params=_cparams("arbitrary"),
        name="odd_in",
    )(x, wt_bf, wkf_bf, bf, perm_k, perm_q)


FT = 256
ACC_ROWS = HEAD_DIM + 16


def _fox_prompt_kernel(qa_ref, ka_ref, vt_ref, o_ref, *scratch):
    acc_refs = scratch[0:C_GROUP]
    st_refs = scratch[C_GROUP:2 * C_GROUP]
    srow = lax.broadcasted_iota(jnp.int32, (FT, FT), 0)
    tcol = lax.broadcasted_iota(jnp.int32, (FT, FT), 1)

    def q_tile(qi, _):
        t0 = pl.multiple_of(qi * FT, FT)
        qts = [qa_ref[g, :, pl.ds(t0, FT)] for g in range(C_GROUP)]
        for acc_ref in acc_refs:
            acc_ref[...] = jnp.zeros_like(acc_ref)

        def chunk(s0, nk, ms, masked):
            ka = ka_ref[pl.ds(s0, nk), :]
            vt = jnp.concatenate([vt_ref[:, pl.ds(s0, nk)], jnp.ones((ACC_ROWS - HEAD_DIM, nk), F32)],
                                 axis=0).astype(BF16)
            bmax = []
            for g in range(C_GROUP):
                st = jnp.dot(ka, qts[g], preferred_element_type=F32)
                if masked:
                    st = jnp.where(srow <= tcol, st, NEG_INF)
                st_refs[g][0:nk, :] = st
                bmax.append(jnp.max(st, axis=0, keepdims=True))
            out = []
            for g in range(C_GROUP):
                m_new = jnp.maximum(ms[g], bmax[g])
                a = jnp.exp2(ms[g] - m_new)
                p = jnp.exp2(st_refs[g][0:nk, :] - m_new)
                acc_refs[g][...] = a * acc_refs[g][...] + jnp.dot(vt, p.astype(BF16), preferred_element_type=F32)
                out.append(m_new)
            return tuple(out)

        init = (jnp.full((1, FT), NEG_INF, F32),) * C_GROUP
        n4 = qi // 4
        rem = qi % 4
        ms = lax.fori_loop(0, n4, lambda j, c: chunk(pl.multiple_of(j * 4 * FT, 4 * FT), 4 * FT, c, False), init)
        s2 = pl.multiple_of(n4 * 4 * FT, FT)
        ms = lax.fori_loop(0, rem // 2, lambda j, c: chunk(s2, 2 * FT, c, False), ms)
        s1 = pl.multiple_of(s2 + (rem // 2) * 2 * FT, FT)
        ms = lax.fori_loop(0, rem % 2, lambda j, c: chunk(s1, FT, c, False), ms)
        chunk(t0, FT, ms, True)
        heads = []
        for g in range(C_GROUP):
            acc = acc_refs[g][...]
            heads.append(acc[0:HEAD_DIM, :] * (1.0 / acc[HEAD_DIM:HEAD_DIM + 1, :]))
        o_ref[pl.ds(t0, FT), :] = jnp.concatenate(heads, axis=0).T
        return 0

    lax.fori_loop(0, SEQ // FT, q_tile, 0)


def _fox_prompt(qa, ka, vtp):
    return pl.pallas_call(
        _fox_prompt_kernel,
        grid=(BATCH, C_KV),
        in_specs=[pl.BlockSpec((None, C_GROUP, 2 * HEAD_DIM, SEQ), lambda b, kv: (b, kv, 0, 0)),
                  pl.BlockSpec((SEQ, LANES), lambda b, kv: (b, kv)),
                  pl.BlockSpec((None, HEAD_DIM, SEQ), lambda b, kv: (b, kv, 0))],
        out_specs=pl.BlockSpec((SEQ, C_GROUP * HEAD_DIM), lambda b, kv: (b, kv)),
        out_shape=jax.ShapeDtypeStruct((N_PROMPT, C_HEADS * HEAD_DIM), F32),
        scratch_shapes=[pltpu.VMEM((ACC_ROWS, FT), F32)] * C_GROUP + [pltpu.VMEM((4 * FT, FT), F32)] * C_GROUP,
        compiler_params=_cparams("arbitrary", "arbitrary"),
        name="fox_prompt",
    )(qa, ka, vtp)


PG = 16
KVD = C_KV * HEAD_DIM


def _fox_sample_kernel(layer, pt_ref, qbd_ref, kts_ref, vts_ref, lfts_ref, kc_hbm, vc_hbm, lfc_hbm,
                       o_ref, kbuf, vbuf, lfbuf, s_ref, off_ref, sem):
    r = pl.program_id(0)
    slot = r % 2
    caches = ((kc_hbm, kbuf), (vc_hbm, vbuf), (lfc_hbm, lfbuf))

    def start_fetch(req, to_slot):
        def body(p, _):
            page = pt_ref[req * N_PAGES + p]
            for t, (src, dst) in enumerate(caches):
                pltpu.make_async_copy(src.at[layer, page], dst.at[to_slot, p], sem.at[t, to_slot]).start()
            return 0
        lax.fori_loop(0, N_PAGES, body, 0, unroll=4)

    @pl.when(r == 0)
    def _():
        start_fetch(0, 0)

    @pl.when(r + 1 < DEC_BATCH)
    def _():
        start_fetch(r + 1, 1 - slot)

    for t, (src, dst) in enumerate(caches):
        pltpu.make_async_copy(src.at[layer, pl.ds(0, N_PAGES)], dst.at[slot], sem.at[t, slot]).wait()

    qbd = qbd_ref[...].astype(BF16)
    rr = lax.broadcasted_iota(jnp.int32, (PAGE, PAGE), 0)
    cc = lax.broadcasted_iota(jnp.int32, (PAGE, PAGE), 1)
    upper = jnp.where(rr <= cc, 1.0, 0.0).astype(BF16)
    n = PG * C_HEADS
    rows = N_PAGES * C_HEADS

    tot = jnp.sum(lfbuf[slot].reshape(rows, PAGE), axis=-1, keepdims=True)
    incl = tot
    shift = C_HEADS
    while shift < rows:
        incl = incl + jnp.concatenate([jnp.zeros((shift, 1), F32), incl[0:rows - shift, :]], axis=0)
        shift *= 2
    off_ref[...] = jnp.broadcast_to(incl - tot, (rows, PAGE))
    total = incl[rows - C_HEADS:rows, :]

    def score_group(j, m):
        p0 = pl.multiple_of(j * PG, PG)
        lf_all = lfbuf[slot, pl.ds(p0, PG)].reshape(n, PAGE)
        parts = jnp.concatenate(_split3(lf_all), axis=0).astype(BF16)
        c3 = jnp.dot(parts, upper, preferred_element_type=F32)
        c_all = c3[0:n, :] + c3[n:2 * n, :] + c3[2 * n:3 * n, :] + off_ref[pl.ds(pl.multiple_of(j * n, n), n), :]
        s_parts = []
        for g in range(PG):
            kt = kbuf[slot, p0 + g].astype(BF16)
            s_parts.append(jnp.dot(qbd, kt, preferred_element_type=F32) - c_all[g * C_HEADS:(g + 1) * C_HEADS, :])
        s = jnp.concatenate(s_parts, axis=1)
        s_ref[:, pl.ds(pl.multiple_of(j * PG * PAGE, PG * PAGE), PG * PAGE)] = s
        return jnp.maximum(m, jnp.max(s, axis=-1, keepdims=True))

    m = lax.fori_loop(0, N_PAGES // PG, score_group, jnp.full((C_HEADS, 1), NEG_INF, F32))
    lane = lax.broadcasted_iota(jnp.int32, (C_HEADS, DEC_BATCH), 1)
    s_new = jnp.dot(qbd, kts_ref[...].astype(BF16), preferred_element_type=F32) - (total + lfts_ref[...])
    s_new = jnp.where(lane == r, s_new, NEG_INF)
    m = jnp.maximum(m, jnp.max(s_new, axis=-1, keepdims=True))

    def value_group(j, c):
        l, acc_t = c
        p0 = pl.multiple_of(j * PG, PG)
        p = jnp.exp(s_ref[:, pl.ds(pl.multiple_of(j * PG * PAGE, PG * PAGE), PG * PAGE)] - m)
        vt = jnp.concatenate([vbuf[slot, p0 + g] for g in range(PG)], axis=1).astype(BF16)
        acc_t = acc_t + jnp.dot(vt, p.T.astype(BF16), preferred_element_type=F32)
        return l + jnp.sum(p, axis=-1, keepdims=True), acc_t

    p_new = jnp.exp(s_new - m)
    init = (jnp.sum(p_new, axis=-1, keepdims=True),
            jnp.dot(vts_ref[...].astype(BF16), p_new.T.astype(BF16), preferred_element_type=F32))
    l, acc_t = lax.fori_loop(0, N_PAGES // PG, value_group, init)
    o_ref[...] = acc_t.T / l


def _fox_sample(layer, page_flat, qbd, kts, vts, lfts, kc, vc, lfc):
    const = lambda shape: pl.BlockSpec(shape, lambda r, pt: (0,) * len(shape))
    hbm = pl.BlockSpec(memory_space=pl.ANY)
    grid_spec = pltpu.PrefetchScalarGridSpec(
        num_scalar_prefetch=1,
        grid=(DEC_BATCH,),
        in_specs=[pl.BlockSpec((None, C_HEADS, KVD), lambda r, pt: (r, 0, 0)),
                  const((KVD, DEC_BATCH)), const((KVD, DEC_BATCH)), const((C_HEADS, DEC_BATCH)), hbm, hbm, hbm],
        out_specs=pl.BlockSpec((None, C_HEADS, KVD), lambda r, pt: (r, 0, 0)),
        scratch_shapes=[pltpu.VMEM((2, N_PAGES, KVD, PAGE), F32), pltpu.VMEM((2, N_PAGES, KVD, PAGE), F32),
                        pltpu.VMEM((2, N_PAGES, C_HEADS, PAGE), F32),
                        pltpu.VMEM((C_HEADS, (N_PAGES + 1) * PAGE), F32),
                        pltpu.VMEM((N_PAGES * C_HEADS, PAGE), F32),
                        pltpu.SemaphoreType.DMA((3, 2))],
    )
    return pl.pallas_call(
        functools.partial(_fox_sample_kernel, layer),
        grid_spec=grid_spec,
        out_shape=jax.ShapeDtypeStruct((DEC_BATCH, C_HEADS, KVD), F32),
        compiler_params=pltpu.CompilerParams(dimension_semantics=("arbitrary",), vmem_limit_bytes=VMEM_LIMIT,
                                             disable_bounds_checks=True),
        name="fox_sample",
    )(page_flat, qbd, kts, vts, lfts, kc, vc, lfc)


def _t5_bucket(dist):
    n = jnp.maximum(dist, 0)
    max_exact = N_BUCKETS // 2
    nf = jnp.maximum(n, 1).astype(F32)
    large = max_exact + (jnp.log(nf / max_exact) / math.log(MAX_DISTANCE / max_exact)
                         * (N_BUCKETS - max_exact)).astype(jnp.int32)
    large = jnp.minimum(large, N_BUCKETS - 1)
    return jnp.where(n < max_exact, n, large)


def kernel(x_prompt, x_sample, cache_swa_k, cache_swa_v, state_conv, cache_fox_k, cache_fox_v, cache_fox_logf, page_table, rel_bias_table, attn_sinks, w_in_even, conv_w, w_out_even, w_in_odd, b_forget, w_out_odd, ln_g, ln_b, w_group, b_group, w_router, b_router, w_gate, w_up, w_down):
    x = jnp.concatenate([x_prompt.reshape(N_PROMPT, D_MODEL), x_sample.reshape(DEC_BATCH, D_MODEL),
                         jnp.zeros((N_TOK - N_PROMPT - DEC_BATCH, D_MODEL), F32)], axis=0)

    qi = jnp.arange(WINDOW)[:, None]
    kj = jnp.arange(2 * WINDOW)[None, :]
    dist = WINDOW + qi - kj
    band = (dist >= 0) & (dist < WINDOW)
    onehot = (_t5_bucket(dist)[:, :, None] == jnp.arange(N_BUCKETS)[None, None, :]).astype(F32)
    bias_t = jnp.einsum('qkb,bh->hkq', onehot, rel_bias_table, precision=lax.Precision.HIGHEST)
    bias_p = jnp.stack([jnp.where((band & (kj >= WINDOW)).T[None], bias_t, NEG_INF),
                        jnp.where(band.T[None], bias_t, NEG_INF)])
    bias_s = rel_bias_table[_t5_bucket(WINDOW - jnp.arange(WINDOW))].T
    bnew_b = jnp.broadcast_to(rel_bias_table[0][:, None], (A_HEADS, LANES))

    swa_kt = jnp.transpose(cache_swa_k, (0, 1, 3, 4, 2))
    swa_vt = jnp.transpose(cache_swa_v, (0, 1, 3, 4, 2))
    pool = cache_fox_k.shape[1]
    fox_kt = jnp.transpose(cache_fox_k, (0, 1, 3, 4, 2)).reshape(DEPTH // 2, pool, KVD, PAGE)
    fox_vt = jnp.transpose(cache_fox_v, (0, 1, 3, 4, 2)).reshape(DEPTH // 2, pool, KVD, PAGE)
    fox_lft = jnp.transpose(cache_fox_logf, (0, 1, 3, 2))
    page_flat = page_table.reshape(-1)

    wr = jnp.concatenate([w_router, w_group], axis=-1)
    wr_hi = wr.astype(BF16)
    wr_lo = (wr - wr_hi.astype(F32)).astype(BF16)
    zpad = lambda n: jnp.zeros((DEPTH, D_MODEL, n), BF16)
    wr_all = jnp.concatenate([wr_hi, zpad(12), wr_lo, zpad(LANES - 52)], axis=-1)
    br_all = jnp.concatenate([b_router, b_group, jnp.zeros((DEPTH, LANES - 20), F32)], axis=-1)
    eye = jnp.eye(C_KV, dtype=F32)
    perm_k, perm_q = _aug_constants()

    outs = {n: [] for n in ("swa_kp", "swa_vp", "swa_ks", "swa_vs", "conv_p", "conv_s",
                            "fox_kp", "fox_vp", "fox_lp", "fox_ks", "fox_vs", "fox_ls")}
    for layer in range(DEPTH):
        if layer % 2 == 0:
            e = layer // 2
            s0 = state_conv[e, :, 0, :]
            s1 = state_conv[e, :, 1, :]
            q, k, v, cv, convp, us, klt, vlt = _even_in(x, w_in_even[e].astype(BF16), conv_w[e], s0, s1)
            sink_b = jnp.broadcast_to(attn_sinks[e][:, None], (A_HEADS, LANES))
            att_p = _swa_prompt(q, k, v, bias_p, sink_b)
            att_s = _swa_sample(q, k, v, swa_kt[e], swa_vt[e], bias_s, bnew_b, sink_b)
            conv = cv
            w_out = w_out_even[e]
            ks =k[N_PROMPT:N_PROMPT + DEC_BATCH].reshape(DEC_BATCH, 1, A_KV, HEAD_DIM)
            vs = v[N_PROMPT:N_PROMPT + DEC_BATCH].reshape(DEC_BATCH, 1, A_KV, HEAD_DIM)
            last = lambda t: jnp.transpose(t.reshape(BATCH, A_KV, HEAD_DIM, WINDOW), (0, 3, 1, 2))
            outs["swa_kp"].append(last(klt))
            outs["swa_vp"].append(last(vlt))
            outs["swa_ks"].append(jnp.concatenate([cache_swa_k[e][:, 1:], ks], axis=1))
            outs["swa_vs"].append(jnp.concatenate([cache_swa_v[e][:, 1:], vs], axis=1))
            outs["conv_p"].append(convp)
            outs["conv_s"].append(jnp.stack([s1, us], axis=1))
        else:
            o = layer // 2
            wt_bf = jnp.transpose(w_in_odd[o][:, 0:1536]).astype(BF16)
            wkf_bf = jnp.concatenate([w_in_odd[o][:, 1024:1280], w_in_odd[o][:, 1536:1552],
                                      jnp.zeros((D_MODEL, 384 - 272), F32)], axis=-1).astype(BF16)
            bf_pad = jnp.concatenate([b_forget[o], jnp.zeros((LANES - C_HEADS,), F32)])[None, :]
            qa, ka, ktp, vtp, lftp, qts, kts, vts, lfts = _odd_in(x, wt_bf, wkf_bf, bf_pad, perm_k, perm_q)
            att_p = _fox_prompt(qa, ka, vtp)
            qs = qts.T.reshape(DEC_BATCH, C_KV, C_GROUP, 1, HEAD_DIM) * SCALE
            qbd = (qs * eye[None, :, None, :, None]).reshape(DEC_BATCH, C_HEADS, KVD)
            of = _fox_sample(o, page_flat, qbd, kts, vts, lfts, fox_kt, fox_vt, fox_lft)
            of = of.reshape(DEC_BATCH, C_KV, C_GROUP, C_KV, HEAD_DIM)
            att_s = jnp.sum(of * eye[None, :, None, :, None], axis=3).reshape(DEC_BATCH, C_HEADS * HEAD_DIM)
            att_s = jnp.concatenate([att_s, jnp.zeros((TM - DEC_BATCH, C_HEADS * HEAD_DIM), F32)], axis=0)
            conv = None
            w_out = w_out_odd[o]
            outs["fox_kp"].append(jnp.transpose(ktp.reshape(BATCH, C_KV, HEAD_DIM, SEQ), (0, 3, 1, 2)))
            outs["fox_vp"].append(jnp.transpose(vtp.reshape(BATCH, C_KV, HEAD_DIM, SEQ), (0, 3, 1, 2)))
            outs["fox_lp"].append(jnp.transpose(lftp, (0, 2, 1)))
            outs["fox_ks"].append(jnp.transpose(kts.reshape(C_KV, HEAD_DIM, DEC_BATCH), (2, 0, 1))[:, None])
            outs["fox_vs"].append(jnp.transpose(vts.reshape(C_KV, HEAD_DIM, DEC_BATCH), (2, 0, 1))[:, None])
            outs["fox_ls"].append(lfts.T[:, None, :])
        x1, rinfo, totals = _post_mix(att_p, att_s, conv, x, w_out.astype(BF16), ln_g[layer, 0][None, :], ln_b[layer, 0][None, :],
                                      wr_all[layer], br_all[layer][None, :])
        x = _moe(x1, rinfo, totals, layer, w_gate, w_up, w_down,
                 ln_g[layer, 1][None, :], ln_b[layer, 1][None, :], layer == DEPTH - 1)

    st = {n: jnp.stack(vl) for n, vl in outs.items()}
    y_prompt = x[0].reshape(BATCH, SEQ, D_MODEL)
    y_sample = x[1].reshape(DEC_BATCH, 1, D_MODEL)
    return (y_prompt, y_sample, st["swa_kp"], st["swa_vp"], st["swa_ks"], st["swa_vs"], st["conv_p"], st["conv_s"],
            st["fox_kp"], st["fox_vp"], st["fox_lp"], st["fox_ks"], st["fox_vs"], st["fox_ls"])
```

```python
import functools
import math

import jax
import jax.numpy as jnp
import numpy as np
from jax import lax
from jax.experimental import pallas as pl
from jax.experimental.pallas import tpu as pltpu

F32 = jnp.float32
BF16 = jnp.bfloat16

D_MODEL = 1024
BATCH = 4
SEQ = 4096
DEC_BATCH = 128
PAGE = 128
N_PAGES = 64
HEAD_DIM = 64
A_HEADS = 8
A_KV = 2
WINDOW = 128
B_WIDTH = 512
C_HEADS = 16
C_KV = 4
N_BUCKETS = 32
MAX_DISTANCE = 128
N_GROUPS = 4
N_EXPERTS = 16
D_EXPERT = 256
DEPTH = 4
ALPHA = (2.0 * DEPTH) ** 0.25
LN_EPS = 1e-5
NEG_INF = -1e30
SCALE = HEAD_DIM ** -0.5

N_PROMPT = BATCH * SEQ
TM = 512
N_PTILES = N_PROMPT // TM
TILES_PER_SEQ = SEQ // TM
N_TILES = N_PTILES + 1
N_TOK = N_TILES * TM
LANES = 128

TE = 512
TE_SUB = 256
N_SLOTS = 2 * N_TOK
N_VISITS = N_SLOTS // TE + N_EXPERTS - 1

VMEM_LIMIT = 56 * 1024 * 1024


def _cparams(*sem):
    return pltpu.CompilerParams(dimension_semantics=sem, vmem_limit_bytes=VMEM_LIMIT)


def _layer_norm(y, g, b):
    mu = jnp.mean(y, axis=-1, keepdims=True)
    yc = y - mu
    var = jnp.mean(yc * yc, axis=-1, keepdims=True)
    return yc * lax.rsqrt(var + LN_EPS) * g + b


def _even_in_kernel(x_ref, w_ref, cw_ref, s0_ref, s1_ref,
                    q_ref, k_ref, v_ref, cv_ref, convp_ref, us_ref, klt_ref, vlt_ref, carry_ref):
    i = pl.program_id(0)
    h = jnp.dot(x_ref[...].astype(BF16), w_ref[...], preferred_element_type=F32)
    q_ref[...] = h[:, 0:512]
    k_ref[...] = h[:, 512:640]
    v_ref[...] = h[:, 640:768]
    bg = h[:, 768:1280]
    u = h[:, 1280:1792] * h[:, 1792:2304]
    w0 = cw_ref[0:1, :]
    w1 = cw_ref[1:2, :]
    w2 = cw_ref[2:3, :]

    @pl.when(i < N_PTILES)
    def _():
        @pl.when(i % TILES_PER_SEQ == 0)
        def _():
            carry_ref[...] = jnp.zeros_like(carry_ref)
        row = lax.broadcasted_iota(jnp.int32, (TM, B_WIDTH), 0)
        c2 = carry_ref[0:1, :]
        c1 = carry_ref[1:2, :]
        u1 = jnp.where(row == 0, c1, pltpu.roll(u, 1, 0))
        u2 = jnp.where(row == 0, c2, jnp.where(row == 1, c1, pltpu.roll(u, 2, 0)))
        cv_ref[...] = bg * (w0 * u2 + w1 * u1 + w2 * u)
        carry_ref[0:2, :] = u[TM - 2:TM, :]
        convp_ref[...] = u[TM - 2:TM, :]

        @pl.when(i % TILES_PER_SEQ == TILES_PER_SEQ - 1)
        def _():
            klt_ref[...] = h[TM - WINDOW:TM, 512:640].T
            vlt_ref[...] = h[TM - WINDOW:TM, 640:768].T

    @pl.when(i == N_PTILES)
    def _():
        us = u[0:DEC_BATCH, :]
        z = w0 * s0_ref[...] + w1 * s1_ref[...] + w2 * us
        cv_ref[0:DEC_BATCH, :] = bg[0:DEC_BATCH, :] * z
        cv_ref[DEC_BATCH:TM, :] = jnp.zeros((TM - DEC_BATCH, B_WIDTH), F32)
        us_ref[...] = us


def _even_in(x, w_bf, cw, s0, s1):
    tile = lambda n: pl.BlockSpec((TM, n), lambda i: (i, 0))
    full = lambda shape: pl.BlockSpec(shape, lambda i: (0,) * len(shape))
    return pl.pallas_call(
        _even_in_kernel,
        grid=(N_TILES,),
        in_specs=[tile(D_MODEL), full((D_MODEL, 2304)), full((3, B_WIDTH)),
                  full((DEC_BATCH, B_WIDTH)), full((DEC_BATCH, B_WIDTH))],
        out_specs=[tile(512), tile(128), tile(128), tile(B_WIDTH),
                   pl.BlockSpec((None, 2, B_WIDTH),
                                lambda i: (jnp.minimum(i // TILES_PER_SEQ, BATCH - 1), 0, 0)),
                   full((DEC_BATCH, B_WIDTH)),
                   pl.BlockSpec((None, 128, WINDOW), lambda i: (jnp.minimum(i // TILES_PER_SEQ, BATCH - 1), 0, 0)),
                   pl.BlockSpec((None, 128, WINDOW), lambda i: (jnp.minimum(i // TILES_PER_SEQ, BATCH - 1), 0, 0))],
        out_shape=[jax.ShapeDtypeStruct((N_TOK, 512), F32),
                   jax.ShapeDtypeStruct((N_TOK, 128), F32),
                   jax.ShapeDtypeStruct((N_TOK, 128), F32),
                   jax.ShapeDtypeStruct((N_TOK, B_WIDTH), F32),
                   jax.ShapeDtypeStruct((BATCH, 2, B_WIDTH), F32),
                   jax.ShapeDtypeStruct((DEC_BATCH, B_WIDTH), F32),
                   jax.ShapeDtypeStruct((BATCH, 128, WINDOW), F32),
                   jax.ShapeDtypeStruct((BATCH, 128, WINDOW), F32)],
        scratch_shapes=[pltpu.VMEM((8, B_WIDTH), F32)],
        compiler_params=_cparams("arbitrary"),
        name="even_in",
    )(x, w_bf, cw, s0, s1)


def _swa_prompt_kernel(q_ref, kc_ref, kp_ref, vc_ref, vp_ref, bias_ref, sink_ref, o_ref, *st_refs):
    qt = (q_ref[...] * SCALE).T
    kk = jnp.concatenate([kp_ref[...], kc_ref[...]], axis=0).astype(BF16)
    vvt = jnp.concatenate([vp_ref[...], vc_ref[...]], axis=0).T
    zeros = jnp.zeros((HEAD_DIM, WINDOW), F32)
    ones = jnp.ones((SWA_ACC_ROWS - HEAD_DIM, 2 * WINDOW), F32)
    cmax = []
    for h in range(A_HEADS):
        kv = h // (A_HEADS // A_KV)
        qh = qt[h * HEAD_DIM:(h + 1) * HEAD_DIM, :]
        qh = jnp.concatenate([qh, zeros] if kv == 0 else [zeros, qh], axis=0).astype(BF16)
        st = jnp.dot(kk, qh, preferred_element_type=F32) + bias_ref[h]
        st_refs[h][...] = st
        cmax.append(jnp.max(st, axis=0, keepdims=True))
    outs = []
    for h in range(A_HEADS):
        kv = h // (A_HEADS // A_KV)
        sk = sink_ref[h:h + 1, :]
        m = jnp.maximum(cmax[h], sk)
        p = jnp.exp(st_refs[h][...] - m)
        vt = jnp.concatenate([vvt[kv * HEAD_DIM:(kv + 1) * HEAD_DIM, :], ones], axis=0).astype(BF16)
        acc = jnp.dot(vt, p.astype(BF16), preferred_element_type=F32)
        den = acc[HEAD_DIM:HEAD_DIM + 1, :] + jnp.exp(sk - m)
        outs.append(acc[0:HEAD_DIM, :] / den)
    o_ref[...] = jnp.concatenate(outs, axis=0).T


def _swa_prompt(q, k, v, bias, sink_b):
    nblk = SEQ // WINDOW
    cur = lambda b, j: (b * nblk + j, 0)
    prev = lambda b, j: (jnp.maximum(b * nblk + j - 1, 0), 0)
    return pl.pallas_call(
        _swa_prompt_kernel,
        grid=(BATCH, nblk),
        in_specs=[pl.BlockSpec((WINDOW, 512), cur),
                  pl.BlockSpec((WINDOW, 128), cur), pl.BlockSpec((WINDOW, 128), prev),
                  pl.BlockSpec((WINDOW, 128), cur), pl.BlockSpec((WINDOW, 128), prev),
                  pl.BlockSpec((None, A_HEADS, 2 * WINDOW, WINDOW), lambda b, j: (jnp.minimum(j, 1), 0, 0, 0)),
                  pl.BlockSpec((A_HEADS, LANES), lambda b, j: (0, 0))],
        out_specs=pl.BlockSpec((WINDOW, 512), cur),
        out_shape=jax.ShapeDtypeStruct((N_PROMPT, 512), F32),
        scratch_shapes=[pltpu.VMEM((2 * WINDOW, WINDOW), F32)] * A_HEADS,
        compiler_params=_cparams("arbitrary", "arbitrary"),
        name="swa_prompt",
    )(q, k, k, v, v, bias, sink_b)


SWA_RB = 8
SWA_ACC_ROWS = HEAD_DIM + 16


def _swa_sample_kernel(q_ref, k_ref, v_ref, kt_ref, vt_ref, bias_ref, bnew_ref, sink_ref, o_ref):

    @pl.when(pl.program_id(0) >= DEC_BATCH // SWA_RB)
    def _():
        o_ref[...] = jnp.zeros_like(o_ref)

    @pl.when(pl.program_id(0) < DEC_BATCH // SWA_RB)
    def _():
        _swa_sample_rows(q_ref, k_ref, v_ref, kt_ref, vt_ref, bias_ref, bnew_ref, sink_ref, o_ref)


def _swa_sample_rows(q_ref, k_ref, v_ref, kt_ref, vt_ref, bias_ref, bnew_ref, sink_ref, o_ref):
    lane = lax.broadcasted_iota(jnp.int32, (4, WINDOW), 1)
    grp = A_HEADS // A_KV
    results = []
    for r in range(SWA_RB):
        for kv in range(A_KV):
            hs = slice(kv * grp, (kv + 1) * grp)
            q4 = jnp.concatenate(
                [q_ref[r:r + 1, (kv * grp + g) * HEAD_DIM:(kv * grp + g + 1) * HEAD_DIM] for g in range(grp)],
                axis=0)
            kt = kt_ref[r, kv]
            vt = vt_ref[r, kv]
            s_old = jnp.dot(q4.astype(BF16), kt.astype(BF16), preferred_element_type=F32)
            s_old = s_old * SCALE + bias_ref[hs, :]
            s_old = jnp.where(lane >= 1, s_old, NEG_INF)
            kn = k_ref[r:r + 1, kv * HEAD_DIM:(kv + 1) * HEAD_DIM]
            vn = v_ref[r:r + 1, kv * HEAD_DIM:(kv + 1) * HEAD_DIM]
            s_new = jnp.sum(q4 * kn, axis=-1, keepdims=True) * SCALE + bnew_ref[hs, 0:1]
            sk = sink_ref[hs, 0:1]
            m = jnp.maximum(jnp.maximum(jnp.max(s_old, axis=-1, keepdims=True), s_new), sk)
            p_old = jnp.exp(s_old - m)
            p_new = jnp.exp(s_new - m)
            den = jnp.sum(p_old, axis=-1, keepdims=True) + p_new + jnp.exp(sk - m)
            o = lax.dot_general(p_old.astype(BF16), vt.astype(BF16), (((1,), (1,)), ((), ())),
                                preferred_element_type=F32)
            results.append((o + p_new * vn) / den)
    for r in range(SWA_RB):
        for kv in range(A_KV):
            o = results[r * A_KV + kv]
            for g in range(grp):
                hh = kv * grp + g
                o_ref[r:r + 1, hh * HEAD_DIM:(hh + 1) * HEAD_DIM] = o[g:g + 1, :]


def _swa_sample(q, k, v, kt, vt, bias_s, bnew_b, sink_b):
    base = N_PROMPT // SWA_RB
    rows = lambda n: pl.BlockSpec((SWA_RB, n), lambda i: (base + i, 0))
    cache = pl.BlockSpec((SWA_RB, A_KV, HEAD_DIM, WINDOW),
                         lambda i: (jnp.minimum(i, DEC_BATCH // SWA_RB - 1), 0, 0, 0))
    small = pl.BlockSpec((A_HEADS, LANES), lambda i: (0, 0))
    return pl.pallas_call(
        _swa_sample_kernel,
        grid=(TM // SWA_RB,),
        in_specs=[rows(512), rows(128), rows(128), cache, cache, small, small, small],
        out_specs=pl.BlockSpec((SWA_RB, 512), lambda i: (i, 0)),
        out_shape=jax.ShapeDtypeStruct((TM, 512), F32),
        compiler_params=_cparams("arbitrary"),
        name="swa_sample",
    )(q, k, v, kt, vt, bias_s, bnew_b, sink_b)


def _route(x1, wr_ref, br_ref, below_ref, cnt_ref):
    x_hi = x1.astype(BF16)
    x_lo = (x1 - x_hi.astype(F32)).astype(BF16)
    pa = jnp.dot(x_hi, wr_ref[...], preferred_element_type=F32)
    pb = jnp.dot(x_lo, wr_ref[...], preferred_element_type=F32)
    logits = pa + pltpu.roll(pa, LANES - 32, 1) + pb + br_ref[...]
    lane = lax.broadcasted_iota(jnp.int32, logits.shape, 1)
    lane_f = lane.astype(F32)
    lane_grp = (lane >> 2).astype(F32)
    is_grp = (lane >= N_EXPERTS) & (lane < N_EXPERTS + N_GROUPS)
    big = 1e9
    gl = jnp.where(is_grp, logits, NEG_INF)
    gmax = jnp.max(gl, axis=-1, keepdims=True)
    gidx = jnp.min(jnp.where(is_grp & (logits == gmax), lane_f - N_EXPERTS, big), axis=-1, keepdims=True)
    gsum = jnp.sum(jnp.where(is_grp, jnp.exp(gl - gmax), 0.0), axis=-1, keepdims=True)
    grp_w = 1.0 / gsum
    in_grp = (lane < N_EXPERTS) & (lane_grp == gidx)
    e1 = jnp.where(in_grp, logits, NEG_INF)
    t1 = jnp.max(e1, axis=-1, keepdims=True)
    i1 = jnp.min(jnp.where(in_grp & (logits == t1), lane_f, big), axis=-1, keepdims=True)
    rest = in_grp & (lane_f != i1)
    e2 = jnp.where(rest, logits, NEG_INF)
    t2 = jnp.max(e2, axis=-1, keepdims=True)
    i2 = jnp.min(jnp.where(rest & (logits == t2), lane_f, big), axis=-1, keepdims=True)
    ex = jnp.exp(t2 - t1)
    g1 = grp_w / (1.0 + ex)
    g2 = grp_w * ex / (1.0 + ex)
    sel1 = lane_f == i1
    sel2 = lane_f == i2
    onehot = jnp.where(sel1 | sel2, 1.0, 0.0)
    before = jnp.dot(below_ref[...], onehot.astype(BF16), preferred_element_type=F32) + cnt_ref[0:1, :]
    rank1 = jnp.sum(jnp.where(sel1, before, 0.0), axis=-1, keepdims=True)
    rank2 = jnp.sum(jnp.where(sel2, before, 0.0), axis=-1, keepdims=True)
    cnt_ref[0:1, :] = cnt_ref[0:1, :] + jnp.sum(onehot, axis=0, keepdims=True)
    out = jnp.where(lane == 0, i1, 0.0)
    out = jnp.where(lane == 1, i2, out)
    out = jnp.where(lane == 2, g1, out)
    out = jnp.where(lane == 3, g2, out)
    out = jnp.where(lane == 4, rank1, out)
    out = jnp.where(lane == 5, rank2, out)
    return out


def _post_mix_kernel(has_conv, *refs):
    n_in = 3 if has_conv else 2
    att_p_ref, att_s_ref = refs[0:2]
    x_ref, w_ref, g_ref, b_ref, wr_ref, br_ref, below_ref, x1_ref, r_ref, tot_ref, cnt_ref = refs[n_in:]
    i = pl.program_id(0)

    @pl.when(i == 0)
    def _():
        cnt_ref[...] = jnp.zeros_like(cnt_ref)
    att = jnp.where(i < N_PTILES, att_p_ref[...], att_s_ref[...]).astype(BF16)
    kw = att.shape[1]
    mix = jnp.dot(att, w_ref[0:kw, :], preferred_element_type=F32)
    if has_conv:
        mix = mix + jnp.dot(refs[2][...].astype(BF16), w_ref[kw:D_MODEL, :], preferred_element_type=F32)
    x1 = _layer_norm(ALPHA * x_ref[...] + mix, g_ref[...], b_ref[...])
    x1_ref[...] = x1
    r_ref[...] = _route(x1, wr_ref, br_ref, below_ref, cnt_ref)
    tot_ref[...] = cnt_ref[...]


def _post_mix(att_p, att_s, conv, x, w_bf, g, b, wr_bf, br):
    kw = att_p.shape[1]
    tile = lambda n: pl.BlockSpec((TM, n), lambda i: (i, 0))
    full = lambda shape: pl.BlockSpec(shape, lambda i: (0,) * len(shape))
    mix_specs = [pl.BlockSpec((TM, kw), lambda i: (jnp.minimum(i, N_PTILES - 1), 0)), full((TM, kw))]
    mixes = [att_p, att_s]
    if conv is not None:
        mix_specs.append(tile(D_MODEL - kw))
        mixes.append(conv)
    return pl.pallas_call(
        functools.partial(_post_mix_kernel, conv is not None),
        grid=(N_TILES,),
        in_specs=mix_specs + [tile(D_MODEL), full((D_MODEL, D_MODEL)), full((1, D_MODEL)),
                              full((1, D_MODEL)), full((D_MODEL, LANES)), full((1, LANES)), full((TM, TM))],
        out_specs=[tile(D_MODEL), tile(LANES), full((8, LANES))],
        out_shape=[jax.ShapeDtypeStruct((N_TOK, D_MODEL), F32),
                   jax.ShapeDtypeStruct((N_TOK, LANES), F32),
                   jax.ShapeDtypeStruct((8, LANES), F32)],
        scratch_shapes=[pltpu.VMEM((8, LANES), F32)],
        compiler_params=_cparams("arbitrary"),
        name="post_mix",
    )(*mixes, x, w_bf, g, b, wr_bf, br, jnp.asarray(np.tril(np.ones((TM, TM), np.float32), -1), BF16))


def _expert_kernel(vt_ref, ve_ref, lo_ref, hi_ref, first_ref, newexp_ref, xs_ref, wg_ref, wu_ref, wd_ref, ys_ref,
                   wgub_ref, wdb_ref):
    del vt_ref, ve_ref
    v = pl.program_id(0)
    lo = lo_ref[v]
    hi = hi_ref[v]

    @pl.when((hi > lo) & (newexp_ref[v] == 1))
    def _():
        wgub_ref[:, 0:D_EXPERT] = wg_ref[...].astype(BF16)
        wgub_ref[:, D_EXPERT:2 * D_EXPERT] = wu_ref[...].astype(BF16)
        wdb_ref[...] = wd_ref[...].astype(BF16)

    @pl.when(hi > lo)
    def _():
        row = lax.broadcasted_iota(jnp.int32, (TE, 1), 0)
        mine = jnp.where((row >= lo) & (row < hi), 1.0, 0.0)
        wgu = wgub_ref[...]
        wd = wdb_ref[...]
        parts = []
        for s in range(TE // TE_SUB):
            rows = slice(s * TE_SUB, (s + 1) * TE_SUB)
            gu = jnp.dot(xs_ref[rows, :].astype(BF16), wgu, preferred_element_type=F32)
            g = gu[:, 0:D_EXPERT]
            u = gu[:, D_EXPERT:2 * D_EXPERT]
            h = g * (1.0 / (1.0 + jnp.exp(-g))) * u * mine[rows, :]
            parts.append(jnp.dot(h.astype(BF16), wd, preferred_element_type=F32))
        y = jnp.concatenate(parts, axis=0)

        @pl.when(first_ref[v] == 1)
        def _():
            ys_ref[...] = y

        @pl.when(first_ref[v] == 0)
        def _():
            ys_ref[...] += y


def _experts(meta, xs, layer, wg, wu, wd):
    tile_map = lambda v, vt, ve, lo, hi, fi, ne: (vt[v], 0)
    w_map = lambda v, vt, ve, lo, hi, fi, ne: (layer, ve[v], 0, 0)
    grid_spec = pltpu.PrefetchScalarGridSpec(
        num_scalar_prefetch=6,
        grid=(N_VISITS,),
        in_specs=[pl.BlockSpec((TE, D_MODEL), tile_map),
                  pl.BlockSpec((None, None, D_MODEL, D_EXPERT), w_map),
                  pl.BlockSpec((None, None, D_MODEL, D_EXPERT), w_map),
                  pl.BlockSpec((None, None, D_EXPERT, D_MODEL), w_map)],
        out_specs=pl.BlockSpec((TE, D_MODEL), tile_map),
        scratch_shapes=[pltpu.VMEM((D_MODEL, 2 * D_EXPERT), BF16), pltpu.VMEM((D_EXPERT, D_MODEL), BF16)],
    )
    return pl.pallas_call(
        _expert_kernel,
        grid_spec=grid_spec,
        out_shape=jax.ShapeDtypeStruct((N_SLOTS, D_MODEL), F32),
        compiler_params=_cparams("arbitrary"),
        name="experts",
    )(*meta, xs, wg, wu, wd)


def _dispatch(rinfo, totals):
    ids = rinfo[:, 0:2].astype(jnp.int32)
    ranks = rinfo[:, 4:6].astype(jnp.int32)
    counts = totals[0, 0:N_EXPERTS].astype(jnp.int32)
    e_iota = jnp.arange(N_EXPERTS, dtype=jnp.int32)
    incl = (e_iota[None, :] <= e_iota[:, None]).astype(jnp.int32)
    end = jnp.sum(incl * counts[None, :], axis=1)
    off = end - counts
    pos_of_pair = jnp.sum(jnp.where(e_iota[None, None, :] < ids[:, :, None], counts[None, None, :], 0), axis=2) + ranks
    tok = lax.broadcasted_iota(jnp.int32, (N_TOK, 2), 0)
    _, sorted_tok = lax.sort((ids.reshape(-1), tok.reshape(-1)), num_keys=1, is_stable=True)
    first_tile = off // TE
    n_vis = jnp.where(counts > 0, (end - 1) // TE - first_tile + 1, 0)
    v_end = jnp.sum(incl * n_vis[None, :], axis=1)
    v = jnp.arange(N_VISITS, dtype=jnp.int32)
    valid = v < v_end[N_EXPERTS - 1]
    e_v = jnp.minimum(jnp.sum((v[:, None] >= v_end[None, :]).astype(jnp.int32), axis=1), N_EXPERTS - 1)
    pick = (e_v[:, None] == e_iota[None, :]).astype(jnp.int32)
    sel = lambda a: jnp.sum(pick * a[None, :], axis=1)
    tile_v = jnp.where(valid, sel(first_tile) + v - sel(v_end - n_vis), N_SLOTS // TE - 1)
    lo = jnp.where(valid, jnp.clip(sel(off) - tile_v * TE, 0, TE), 0)
    hi = jnp.where(valid, jnp.clip(sel(end) - tile_v * TE, 0, TE), 0)
    prev_tile = jnp.concatenate([jnp.full((1,), -1, jnp.int32), tile_v[:-1]])
    prev_e = jnp.concatenate([jnp.full((1,), -1, jnp.int32), e_v[:-1]])
    first = (valid & (tile_v != prev_tile)).astype(jnp.int32)
    newexp = (valid & (e_v != prev_e)).astype(jnp.int32)
    meta = (tile_v.astype(jnp.int32), e_v.astype(jnp.int32), lo.astype(jnp.int32), hi.astype(jnp.int32), first, newexp)
    return sorted_tok, pos_of_pair, meta


def _combine_ln(x1_ref, ya_ref, yb_ref, r_ref, g_ref, b_ref):
    r = r_ref[...]
    moe = r[:, 2:3] * ya_ref[...] + r[:, 3:4] * yb_ref[...]
    return _layer_norm(ALPHA * x1_ref[...] + moe, g_ref[...], b_ref[...])


def _ln2_kernel(x1_ref, ya_ref, yb_ref, r_ref, g_ref, b_ref, o_ref):
    o_ref[...] = _combine_ln(x1_ref, ya_ref, yb_ref, r_ref, g_ref, b_ref)


def _ln2_final_kernel(x1_ref, ya_ref, yb_ref, r_ref, g_ref, b_ref, yp_ref, ys_ref):
    i = pl.program_id(0)
    y = _combine_ln(x1_ref, ya_ref, yb_ref, r_ref, g_ref, b_ref)

    @pl.when(i < N_PTILES)
    def _():
        yp_ref[...] = y

    @pl.when(i == N_PTILES)
    def _():
        ys_ref[...] = y[0:DEC_BATCH, :]


def _ln2(x1, ya, yb, rinfo, g, b, final):
    tile = pl.BlockSpec((TM, D_MODEL), lambda i: (i, 0))
    rtile = pl.BlockSpec((TM, LANES), lambda i: (i, 0))
    vec = pl.BlockSpec((1, D_MODEL), lambda i: (0, 0))
    if final:
        out_specs = [pl.BlockSpec((TM, D_MODEL), lambda i: (jnp.minimum(i, N_PTILES - 1), 0)),
                     pl.BlockSpec((DEC_BATCH, D_MODEL), lambda i: (0, 0))]
        out_shape = [jax.ShapeDtypeStruct((N_PROMPT, D_MODEL), F32), jax.ShapeDtypeStruct((DEC_BATCH, D_MODEL), F32)]
    else:
        out_specs, out_shape = tile, jax.ShapeDtypeStruct((N_TOK, D_MODEL), F32)
    return pl.pallas_call(
        _ln2_final_kernel if final else _ln2_kernel,
        grid=(N_TILES,),
        in_specs=[tile, tile, tile, rtile, vec, vec],
        out_specs=out_specs,
        out_shape=out_shape,
        compiler_params=_cparams("arbitrary"),
        name="ln2",
    )(x1, ya, yb, rinfo, g, b)


def _moe(x1, rinfo, totals, layer, wg, wu, wd, g2, b2, final):
    sorted_tok, pos_of_pair, meta = _dispatch(rinfo, totals)
    ys = _experts(meta, x1[sorted_tok], layer, wg, wu, wd)
    return _ln2(x1, ys[pos_of_pair[:, 0]], ys[pos_of_pair[:, 1]], rinfo, g2, b2, final)


LOG2E = math.log2(math.e)
QSCALE = SCALE * LOG2E
AUG_ONE = HEAD_DIM
AUG_CK = HEAD_DIM + 3
L2_ONE = 48
C_GROUP = C_HEADS // C_KV


def _aug_constants():
    perm_k = np.zeros((256 + LANES, C_KV * LANES), np.float32)
    for kv in range(C_KV):
        for d in range(HEAD_DIM):
            perm_k[kv * HEAD_DIM + d, kv * LANES + d] = 1.0
        for p in range(3):
            perm_k[256 + L2_ONE, kv * LANES + AUG_ONE + p] = 1.0
            for g in range(C_GROUP):
                perm_k[256 + 16 * p + kv * C_GROUP + g, kv * LANES + AUG_CK + 4 * p + g] = 1.0
    perm_q = np.zeros((C_HEADS * HEAD_DIM, HEAD_DIM), np.float32)
    for h in range(C_HEADS):
        for p in range(3):
            perm_q[h * HEAD_DIM + p, 16 * p + h] = 1.0
            perm_q[h * HEAD_DIM + 3 + 4 * p + h % C_GROUP, L2_ONE] = -1.0
    return jnp.asarray(perm_k, BF16), jnp.asarray(perm_q, BF16)


def _split3(v):
    hi = v.astype(BF16).astype(F32)
    mid = (v - hi).astype(BF16).astype(F32)
    lo = (v - hi - mid).astype(BF16).astype(F32)
    return hi, mid, lo


def _split_layout(v, lane):
    hi, mid, lo = _split3(jnp.where(lane < C_HEADS, v, 0.0))
    return hi + pltpu.roll(mid, 16, 1) + pltpu.roll(lo, 32, 1)


def _odd_in_kernel(x_ref, wt_ref, wkf_ref, bf_ref, permk_ref, permq_ref,
                   qa_ref, ka_ref, ktp_ref, vtp_ref, lftp_ref, qts_ref, kts_ref, vts_ref, lfts_ref,
                   carry_ref, r_ref):
    i = pl.program_id(0)
    xb = x_ref[...].astype(BF16)
    nt_dims = (((1,), (1,)), ((), ()))
    hr = jnp.dot(xb, wkf_ref[...], preferred_element_type=F32)
    lane = lax.broadcasted_iota(jnp.int32, (TM, LANES), 1)
    z = hr[:, 256:384] + bf_ref[...]
    lf = -(jnp.maximum(-z, 0.0) + jnp.log1p(jnp.exp(-jnp.abs(z))))
    lf = jnp.where(lane < C_HEADS, lf, 0.0)
    lft = lf.T[0:C_HEADS, :]

    @pl.when(i < N_PTILES)
    def _():
        @pl.when(i % TILES_PER_SEQ == 0)
        def _():
            carry_ref[...] = jnp.zeros_like(carry_ref)
        ht = lax.dot_general(wt_ref[...], xb, nt_dims, preferred_element_type=F32)
        row = lax.broadcasted_iota(jnp.int32, (TM, TM), 0)
        col = lax.broadcasted_iota(jnp.int32, (TM, TM), 1)
        tri = jnp.where(row >= col, 1.0, 0.0).astype(BF16)
        r_ref[...] = jnp.dot(tri, _split_layout(lf, lane).astype(BF16), preferred_element_type=F32)
        r = r_ref[...]
        c = r + pltpu.roll(r, LANES - 16, 1) + pltpu.roll(r, LANES - 32, 1)
        c = jnp.where(lane < C_HEADS, c, 0.0) + carry_ref[0:1, :]
        carry_ref[0:1, :] = c[TM - 1:TM, :]
        l2 = _split_layout(c * LOG2E, lane) + jnp.where(lane == L2_ONE, 1.0, 0.0)
        kin = jnp.concatenate([hr[:, 0:256], l2], axis=1).astype(BF16)
        ka_ref[...] = jnp.dot(kin, permk_ref[...], preferred_element_type=F32).astype(BF16)
        l2t = l2.T[0:HEAD_DIM, :].astype(BF16)
        qextra = jnp.dot(permq_ref[...], l2t, preferred_element_type=F32)
        for h in range(C_HEADS):
            rows = slice(h * HEAD_DIM, (h + 1) * HEAD_DIM)
            qa_ref[h, 0:HEAD_DIM, :] = (ht[rows, :] * QSCALE).astype(BF16)
            qa_ref[h, HEAD_DIM:2 * HEAD_DIM, :] = qextra[rows, :].astype(BF16)
        ktp_ref[...] = ht[1024:1280, :]
        vtp_ref[...] = ht[1280:1536, :]
        lftp_ref[...] = lft

    @pl.when(i == N_PTILES)
    def _():
        ht = lax.dot_general(wt_ref[...], xb[0:DEC_BATCH, :], nt_dims, preferred_element_type=F32)
        qts_ref[...] = ht[0:1024, :]
        kts_ref[...] = ht[1024:1280, :]
        vts_ref[...] = ht[1280:1536, :]
        lfts_ref[...] = lft[:, 0:DEC_BATCH]


def _odd_in(x, wt_bf, wkf_bf, bf, perm_k, perm_q):
    full = lambda shape: pl.BlockSpec(shape, lambda i: (0,) * len(shape))
    bidx = lambda i: jnp.minimum(i // TILES_PER_SEQ, BATCH - 1)
    tidx = lambda i: jnp.where(i < N_PTILES, i % TILES_PER_SEQ, TILES_PER_SEQ - 1)
    tpose = lambda rows: pl.BlockSpec((None, rows, TM), lambda i: (bidx(i), 0, tidx(i)))
    return pl.pallas_call(
        _odd_in_kernel,
        grid=(N_TILES,),
        in_specs=[pl.BlockSpec((TM, D_MODEL), lambda i: (i, 0)), full((1536, D_MODEL)), full((D_MODEL, 384)),
                  full((1, LANES)), full((256 + LANES, C_KV * LANES)), full((C_HEADS * HEAD_DIM, HEAD_DIM))],
        out_specs=[pl.BlockSpec((None, C_HEADS, 2 * HEAD_DIM, TM), lambda i: (bidx(i), 0, 0, tidx(i))),
                   pl.BlockSpec((TM, C_KV * LANES), lambda i: (jnp.minimum(i, N_PTILES - 1), 0)),
                   tpose(256), tpose(256), tpose(C_HEADS),
                   full((1024, DEC_BATCH)), full((256, DEC_BATCH)), full((256, DEC_BATCH)),
                   full((C_HEADS, DEC_BATCH))],
        out_shape=[jax.ShapeDtypeStruct((BATCH, C_HEADS, 2 * HEAD_DIM, SEQ), BF16),
                   jax.ShapeDtypeStruct((N_PROMPT, C_KV * LANES), BF16),
                   jax.ShapeDtypeStruct((BATCH, 256, SEQ), F32),
                   jax.ShapeDtypeStruct((BATCH, 256, SEQ), F32),
                   jax.ShapeDtypeStruct((BATCH, C_HEADS, SEQ), F32),
                   jax.ShapeDtypeStruct((1024, DEC_BATCH), F32),
                   jax.ShapeDtypeStruct((256, DEC_BATCH), F32),
                   jax.ShapeDtypeStruct((256, DEC_BATCH), F32),
                   jax.ShapeDtypeStruct((C_HEADS, DEC_BATCH), F32)],
        scratch_shapes=[pltpu.VMEM((8, LANES), F32), pltpu.VMEM((TM, LANES), F32)],
        compiler_params=_cparams("arbitrary"),
        name="odd_in",
    )(x, wt_bf, wkf_bf, bf, perm_k, perm_q)


FT = 256
ACC_ROWS = HEAD_DIM + 16


def _fox_prompt_kernel(qa_ref, ka_ref, vt_ref, o_ref, *scratch):
    acc_refs = scratch[0:C_GROUP]
    st_refs = scratch[C_GROUP:2 * C_GROUP]
    srow = lax.broadcasted_iota(jnp.int32, (FT, FT), 0)
    tcol = lax.broadcasted_iota(jnp.int32, (FT, FT), 1)

    def q_tile(qi, _):
        t0 = pl.multiple_of(qi * FT, FT)
        qts = [qa_ref[g, :, pl.ds(t0, FT)] for g in range(C_GROUP)]
        for acc_ref in acc_refs:
            acc_ref[...] = jnp.zeros_like(acc_ref)

        def chunk(s0, nk, ms, masked):
            ka = ka_ref[pl.ds(s0, nk), :]
            vt = jnp.concatenate([vt_ref[:, pl.ds(s0, nk)], jnp.ones((ACC_ROWS - HEAD_DIM, nk), F32)],
                                 axis=0).astype(BF16)
            bmax = []
            for g in range(C_GROUP):
                st = jnp.dot(ka, qts[g], preferred_element_type=F32)
                if masked:
                    diag = jnp.where(srow <= tcol, st[nk - FT:nk, :], NEG_INF)
                    st = diag if nk == FT else jnp.concatenate([st[0:nk - FT, :], diag], axis=0)
                st_refs[g][0:nk, :] = st
                bmax.append(jnp.max(st, axis=0, keepdims=True))
            out = []
            for g in range(C_GROUP):
                m_new = jnp.maximum(ms[g], bmax[g])
                a = jnp.exp2(ms[g] - m_new)
                p = jnp.exp2(st_refs[g][0:nk, :] - m_new)
                acc_refs[g][...] = a * acc_refs[g][...] + jnp.dot(vt, p.astype(BF16), preferred_element_type=F32)
                out.append(m_new)
            return tuple(out)

        init = (jnp.full((1, FT), NEG_INF, F32),) * C_GROUP
        n4 = qi // 4
        ms = lax.fori_loop(0, n4, lambda j, c: chunk(pl.multiple_of(j * 4 * FT, 4 * FT), 4 * FT, c, False), init)
        s_tail = pl.multiple_of(n4 * 4 * FT, 4 * FT)

        def tail(nb):
            def run(ms_in):
                chunk(s_tail, nb * FT, ms_in, True)
                return 0
            return run

        lax.switch(qi % 4, [tail(nb) for nb in (1, 2, 3, 4)], ms)
        heads = []
        for g in range(C_GROUP):
            acc = acc_refs[g][...]
            heads.append(acc[0:HEAD_DIM, :] * (1.0 / acc[HEAD_DIM:HEAD_DIM + 1, :]))
        o_ref[pl.ds(t0, FT), :] = jnp.concatenate(heads, axis=0).T
        return 0

    lax.fori_loop(0, SEQ // FT, q_tile, 0)


def _fox_prompt(qa, ka, vtp):
    return pl.pallas_call(
        _fox_prompt_kernel,
        grid=(BATCH, C_KV),
        in_specs=[pl.BlockSpec((None, C_GROUP, 2 * HEAD_DIM, SEQ), lambda b, kv: (b, kv, 0, 0)),
                  pl.BlockSpec((SEQ, LANES), lambda b, kv: (b, kv)),
                  pl.BlockSpec((None, HEAD_DIM, SEQ), lambda b, kv: (b, kv, 0))],
        out_specs=pl.BlockSpec((SEQ, C_GROUP * HEAD_DIM), lambda b, kv: (b, kv)),
        out_shape=jax.ShapeDtypeStruct((N_PROMPT, C_HEADS * HEAD_DIM), F32),
        scratch_shapes=[pltpu.VMEM((ACC_ROWS, FT), F32)] * C_GROUP + [pltpu.VMEM((4 * FT, FT), F32)] * C_GROUP,
        compiler_params=_cparams("arbitrary", "arbitrary"),
        name="fox_prompt",
    )(qa, ka, vtp)


PG = 16
KVD = C_KV * HEAD_DIM


def _fox_sample_kernel(layer, pt_ref, qbd_ref, kts_ref, vts_ref, lfts_ref, kc_hbm, vc_hbm, lfc_hbm,
                       o_ref, kbuf, vbuf, lfbuf, s_ref, off_ref, sem):
    r = pl.program_id(0)
    slot = r % 2
    caches = ((kc_hbm, kbuf), (vc_hbm, vbuf), (lfc_hbm, lfbuf))

    def start_fetch(req, to_slot):
        def body(p, _):
            page = pt_ref[req * N_PAGES + p]
            for t, (src, dst) in enumerate(caches):
                pltpu.make_async_copy(src.at[layer, page], dst.at[to_slot, p], sem.at[t, to_slot]).start()
            return 0
        lax.fori_loop(0, N_PAGES, body, 0, unroll=4)

    @pl.when(r == 0)
    def _():
        start_fetch(0, 0)

    @pl.when(r + 1 < DEC_BATCH)
    def _():
        start_fetch(r + 1, 1 - slot)

    for t, (src, dst) in enumerate(caches):
        pltpu.make_async_copy(src.at[layer, pl.ds(0, N_PAGES)], dst.at[slot], sem.at[t, slot]).wait()

    qbd = qbd_ref[...].astype(BF16)
    rr = lax.broadcasted_iota(jnp.int32, (PAGE, PAGE), 0)
    cc = lax.broadcasted_iota(jnp.int32, (PAGE, PAGE), 1)
    upper = jnp.where(rr <= cc, 1.0, 0.0).astype(BF16)
    n = PG * C_HEADS
    rows = N_PAGES * C_HEADS

    tot = jnp.sum(lfbuf[slot].reshape(rows, PAGE), axis=-1, keepdims=True)
    incl = tot
    shift = C_HEADS
    while shift < rows:
        incl = incl + jnp.concatenate([jnp.zeros((shift, 1), F32), incl[0:rows - shift, :]], axis=0)
        shift *= 2
    off_ref[...] = jnp.broadcast_to(incl - tot, (rows, PAGE))
    total = incl[rows - C_HEADS:rows, :]

    def score_group(j, m):
        p0 = pl.multiple_of(j * PG, PG)
        lf_all = lfbuf[slot, pl.ds(p0, PG)].reshape(n, PAGE)
        parts = jnp.concatenate(_split3(lf_all), axis=0).astype(BF16)
        c3 = jnp.dot(parts, upper, preferred_element_type=F32)
        c_all = c3[0:n, :] + c3[n:2 * n, :] + c3[2 * n:3 * n, :] + off_ref[pl.ds(pl.multiple_of(j * n, n), n), :]
        s_parts = []
        for g in range(PG):
            kt = kbuf[slot, p0 + g].astype(BF16)
            s_parts.append(jnp.dot(qbd, kt, preferred_element_type=F32) - c_all[g * C_HEADS:(g + 1) * C_HEADS, :])
        s = jnp.concatenate(s_parts, axis=1)
        s_ref[:, pl.ds(pl.multiple_of(j * PG * PAGE, PG * PAGE), PG * PAGE)] = s
        return jnp.maximum(m, jnp.max(s, axis=-1, keepdims=True))

    m = lax.fori_loop(0, N_PAGES // PG, score_group, jnp.full((C_HEADS, 1), NEG_INF, F32))
    lane = lax.broadcasted_iota(jnp.int32, (C_HEADS, DEC_BATCH), 1)
    s_new = jnp.dot(qbd, kts_ref[...].astype(BF16), preferred_element_type=F32) - (total + lfts_ref[...])
    s_new = jnp.where(lane == r, s_new, NEG_INF)
    m = jnp.maximum(m, jnp.max(s_new, axis=-1, keepdims=True))

    def value_group(j, c):
        l, acc_t = c
        p0 = pl.multiple_of(j * PG, PG)
        p = jnp.exp(s_ref[:, pl.ds(pl.multiple_of(j * PG * PAGE, PG * PAGE), PG * PAGE)] - m)
        vt = jnp.concatenate([vbuf[slot, p0 + g] for g in range(PG)], axis=1).astype(BF16)
        acc_t = acc_t + jnp.dot(vt, p.T.astype(BF16), preferred_element_type=F32)
        return l + jnp.sum(p, axis=-1, keepdims=True), acc_t

    p_new = jnp.exp(s_new - m)
    init = (jnp.sum(p_new, axis=-1, keepdims=True),
            jnp.dot(vts_ref[...].astype(BF16), p_new.T.astype(BF16), preferred_element_type=F32))
    l, acc_t = lax.fori_loop(0, N_PAGES // PG, value_group, init)
    o_ref[...] = acc_t.T / l


def _fox_sample(layer, page_flat, qbd, kts, vts, lfts, kc, vc, lfc):
    const = lambda shape: pl.BlockSpec(shape, lambda r, pt: (0,) * len(shape))
    hbm = pl.BlockSpec(memory_space=pl.ANY)
    grid_spec = pltpu.PrefetchScalarGridSpec(
        num_scalar_prefetch=1,
        grid=(DEC_BATCH,),
        in_specs=[pl.BlockSpec((None, C_HEADS, KVD), lambda r, pt: (r, 0, 0)),
                  const((KVD, DEC_BATCH)), const((KVD, DEC_BATCH)), const((C_HEADS, DEC_BATCH)), hbm, hbm, hbm],
        out_specs=pl.BlockSpec((None, C_HEADS, KVD), lambda r, pt: (r, 0, 0)),
        scratch_shapes=[pltpu.VMEM((2, N_PAGES, KVD, PAGE), F32), pltpu.VMEM((2, N_PAGES, KVD, PAGE), F32),
                        pltpu.VMEM((2, N_PAGES, C_HEADS, PAGE), F32),
                        pltpu.VMEM((C_HEADS, (N_PAGES + 1) * PAGE), F32),
                        pltpu.VMEM((N_PAGES * C_HEADS, PAGE), F32),
                        pltpu.SemaphoreType.DMA((3, 2))],
    )
    return pl.pallas_call(
        functools.partial(_fox_sample_kernel, layer),
        grid_spec=grid_spec,
        out_shape=jax.ShapeDtypeStruct((DEC_BATCH, C_HEADS, KVD), F32),
        compiler_params=pltpu.CompilerParams(dimension_semantics=("arbitrary",), vmem_limit_bytes=VMEM_LIMIT,
                                             disable_bounds_checks=True),
        name="fox_sample",
    )(page_flat, qbd, kts, vts, lfts, kc, vc, lfc)


def _t5_bucket(dist):
    n = jnp.maximum(dist, 0)
    max_exact = N_BUCKETS // 2
    nf = jnp.maximum(n, 1).astype(F32)
    large = max_exact + (jnp.log(nf / max_exact) / math.log(MAX_DISTANCE / max_exact)
                         * (N_BUCKETS - max_exact)).astype(jnp.int32)
    large = jnp.minimum(large, N_BUCKETS - 1)
    return jnp.where(n < max_exact, n, large)


def kernel(x_prompt, x_sample, cache_swa_k, cache_swa_v, state_conv, cache_fox_k, cache_fox_v, cache_fox_logf, page_table, rel_bias_table, attn_sinks, w_in_even, conv_w, w_out_even, w_in_odd, b_forget, w_out_odd, ln_g, ln_b, w_group, b_group, w_router, b_router, w_gate, w_up, w_down):
    x = jnp.concatenate([x_prompt.reshape(N_PROMPT, D_MODEL), x_sample.reshape(DEC_BATCH, D_MODEL),
                         jnp.zeros((N_TOK - N_PROMPT - DEC_BATCH, D_MODEL), F32)], axis=0)

    qi = jnp.arange(WINDOW)[:, None]
    kj = jnp.arange(2 * WINDOW)[None, :]
    dist = WINDOW + qi - kj
    band = (dist >= 0) & (dist < WINDOW)
    onehot = (_t5_bucket(dist)[:, :, None] == jnp.arange(N_BUCKETS)[None, None, :]).astype(F32)
    bias_t = jnp.einsum('qkb,bh->hkq', onehot, rel_bias_table, precision=lax.Precision.HIGHEST)
    bias_p = jnp.stack([jnp.where((band & (kj >= WINDOW)).T[None], bias_t, NEG_INF),
                        jnp.where(band.T[None], bias_t, NEG_INF)])
    bias_s = rel_bias_table[_t5_bucket(WINDOW - jnp.arange(WINDOW))].T
    bnew_b = jnp.broadcast_to(rel_bias_table[0][:, None], (A_HEADS, LANES))

    swa_kt = jnp.transpose(cache_swa_k, (0, 1, 3, 4, 2))
    swa_vt = jnp.transpose(cache_swa_v, (0, 1, 3, 4, 2))
    pool = cache_fox_k.shape[1]
    fox_kt = jnp.transpose(cache_fox_k, (0, 1, 3, 4, 2)).reshape(DEPTH // 2, pool, KVD, PAGE)
    fox_vt = jnp.transpose(cache_fox_v, (0, 1, 3, 4, 2)).reshape(DEPTH // 2, pool, KVD, PAGE)
    fox_lft = jnp.transpose(cache_fox_logf, (0, 1, 3, 2))
    page_flat = page_table.reshape(-1)

    wr = jnp.concatenate([w_router, w_group], axis=-1)
    wr_hi = wr.astype(BF16)
    wr_lo = (wr - wr_hi.astype(F32)).astype(BF16)
    zpad = lambda n: jnp.zeros((DEPTH, D_MODEL, n), BF16)
    wr_all = jnp.concatenate([wr_hi, zpad(12), wr_lo, zpad(LANES - 52)], axis=-1)
    br_all = jnp.concatenate([b_router, b_group, jnp.zeros((DEPTH, LANES - 20), F32)], axis=-1)
    eye = jnp.eye(C_KV, dtype=F32)
    perm_k, perm_q = _aug_constants()

    outs = {n: [] for n in ("swa_kp", "swa_vp", "swa_ks", "swa_vs", "conv_p", "conv_s",
                            "fox_kp", "fox_vp", "fox_lp", "fox_ks", "fox_vs", "fox_ls")}
    for layer in range(DEPTH):
        if layer % 2 == 0:
            e = layer // 2
            s0 = state_conv[e, :, 0, :]
            s1 = state_conv[e, :, 1, :]
            q, k, v, cv, convp, us, klt, vlt = _even_in(x, w_in_even[e].astype(BF16), conv_w[e], s0, s1)
            sink_b = jnp.broadcast_to(attn_sinks[e][:, None], (A_HEADS, LANES))
            att_p = _swa_prompt(q, k, v, bias_p, sink_b)
            att_s = _swa_sample(q, k, v, swa_kt[e], swa_vt[e], bias_s, bnew_b, sink_b)
            conv = cv
            w_out = w_out_even[e]
            ks =k[N_PROMPT:N_PROMPT + DEC_BATCH].reshape(DEC_BATCH, 1, A_KV, HEAD_DIM)
            vs = v[N_PROMPT:N_PROMPT + DEC_BATCH].reshape(DEC_BATCH, 1, A_KV, HEAD_DIM)
            last = lambda t: jnp.transpose(t.reshape(BATCH, A_KV, HEAD_DIM, WINDOW), (0, 3, 1, 2))
            outs["swa_kp"].append(last(klt))
            outs["swa_vp"].append(last(vlt))
            outs["swa_ks"].append(jnp.concatenate([cache_swa_k[e][:, 1:], ks], axis=1))
            outs["swa_vs"].append(jnp.concatenate([cache_swa_v[e][:, 1:], vs], axis=1))
            outs["conv_p"].append(convp)
            outs["conv_s"].append(jnp.stack([s1, us], axis=1))
        else:
            o = layer // 2
            wt_bf = jnp.transpose(w_in_odd[o][:, 0:1536]).astype(BF16)
            wkf_bf = jnp.concatenate([w_in_odd[o][:, 1024:1280], w_in_odd[o][:, 1536:1552],
                                      jnp.zeros((D_MODEL, 384 - 272), F32)], axis=-1).astype(BF16)
            bf_pad = jnp.concatenate([b_forget[o], jnp.zeros((LANES - C_HEADS,), F32)])[None, :]
            qa, ka, ktp, vtp, lftp, qts, kts, vts, lfts = _odd_in(x, wt_bf, wkf_bf, bf_pad, perm_k, perm_q)
            att_p = _fox_prompt(qa, ka, vtp)
            qs = qts.T.reshape(DEC_BATCH, C_KV, C_GROUP, 1, HEAD_DIM) * SCALE
            qbd = (qs * eye[None, :, None, :, None]).reshape(DEC_BATCH, C_HEADS, KVD)
            of = _fox_sample(o, page_flat, qbd, kts, vts, lfts, fox_kt, fox_vt, fox_lft)
            of = of.reshape(DEC_BATCH, C_KV, C_GROUP, C_KV, HEAD_DIM)
            att_s = jnp.sum(of * eye[None, :, None, :, None], axis=3).reshape(DEC_BATCH, C_HEADS * HEAD_DIM)
            att_s = jnp.concatenate([att_s, jnp.zeros((TM - DEC_BATCH, C_HEADS * HEAD_DIM), F32)], axis=0)
            conv = None
            w_out = w_out_odd[o]
            outs["fox_kp"].append(jnp.transpose(ktp.reshape(BATCH, C_KV, HEAD_DIM, SEQ), (0, 3, 1, 2)))
            outs["fox_vp"].append(jnp.transpose(vtp.reshape(BATCH, C_KV, HEAD_DIM, SEQ), (0, 3, 1, 2)))
            outs["fox_lp"].append(jnp.transpose(lftp, (0, 2, 1)))
            outs["fox_ks"].append(jnp.transpose(kts.reshape(C_KV, HEAD_DIM, DEC_BATCH), (2, 0, 1))[:, None])
            outs["fox_vs"].append(jnp.transpose(vts.reshape(C_KV, HEAD_DIM, DEC_BATCH), (2, 0, 1))[:, None])
            outs["fox_ls"].append(lfts.T[:, None, :])
        x1, rinfo, totals = _post_mix(att_p, att_s, conv, x, w_out.astype(BF16), ln_g[layer, 0][None, :], ln_b[layer, 0][None, :],
                                      wr_all[layer], br_all[layer][None, :])
        x = _moe(x1, rinfo, totals, layer, w_gate, w_up, w_down,
                 ln_g[layer, 1][None, :], ln_b[layer, 1][None, :], layer == DEPTH - 1)

    st = {n: jnp.stack(vl) for n, vl in outs.items()}
    y_prompt = x[0].reshape(BATCH, SEQ, D_MODEL)
    y_sample = x[1].reshape(DEC_BATCH, 1, D_MODEL)
    return (y_prompt, y_sample, st["swa_kp"], st["swa_vp"], st["swa_ks"], st["swa_vs"], st["conv_p"], st["conv_s"],
            st["fox_kp"], st["fox_vp"], st["fox_lp"], st["fox_ks"], st["fox_vs"], st["fox_ls"])
```

```python
import functools
import math

import jax
import jax.numpy as jnp
import numpy as np
from jax import lax
from jax.experimental import pallas as pl
from jax.experimental.pallas import tpu as pltpu

F32 = jnp.float32
BF16 = jnp.bfloat16

D_MODEL = 1024
BATCH = 4
SEQ = 4096
DEC_BATCH = 128
PAGE = 128
N_PAGES = 64
HEAD_DIM = 64
A_HEADS = 8
A_KV = 2
WINDOW = 128
B_WIDTH = 512
C_HEADS = 16
C_KV = 4
N_BUCKETS = 32
MAX_DISTANCE = 128
N_GROUPS = 4
N_EXPERTS = 16
D_EXPERT = 256
DEPTH = 4
ALPHA = (2.0 * DEPTH) ** 0.25
LN_EPS = 1e-5
NEG_INF = -1e30
SCALE = HEAD_DIM ** -0.5

N_PROMPT = BATCH * SEQ
TM = 512
N_PTILES = N_PROMPT // TM
TILES_PER_SEQ = SEQ // TM
N_TILES = N_PTILES + 1
N_TOK = N_TILES * TM
LANES = 128

TE = 512
TE_SUB = 256
N_SLOTS = 2 * N_TOK
N_VISITS = N_SLOTS // TE + N_EXPERTS - 1

VMEM_LIMIT = 56 * 1024 * 1024


def _cparams(*sem):
    return pltpu.CompilerParams(dimension_semantics=sem, vmem_limit_bytes=VMEM_LIMIT)


def _layer_norm(y, g, b):
    mu = jnp.mean(y, axis=-1, keepdims=True)
    yc = y - mu
    var = jnp.mean(yc * yc, axis=-1, keepdims=True)
    return yc * lax.rsqrt(var + LN_EPS) * g + b


_TOKEN_SPECS = [pl.BlockSpec((TM, D_MODEL), lambda i: (jnp.minimum(i, N_PTILES - 1), 0)),
                pl.BlockSpec((TM, D_MODEL), lambda i: (0, 0))]


def _token_tile(i, xp_ref, xs_ref):
    return jnp.where(i < N_PTILES, xp_ref[...], xs_ref[...])


def _even_in_kernel(xp_ref, xs_ref, w_ref, cw_ref, s0_ref, s1_ref,
                    q_ref, k_ref, v_ref, cv_ref, convp_ref, us_ref, klt_ref, vlt_ref, carry_ref):
    i = pl.program_id(0)
    h = jnp.dot(_token_tile(i, xp_ref, xs_ref).astype(BF16), w_ref[...], preferred_element_type=F32)
    q_ref[...] = h[:, 0:512]
    k_ref[...] = h[:, 512:640]
    v_ref[...] = h[:, 640:768]
    bg = h[:, 768:1280]
    u = h[:, 1280:1792] * h[:, 1792:2304]
    w0 = cw_ref[0:1, :]
    w1 = cw_ref[1:2, :]
    w2 = cw_ref[2:3, :]

    @pl.when(i < N_PTILES)
    def _():
        @pl.when(i % TILES_PER_SEQ == 0)
        def _():
            carry_ref[...] = jnp.zeros_like(carry_ref)
        row = lax.broadcasted_iota(jnp.int32, (TM, B_WIDTH), 0)
        c2 = carry_ref[0:1, :]
        c1 = carry_ref[1:2, :]
        u1 = jnp.where(row == 0, c1, pltpu.roll(u, 1, 0))
        u2 = jnp.where(row == 0, c2, jnp.where(row == 1, c1, pltpu.roll(u, 2, 0)))
        cv_ref[...] = bg * (w0 * u2 + w1 * u1 + w2 * u)
        carry_ref[0:2, :] = u[TM - 2:TM, :]
        convp_ref[...] = u[TM - 2:TM, :]

        @pl.when(i % TILES_PER_SEQ == TILES_PER_SEQ - 1)
        def _():
            klt_ref[...] = h[TM - WINDOW:TM, 512:640].T
            vlt_ref[...] = h[TM - WINDOW:TM, 640:768].T

    @pl.when(i == N_PTILES)
    def _():
        us = u[0:DEC_BATCH, :]
        z = w0 * s0_ref[...] + w1 * s1_ref[...] + w2 * us
        cv_ref[0:DEC_BATCH, :] = bg[0:DEC_BATCH, :] * z
        cv_ref[DEC_BATCH:TM, :] = jnp.zeros((TM - DEC_BATCH, B_WIDTH), F32)
        us_ref[...] = us


def _even_in(x, w_bf, cw, s0, s1):
    tile = lambda n: pl.BlockSpec((TM, n), lambda i: (i, 0))
    full = lambda shape: pl.BlockSpec(shape, lambda i: (0,) * len(shape))
    return pl.pallas_call(
        _even_in_kernel,
        grid=(N_TILES,),
        in_specs=_TOKEN_SPECS + [full((D_MODEL, 2304)), full((3, B_WIDTH)),
                  full((DEC_BATCH, B_WIDTH)), full((DEC_BATCH, B_WIDTH))],
        out_specs=[tile(512), tile(128), tile(128), tile(B_WIDTH),
                   pl.BlockSpec((None, 2, B_WIDTH),
                                lambda i: (jnp.minimum(i // TILES_PER_SEQ, BATCH - 1), 0, 0)),
                   full((DEC_BATCH, B_WIDTH)),
                   pl.BlockSpec((None, 128, WINDOW), lambda i: (jnp.minimum(i // TILES_PER_SEQ, BATCH - 1), 0, 0)),
                   pl.BlockSpec((None, 128, WINDOW), lambda i: (jnp.minimum(i // TILES_PER_SEQ, BATCH - 1), 0, 0))],
        out_shape=[jax.ShapeDtypeStruct((N_TOK, 512), F32),
                   jax.ShapeDtypeStruct((N_TOK, 128), F32),
                   jax.ShapeDtypeStruct((N_TOK, 128), F32),
                   jax.ShapeDtypeStruct((N_TOK, B_WIDTH), F32),
                   jax.ShapeDtypeStruct((BATCH, 2, B_WIDTH), F32),
                   jax.ShapeDtypeStruct((DEC_BATCH, B_WIDTH), F32),
                   jax.ShapeDtypeStruct((BATCH, 128, WINDOW), F32),
                   jax.ShapeDtypeStruct((BATCH, 128, WINDOW), F32)],
        scratch_shapes=[pltpu.VMEM((8, B_WIDTH), F32)],
        compiler_params=_cparams("arbitrary"),
        name="even_in",
    )(*x, w_bf, cw, s0, s1)


def _swa_prompt_kernel(q_ref, kc_ref, kp_ref, vc_ref, vp_ref, bias_ref, sink_ref, o_ref, *st_refs):
    qt = (q_ref[...] * SCALE).T
    kk = jnp.concatenate([kp_ref[...], kc_ref[...]], axis=0).astype(BF16)
    vvt = jnp.concatenate([vp_ref[...], vc_ref[...]], axis=0).T
    zeros = jnp.zeros((HEAD_DIM, WINDOW), F32)
    ones = jnp.ones((SWA_ACC_ROWS - HEAD_DIM, 2 * WINDOW), F32)
    cmax = []
    for h in range(A_HEADS):
        kv = h // (A_HEADS // A_KV)
        qh = qt[h * HEAD_DIM:(h + 1) * HEAD_DIM, :]
        qh = jnp.concatenate([qh, zeros] if kv == 0 else [zeros, qh], axis=0).astype(BF16)
        st = jnp.dot(kk, qh, preferred_element_type=F32) + bias_ref[h]
        st_refs[h][...] = st
        cmax.append(jnp.max(st, axis=0, keepdims=True))
    outs = []
    for h in range(A_HEADS):
        kv = h // (A_HEADS // A_KV)
        sk = sink_ref[h:h + 1, :]
        m = jnp.maximum(cmax[h], sk)
        p = jnp.exp(st_refs[h][...] - m)
        vt = jnp.concatenate([vvt[kv * HEAD_DIM:(kv + 1) * HEAD_DIM, :], ones], axis=0).astype(BF16)
        acc = jnp.dot(vt, p.astype(BF16), preferred_element_type=F32)
        den = acc[HEAD_DIM:HEAD_DIM + 1, :] + jnp.exp(sk - m)
        outs.append(acc[0:HEAD_DIM, :] / den)
    o_ref[...] = jnp.concatenate(outs, axis=0).T


def _swa_prompt(q, k, v, bias, sink_b):
    nblk = SEQ // WINDOW
    cur = lambda b, j: (b * nblk + j, 0)
    prev = lambda b, j: (jnp.maximum(b * nblk + j - 1, 0), 0)
    return pl.pallas_call(
        _swa_prompt_kernel,
        grid=(BATCH, nblk),
        in_specs=[pl.BlockSpec((WINDOW, 512), cur),
                  pl.BlockSpec((WINDOW, 128), cur), pl.BlockSpec((WINDOW, 128), prev),
                  pl.BlockSpec((WINDOW, 128), cur), pl.BlockSpec((WINDOW, 128), prev),
                  pl.BlockSpec((None, A_HEADS, 2 * WINDOW, WINDOW), lambda b, j: (jnp.minimum(j, 1), 0, 0, 0)),
                  pl.BlockSpec((A_HEADS, LANES), lambda b, j: (0, 0))],
        out_specs=pl.BlockSpec((WINDOW, 512), cur),
        out_shape=jax.ShapeDtypeStruct((N_PROMPT, 512), F32),
        scratch_shapes=[pltpu.VMEM((2 * WINDOW, WINDOW), F32)] * A_HEADS,
        compiler_params=_cparams("arbitrary", "arbitrary"),
        name="swa_prompt",
    )(q, k, k, v, v, bias, sink_b)


SWA_RB = 8
SWA_ACC_ROWS = HEAD_DIM + 16


def _swa_sample_kernel(q_ref, k_ref, v_ref, kt_ref, vt_ref, bias_ref, bnew_ref, sink_ref, o_ref):
    lane = lax.broadcasted_iota(jnp.int32, (4, WINDOW), 1)
    grp = A_HEADS // A_KV
    results = []
    for r in range(SWA_RB):
        for kv in range(A_KV):
            hs = slice(kv * grp, (kv + 1) * grp)
            q4 = jnp.concatenate(
                [q_ref[r:r + 1, (kv * grp + g) * HEAD_DIM:(kv * grp + g + 1) * HEAD_DIM] for g in range(grp)],
                axis=0)
            kt = kt_ref[r, kv]
            vt = vt_ref[r, kv]
            s_old = jnp.dot(q4.astype(BF16), kt.astype(BF16), preferred_element_type=F32)
            s_old = s_old * SCALE + bias_ref[hs, :]
            s_old = jnp.where(lane >= 1, s_old, NEG_INF)
            kn = k_ref[r:r + 1, kv * HEAD_DIM:(kv + 1) * HEAD_DIM]
            vn = v_ref[r:r + 1, kv * HEAD_DIM:(kv + 1) * HEAD_DIM]
            s_new = jnp.sum(q4 * kn, axis=-1, keepdims=True) * SCALE + bnew_ref[hs, 0:1]
            sk = sink_ref[hs, 0:1]
            m = jnp.maximum(jnp.maximum(jnp.max(s_old, axis=-1, keepdims=True), s_new), sk)
            p_old = jnp.exp(s_old - m)
            p_new = jnp.exp(s_new - m)
            den = jnp.sum(p_old, axis=-1, keepdims=True) + p_new + jnp.exp(sk - m)
            o = lax.dot_general(p_old.astype(BF16), vt.astype(BF16), (((1,), (1,)), ((), ())),
                                preferred_element_type=F32)
            results.append((o + p_new * vn) / den)
    for r in range(SWA_RB):
        for kv in range(A_KV):
            o = results[r * A_KV + kv]
            for g in range(grp):
                hh = kv * grp + g
                o_ref[r:r + 1, hh * HEAD_DIM:(hh + 1) * HEAD_DIM] = o[g:g + 1, :]


def _swa_sample(q, k, v, kt, vt, bias_s, bnew_b, sink_b):
    base = N_PROMPT // SWA_RB
    rows = lambda n: pl.BlockSpec((SWA_RB, n), lambda i: (base + i, 0))
    cache = pl.BlockSpec((SWA_RB, A_KV, HEAD_DIM, WINDOW), lambda i: (i, 0, 0, 0))
    small = pl.BlockSpec((A_HEADS, LANES), lambda i: (0, 0))
    att = pl.pallas_call(
        _swa_sample_kernel,
        grid=(DEC_BATCH // SWA_RB,),
        in_specs=[rows(512), rows(128), rows(128), cache, cache, small, small, small],
        out_specs=pl.BlockSpec((SWA_RB, 512), lambda i: (i, 0)),
        out_shape=jax.ShapeDtypeStruct((DEC_BATCH, 512), F32),
        compiler_params=_cparams("arbitrary"),
        name="swa_sample",
    )(q, k, v, kt, vt, bias_s, bnew_b, sink_b)
    return jnp.concatenate([att, jnp.zeros((TM - DEC_BATCH, 512), F32)], axis=0)


def _route(x1, wr_ref, br_ref, below_ref, cnt_ref):
    x_hi = x1.astype(BF16)
    x_lo = (x1 - x_hi.astype(F32)).astype(BF16)
    pa = jnp.dot(x_hi, wr_ref[...], preferred_element_type=F32)
    pb = jnp.dot(x_lo, wr_ref[...], preferred_element_type=F32)
    logits = pa + pltpu.roll(pa, LANES - 32, 1) + pb + br_ref[...]
    lane = lax.broadcasted_iota(jnp.int32, logits.shape, 1)
    lane_f = lane.astype(F32)
    lane_grp = (lane >> 2).astype(F32)
    is_grp = (lane >= N_EXPERTS) & (lane < N_EXPERTS + N_GROUPS)
    big = 1e9
    gl = jnp.where(is_grp, logits, NEG_INF)
    gmax = jnp.max(gl, axis=-1, keepdims=True)
    gidx = jnp.min(jnp.where(is_grp & (logits == gmax), lane_f - N_EXPERTS, big), axis=-1, keepdims=True)
    gsum = jnp.sum(jnp.where(is_grp, jnp.exp(gl - gmax), 0.0), axis=-1, keepdims=True)
    grp_w = 1.0 / gsum
    in_grp = (lane < N_EXPERTS) & (lane_grp == gidx)
    e1 = jnp.where(in_grp, logits, NEG_INF)
    t1 = jnp.max(e1, axis=-1, keepdims=True)
    i1 = jnp.min(jnp.where(in_grp & (logits == t1), lane_f, big), axis=-1, keepdims=True)
    rest = in_grp & (lane_f != i1)
    e2 = jnp.where(rest, logits, NEG_INF)
    t2 = jnp.max(e2, axis=-1, keepdims=True)
    i2 = jnp.min(jnp.where(rest & (logits == t2), lane_f, big), axis=-1, keepdims=True)
    ex = jnp.exp(t2 - t1)
    g1 = grp_w / (1.0 + ex)
    g2 = grp_w * ex / (1.0 + ex)
    sel1 = lane_f == i1
    sel2 = lane_f == i2
    onehot = jnp.where(sel1 | sel2, 1.0, 0.0)
    before = jnp.dot(below_ref[...], onehot.astype(BF16), preferred_element_type=F32) + cnt_ref[0:1, :]
    rank1 = jnp.sum(jnp.where(sel1, before, 0.0), axis=-1, keepdims=True)
    rank2 = jnp.sum(jnp.where(sel2, before, 0.0), axis=-1, keepdims=True)
    cnt_ref[0:1, :] = cnt_ref[0:1, :] + jnp.sum(onehot, axis=0, keepdims=True)
    out = jnp.where(lane == 0, i1, 0.0)
    out = jnp.where(lane == 1, i2, out)
    out = jnp.where(lane == 2, g1, out)
    out = jnp.where(lane == 3, g2, out)
    out = jnp.where(lane == 4, rank1, out)
    out = jnp.where(lane == 5, rank2, out)
    return out


def _post_mix_kernel(has_conv, *refs):
    n_in = 3 if has_conv else 2
    att_p_ref, att_s_ref = refs[0:2]
    xp_ref, xs_ref, w_ref, g_ref, b_ref, wr_ref, br_ref, below_ref, x1_ref, r_ref, tot_ref, cnt_ref = refs[n_in:]
    i = pl.program_id(0)

    @pl.when(i == 0)
    def _():
        cnt_ref[...] = jnp.zeros_like(cnt_ref)
    att = jnp.where(i < N_PTILES, att_p_ref[...], att_s_ref[...]).astype(BF16)
    kw = att.shape[1]
    mix = jnp.dot(att, w_ref[0:kw, :], preferred_element_type=F32)
    if has_conv:
        mix = mix + jnp.dot(refs[2][...].astype(BF16), w_ref[kw:D_MODEL, :], preferred_element_type=F32)
    x1 = _layer_norm(ALPHA * _token_tile(i, xp_ref, xs_ref) + mix, g_ref[...], b_ref[...])
    x1_ref[...] = x1
    r_ref[...] = _route(x1, wr_ref, br_ref, below_ref, cnt_ref)
    tot_ref[...] = cnt_ref[...]


def _post_mix(att_p, att_s, conv, x, w_bf, g, b, wr_bf, br):
    kw = att_p.shape[1]
    tile = lambda n: pl.BlockSpec((TM, n), lambda i: (i, 0))
    full = lambda shape: pl.BlockSpec(shape, lambda i: (0,) * len(shape))
    mix_specs = [pl.BlockSpec((TM, kw), lambda i: (jnp.minimum(i, N_PTILES - 1), 0)), full((TM, kw))]
    mixes = [att_p, att_s]
    if conv is not None:
        mix_specs.append(tile(D_MODEL - kw))
        mixes.append(conv)
    return pl.pallas_call(
        functools.partial(_post_mix_kernel, conv is not None),
        grid=(N_TILES,),
        in_specs=mix_specs + _TOKEN_SPECS + [full((D_MODEL, D_MODEL)), full((1, D_MODEL)),
                              full((1, D_MODEL)), full((D_MODEL, LANES)), full((1, LANES)), full((TM, TM))],
        out_specs=[tile(D_MODEL), tile(LANES), full((8, LANES))],
        out_shape=[jax.ShapeDtypeStruct((N_TOK, D_MODEL), F32),
                   jax.ShapeDtypeStruct((N_TOK, LANES), F32),
                   jax.ShapeDtypeStruct((8, LANES), F32)],
        scratch_shapes=[pltpu.VMEM((8, LANES), F32)],
        compiler_params=_cparams("arbitrary"),
        name="post_mix",
    )(*mixes, *x, w_bf, g, b, wr_bf, br, jnp.asarray(np.tril(np.ones((TM, TM), np.float32), -1), BF16))


def _expert_kernel(vt_ref, ve_ref, lo_ref, hi_ref, first_ref, newexp_ref, xs_ref, wg_ref, wu_ref, wd_ref, ys_ref,
                   wgub_ref, wdb_ref):
    del vt_ref, ve_ref
    v = pl.program_id(0)
    lo = lo_ref[v]
    hi = hi_ref[v]

    @pl.when((hi > lo) & (newexp_ref[v] == 1))
    def _():
        wgub_ref[:, 0:D_EXPERT] = wg_ref[...].astype(BF16)
        wgub_ref[:, D_EXPERT:2 * D_EXPERT] = wu_ref[...].astype(BF16)
        wdb_ref[...] = wd_ref[...].astype(BF16)

    @pl.when(hi > lo)
    def _():
        row = lax.broadcasted_iota(jnp.int32, (TE, 1), 0)
        mine = jnp.where((row >= lo) & (row < hi), 1.0, 0.0)
        wgu = wgub_ref[...]
        wd = wdb_ref[...]
        parts = []
        for s in range(TE // TE_SUB):
            rows = slice(s * TE_SUB, (s + 1) * TE_SUB)
            gu = jnp.dot(xs_ref[rows, :].astype(BF16), wgu, preferred_element_type=F32)
            g = gu[:, 0:D_EXPERT]
            u = gu[:, D_EXPERT:2 * D_EXPERT]
            h = g * (1.0 / (1.0 + jnp.exp(-g))) * u * mine[rows, :]
            parts.append(jnp.dot(h.astype(BF16), wd, preferred_element_type=F32))
        y = jnp.concatenate(parts, axis=0)

        @pl.when(first_ref[v] == 1)
        def _():
            ys_ref[...] = y

        @pl.when(first_ref[v] == 0)
        def _():
            ys_ref[...] += y


def _experts(meta, xs, layer, wg, wu, wd):
    tile_map = lambda v, vt, ve, lo, hi, fi, ne: (vt[v], 0)
    w_map = lambda v, vt, ve, lo, hi, fi, ne: (layer, ve[v], 0, 0)
    grid_spec = pltpu.PrefetchScalarGridSpec(
        num_scalar_prefetch=6,
        grid=(N_VISITS,),
        in_specs=[pl.BlockSpec((TE, D_MODEL), tile_map),
                  pl.BlockSpec((None, None, D_MODEL, D_EXPERT), w_map),
                  pl.BlockSpec((None, None, D_MODEL, D_EXPERT), w_map),
                  pl.BlockSpec((None, None, D_EXPERT, D_MODEL), w_map)],
        out_specs=pl.BlockSpec((TE, D_MODEL), tile_map),
        scratch_shapes=[pltpu.VMEM((D_MODEL, 2 * D_EXPERT), BF16), pltpu.VMEM((D_EXPERT, D_MODEL), BF16)],
    )
    return pl.pallas_call(
        _expert_kernel,
        grid_spec=grid_spec,
        out_shape=jax.ShapeDtypeStruct((N_SLOTS, D_MODEL), F32),
        compiler_params=_cparams("arbitrary"),
        name="experts",
    )(*meta, xs, wg, wu, wd)


def _dispatch(rinfo, totals):
    ids = rinfo[:, 0:2].astype(jnp.int32)
    ranks = rinfo[:, 4:6].astype(jnp.int32)
    counts = totals[0, 0:N_EXPERTS].astype(jnp.int32)
    e_iota = jnp.arange(N_EXPERTS, dtype=jnp.int32)
    incl = (e_iota[None, :] <= e_iota[:, None]).astype(jnp.int32)
    end = jnp.sum(incl * counts[None, :], axis=1)
    off = end - counts
    pos_of_pair = jnp.sum(jnp.where(e_iota[None, None, :] < ids[:, :, None], counts[None, None, :], 0), axis=2) + ranks
    tok = lax.broadcasted_iota(jnp.int32, (N_TOK, 2), 0)
    _, sorted_tok = lax.sort((ids.reshape(-1), tok.reshape(-1)), num_keys=1, is_stable=True)
    first_tile = off // TE
    n_vis = jnp.where(counts > 0, (end - 1) // TE - first_tile + 1, 0)
    v_end = jnp.sum(incl * n_vis[None, :], axis=1)
    v = jnp.arange(N_VISITS, dtype=jnp.int32)
    valid = v < v_end[N_EXPERTS - 1]
    e_v = jnp.minimum(jnp.sum((v[:, None] >= v_end[None, :]).astype(jnp.int32), axis=1), N_EXPERTS - 1)
    pick = (e_v[:, None] == e_iota[None, :]).astype(jnp.int32)
    sel = lambda a: jnp.sum(pick * a[None, :], axis=1)
    tile_v = jnp.where(valid, sel(first_tile) + v - sel(v_end - n_vis), N_SLOTS // TE - 1)
    lo = jnp.where(valid, jnp.clip(sel(off) - tile_v * TE, 0, TE), 0)
    hi = jnp.where(valid, jnp.clip(sel(end) - tile_v * TE, 0, TE), 0)
    prev_tile = jnp.concatenate([jnp.full((1,), -1, jnp.int32), tile_v[:-1]])
    prev_e = jnp.concatenate([jnp.full((1,), -1, jnp.int32), e_v[:-1]])
    first = (valid & (tile_v != prev_tile)).astype(jnp.int32)
    newexp = (valid & (e_v != prev_e)).astype(jnp.int32)
    meta = (tile_v.astype(jnp.int32), e_v.astype(jnp.int32), lo.astype(jnp.int32), hi.astype(jnp.int32), first, newexp)
    return sorted_tok, pos_of_pair, meta


def _combine_ln(x1_ref, ya_ref, yb_ref, r_ref, g_ref, b_ref):
    r = r_ref[...]
    moe = r[:, 2:3] * ya_ref[...] + r[:, 3:4] * yb_ref[...]
    return _layer_norm(ALPHA * x1_ref[...] + moe, g_ref[...], b_ref[...])


def _ln2_kernel(x1_ref, ya_ref, yb_ref, r_ref, g_ref, b_ref, yp_ref, ys_ref):
    i = pl.program_id(0)
    y = _combine_ln(x1_ref, ya_ref, yb_ref, r_ref, g_ref, b_ref)

    @pl.when(i < N_PTILES)
    def _():
        yp_ref[...] = y

    @pl.when(i == N_PTILES)
    def _():
        ys_ref[...] = y


def _ln2(x1, ya, yb, rinfo, g, b):
    tile = pl.BlockSpec((TM, D_MODEL), lambda i: (i, 0))
    rtile = pl.BlockSpec((TM, LANES), lambda i: (i, 0))
    vec = pl.BlockSpec((1, D_MODEL), lambda i: (0, 0))
    return pl.pallas_call(
        _ln2_kernel,
        grid=(N_TILES,),
        in_specs=[tile, tile, tile, rtile, vec, vec],
        out_specs=_TOKEN_SPECS,
        out_shape=[jax.ShapeDtypeStruct((N_PROMPT, D_MODEL), F32), jax.ShapeDtypeStruct((TM, D_MODEL), F32)],
        compiler_params=_cparams("arbitrary"),
        name="ln2",
    )(x1, ya, yb, rinfo, g, b)


def _moe(x1, rinfo, totals, layer, wg, wu, wd, g2, b2):
    sorted_tok, pos_of_pair, meta = _dispatch(rinfo, totals)
    ys = _experts(meta, x1[sorted_tok], layer, wg, wu, wd)
    return _ln2(x1, ys[pos_of_pair[:, 0]], ys[pos_of_pair[:, 1]], rinfo, g2, b2)


LOG2E = math.log2(math.e)
QSCALE = SCALE * LOG2E
AUG_ONE = HEAD_DIM
AUG_CK = HEAD_DIM + 3
L2_ONE = 48
C_GROUP = C_HEADS // C_KV


def _aug_constants():
    perm_k = np.zeros((256 + LANES, C_KV * LANES), np.float32)
    for kv in range(C_KV):
        for d in range(HEAD_DIM):
            perm_k[kv * HEAD_DIM + d, kv * LANES + d] = 1.0
        for p in range(3):
            perm_k[256 + L2_ONE, kv * LANES + AUG_ONE + p] = 1.0
            for g in range(C_GROUP):
                perm_k[256 + 16 * p + kv * C_GROUP + g, kv * LANES + AUG_CK + 4 * p + g] = 1.0
    perm_q = np.zeros((C_HEADS * HEAD_DIM, HEAD_DIM), np.float32)
    for h in range(C_HEADS):
        for p in range(3):
            perm_q[h * HEAD_DIM + p, 16 * p + h] = 1.0
            perm_q[h * HEAD_DIM + 3 + 4 * p + h % C_GROUP, L2_ONE] = -1.0
    return jnp.asarray(perm_k, BF16), jnp.asarray(perm_q, BF16)


def _split3(v):
    hi = v.astype(BF16).astype(F32)
    mid = (v - hi).astype(BF16).astype(F32)
    lo = (v - hi - mid).astype(BF16).astype(F32)
    return hi, mid, lo


def _split_layout(v, lane):
    hi, mid, lo = _split3(jnp.where(lane < C_HEADS, v, 0.0))
    return hi + pltpu.roll(mid, 16, 1) + pltpu.roll(lo, 32, 1)


def _odd_in_kernel(xp_ref, xs_ref, wt_ref, wkf_ref, bf_ref, permk_ref, permq_ref,
                   qa_ref, ka_ref, ktp_ref, vtp_ref, lftp_ref, qts_ref, kts_ref, vts_ref, lfts_ref,
                   carry_ref, r_ref):
    i = pl.program_id(0)
    xb = _token_tile(i, xp_ref, xs_ref).astype(BF16)
    nt_dims = (((1,), (1,)), ((), ()))
    hr = jnp.dot(xb, wkf_ref[...], preferred_element_type=F32)
    lane = lax.broadcasted_iota(jnp.int32, (TM, LANES), 1)
    z = hr[:, 256:384] + bf_ref[...]
    lf = -(jnp.maximum(-z, 0.0) + jnp.log1p(jnp.exp(-jnp.abs(z))))
    lf = jnp.where(lane < C_HEADS, lf, 0.0)
    lft = lf.T[0:C_HEADS, :]

    @pl.when(i < N_PTILES)
    def _():
        @pl.when(i % TILES_PER_SEQ == 0)
        def _():
            carry_ref[...] = jnp.zeros_like(carry_ref)
        ht = lax.dot_general(wt_ref[...], xb, nt_dims, preferred_element_type=F32)
        row = lax.broadcasted_iota(jnp.int32, (TM, TM), 0)
        col = lax.broadcasted_iota(jnp.int32, (TM, TM), 1)
        tri = jnp.where(row >= col, 1.0, 0.0).astype(BF16)
        r_ref[...] = jnp.dot(tri, _split_layout(lf, lane).astype(BF16), preferred_element_type=F32)
        r = r_ref[...]
        c = r + pltpu.roll(r, LANES - 16, 1) + pltpu.roll(r, LANES - 32, 1)
        c = jnp.where(lane < C_HEADS, c, 0.0) + carry_ref[0:1, :]
        carry_ref[0:1, :] = c[TM - 1:TM, :]
        l2 = _split_layout(c * LOG2E, lane) + jnp.where(lane == L2_ONE, 1.0, 0.0)
        kin = jnp.concatenate([hr[:, 0:256], l2], axis=1).astype(BF16)
        ka_ref[...] = jnp.dot(kin, permk_ref[...], preferred_element_type=F32).astype(BF16)
        l2t = l2.T[0:HEAD_DIM, :].astype(BF16)
        qextra = jnp.dot(permq_ref[...], l2t, preferred_element_type=F32)
        for h in range(C_HEADS):
            rows = slice(h * HEAD_DIM, (h + 1) * HEAD_DIM)
            qa_ref[h, 0:HEAD_DIM, :] = (ht[rows, :] * QSCALE).astype(BF16)
            qa_ref[h, HEAD_DIM:2 * HEAD_DIM, :] = qextra[rows, :].astype(BF16)
        ktp_ref[...] = ht[1024:1280, :]
        vtp_ref[...] = ht[1280:1536, :]
        lftp_ref[...] = lft

    @pl.when(i == N_PTILES)
    def _():
        ht = lax.dot_general(wt_ref[...], xb[0:DEC_BATCH, :], nt_dims, preferred_element_type=F32)
        qts_ref[...] = ht[0:1024, :]
        kts_ref[...] = ht[1024:1280, :]
        vts_ref[...] = ht[1280:1536, :]
        lfts_ref[...] = lft[:, 0:DEC_BATCH]


def _odd_in(x, wt_bf, wkf_bf, bf, perm_k, perm_q):
    full = lambda shape: pl.BlockSpec(shape, lambda i: (0,) * len(shape))
    bidx = lambda i: jnp.minimum(i // TILES_PER_SEQ, BATCH - 1)
    tidx = lambda i: jnp.where(i < N_PTILES, i % TILES_PER_SEQ, TILES_PER_SEQ - 1)
    tpose = lambda rows: pl.BlockSpec((None, rows, TM), lambda i: (bidx(i), 0, tidx(i)))
    return pl.pallas_call(
        _odd_in_kernel,
        grid=(N_TILES,),
        in_specs=_TOKEN_SPECS + [full((1536, D_MODEL)), full((D_MODEL, 384)),
                  full((1, LANES)), full((256 + LANES, C_KV * LANES)), full((C_HEADS * HEAD_DIM, HEAD_DIM))],
        out_specs=[pl.BlockSpec((None, C_HEADS, 2 * HEAD_DIM, TM), lambda i: (bidx(i), 0, 0, tidx(i))),
                   pl.BlockSpec((TM, C_KV * LANES), lambda i: (jnp.minimum(i, N_PTILES - 1), 0)),
                   tpose(256), tpose(256), tpose(C_HEADS),
                   full((1024, DEC_BATCH)), full((256, DEC_BATCH)), full((256, DEC_BATCH)),
                   full((C_HEADS, DEC_BATCH))],
        out_shape=[jax.ShapeDtypeStruct((BATCH, C_HEADS, 2 * HEAD_DIM, SEQ), BF16),
                   jax.ShapeDtypeStruct((N_PROMPT, C_KV * LANES), BF16),
                   jax.ShapeDtypeStruct((BATCH, 256, SEQ), F32),
                   jax.ShapeDtypeStruct((BATCH, 256, SEQ), F32),
                   jax.ShapeDtypeStruct((BATCH, C_HEADS, SEQ), F32),
                   jax.ShapeDtypeStruct((1024, DEC_BATCH), F32),
                   jax.ShapeDtypeStruct((256, DEC_BATCH), F32),
                   jax.ShapeDtypeStruct((256, DEC_BATCH), F32),
                   jax.ShapeDtypeStruct((C_HEADS, DEC_BATCH), F32)],
        scratch_shapes=[pltpu.VMEM((8, LANES), F32), pltpu.VMEM((TM, LANES), F32)],
        compiler_params=_cparams("arbitrary"),
        name="odd_in",
    )(*x, wt_bf, wkf_bf, bf, perm_k, perm_q)


FT = 256
ACC_ROWS = HEAD_DIM + 16


def _fox_prompt_kernel(qa_ref, ka_ref, vt_ref, o_ref, *scratch):
    acc_refs = scratch[0:C_GROUP]
    st_refs = scratch[C_GROUP:2 * C_GROUP]
    srow = lax.broadcasted_iota(jnp.int32, (FT, FT), 0)
    tcol = lax.broadcasted_iota(jnp.int32, (FT, FT), 1)

    def q_tile(qi, _):
        t0 = pl.multiple_of(qi * FT, FT)
        qts = [qa_ref[g, :, pl.ds(t0, FT)] for g in range(C_GROUP)]
        for acc_ref in acc_refs:
            acc_ref[...] = jnp.zeros_like(acc_ref)

        def chunk(s0, nk, ms, masked):
            ka = ka_ref[pl.ds(s0, nk), :]
            vt = jnp.concatenate([vt_ref[:, pl.ds(s0, nk)], jnp.ones((ACC_ROWS - HEAD_DIM, nk), F32)],
                                 axis=0).astype(BF16)
            bmax = []
            for g in range(C_GROUP):
                st = jnp.dot(ka, qts[g], preferred_element_type=F32)
                if masked:
                    diag = jnp.where(srow <= tcol, st[nk - FT:nk, :], NEG_INF)
                    st = diag if nk == FT else jnp.concatenate([st[0:nk - FT, :], diag], axis=0)
                st_refs[g][0:nk, :] = st
                bmax.append(jnp.max(st, axis=0, keepdims=True))
            out = []
            for g in range(C_GROUP):
                m_new = jnp.maximum(ms[g], bmax[g])
                a = jnp.exp2(ms[g] - m_new)
                p = jnp.exp2(st_refs[g][0:nk, :] - m_new)
                acc_refs[g][...] = a * acc_refs[g][...] + jnp.dot(vt, p.astype(BF16), preferred_element_type=F32)
                out.append(m_new)
            return tuple(out)

        init = (jnp.full((1, FT), NEG_INF, F32),) * C_GROUP
        n4 = qi // 4
        ms = lax.fori_loop(0, n4, lambda j, c: chunk(pl.multiple_of(j * 4 * FT, 4 * FT), 4 * FT, c, False), init)
        s_tail = pl.multiple_of(n4 * 4 * FT, 4 * FT)

        def tail(nb):
            def run(ms_in):
                chunk(s_tail, nb * FT, ms_in, True)
                return 0
            return run

        lax.switch(qi % 4, [tail(nb) for nb in (1, 2, 3, 4)], ms)
        heads = []
        for g in range(C_GROUP):
            acc = acc_refs[g][...]
            heads.append(acc[0:HEAD_DIM, :] * (1.0 / acc[HEAD_DIM:HEAD_DIM + 1, :]))
        o_ref[pl.ds(t0, FT), :] = jnp.concatenate(heads, axis=0).T
        return 0

    lax.fori_loop(0, SEQ // FT, q_tile, 0)


def _fox_prompt(qa, ka, vtp):
    return pl.pallas_call(
        _fox_prompt_kernel,
        grid=(BATCH, C_KV),
        in_specs=[pl.BlockSpec((None, C_GROUP, 2 * HEAD_DIM, SEQ), lambda b, kv: (b, kv, 0, 0)),
                  pl.BlockSpec((SEQ, LANES), lambda b, kv: (b, kv)),
                  pl.BlockSpec((None, HEAD_DIM, SEQ), lambda b, kv: (b, kv, 0))],
        out_specs=pl.BlockSpec((SEQ, C_GROUP * HEAD_DIM), lambda b, kv: (b, kv)),
        out_shape=jax.ShapeDtypeStruct((N_PROMPT, C_HEADS * HEAD_DIM), F32),
        scratch_shapes=[pltpu.VMEM((ACC_ROWS, FT), F32)] * C_GROUP + [pltpu.VMEM((4 * FT, FT), F32)] * C_GROUP,
        compiler_params=_cparams("arbitrary", "arbitrary"),
        name="fox_prompt",
    )(qa, ka, vtp)


PG = 16
KVD = C_KV * HEAD_DIM


def _fox_sample_kernel(layer, pt_ref, qbd_ref, kts_ref, vts_ref, lfts_ref, kc_hbm, vc_hbm, lfc_hbm,
                       o_ref, kbuf, vbuf, lfbuf, s_ref, off_ref, sem):
    r = pl.program_id(0)
    slot = r % 2
    caches = ((kc_hbm, kbuf), (vc_hbm, vbuf), (lfc_hbm, lfbuf))

    def start_fetch(req, to_slot):
        def body(p, _):
            page = pt_ref[req * N_PAGES + p]
            for t, (src, dst) in enumerate(caches):
                pltpu.make_async_copy(src.at[layer, page], dst.at[to_slot, p], sem.at[t, to_slot]).start()
            return 0
        lax.fori_loop(0, N_PAGES, body, 0, unroll=4)

    @pl.when(r == 0)
    def _():
        start_fetch(0, 0)

    @pl.when(r + 1 < DEC_BATCH)
    def _():
        start_fetch(r + 1, 1 - slot)

    for t, (src, dst) in enumerate(caches):
        pltpu.make_async_copy(src.at[layer, pl.ds(0, N_PAGES)], dst.at[slot], sem.at[t, slot]).wait()

    qbd = qbd_ref[...].astype(BF16)
    rr = lax.broadcasted_iota(jnp.int32, (PAGE, PAGE), 0)
    cc = lax.broadcasted_iota(jnp.int32, (PAGE, PAGE), 1)
    upper = jnp.where(rr <= cc, 1.0, 0.0).astype(BF16)
    n = PG * C_HEADS
    rows = N_PAGES * C_HEADS

    tot = jnp.sum(lfbuf[slot].reshape(rows, PAGE), axis=-1, keepdims=True)
    incl = tot
    shift = C_HEADS
    while shift < rows:
        incl = incl + jnp.concatenate([jnp.zeros((shift, 1), F32), incl[0:rows - shift, :]], axis=0)
        shift *= 2
    off_ref[...] = jnp.broadcast_to(incl - tot, (rows, PAGE))
    total = incl[rows - C_HEADS:rows, :]

    def score_group(j, m):
        p0 = pl.multiple_of(j * PG, PG)
        lf_all = lfbuf[slot, pl.ds(p0, PG)].reshape(n, PAGE)
        parts = jnp.concatenate(_split3(lf_all), axis=0).astype(BF16)
        c3 = jnp.dot(parts, upper, preferred_element_type=F32)
        c_all = c3[0:n, :] + c3[n:2 * n, :] + c3[2 * n:3 * n, :] + off_ref[pl.ds(pl.multiple_of(j * n, n), n), :]
        s_parts = []
        for g in range(PG):
            kt = kbuf[slot, p0 + g].astype(BF16)
            s_parts.append(jnp.dot(qbd, kt, preferred_element_type=F32) - c_all[g * C_HEADS:(g + 1) * C_HEADS, :])
        s = jnp.concatenate(s_parts, axis=1)
        s_ref[:, pl.ds(pl.multiple_of(j * PG * PAGE, PG * PAGE), PG * PAGE)] = s
        return jnp.maximum(m, jnp.max(s, axis=-1, keepdims=True))

    m = lax.fori_loop(0, N_PAGES // PG, score_group, jnp.full((C_HEADS, 1), NEG_INF, F32))
    lane = lax.broadcasted_iota(jnp.int32, (C_HEADS, DEC_BATCH), 1)
    s_new = jnp.dot(qbd, kts_ref[...].astype(BF16), preferred_element_type=F32) - (total + lfts_ref[...])
    s_new = jnp.where(lane == r, s_new, NEG_INF)
    m = jnp.maximum(m, jnp.max(s_new, axis=-1, keepdims=True))

    def value_group(j, c):
        l, acc_t = c
        p0 = pl.multiple_of(j * PG, PG)
        p = jnp.exp(s_ref[:, pl.ds(pl.multiple_of(j * PG * PAGE, PG * PAGE), PG * PAGE)] - m)
        vt = jnp.concatenate([vbuf[slot, p0 + g] for g in range(PG)], axis=1).astype(BF16)
        acc_t = acc_t + jnp.dot(vt, p.T.astype(BF16), preferred_element_type=F32)
        return l + jnp.sum(p, axis=-1, keepdims=True), acc_t

    p_new = jnp.exp(s_new - m)
    init = (jnp.sum(p_new, axis=-1, keepdims=True),
            jnp.dot(vts_ref[...].astype(BF16), p_new.T.astype(BF16), preferred_element_type=F32))
    l, acc_t = lax.fori_loop(0, N_PAGES // PG, value_group, init)
    o_ref[...] = acc_t.T / l


def _fox_sample(layer, page_flat, qbd, kts, vts, lfts, kc, vc, lfc):
    const = lambda shape: pl.BlockSpec(shape, lambda r, pt: (0,) * len(shape))
    hbm = pl.BlockSpec(memory_space=pl.ANY)
    grid_spec = pltpu.PrefetchScalarGridSpec(
        num_scalar_prefetch=1,
        grid=(DEC_BATCH,),
        in_specs=[pl.BlockSpec((None, C_HEADS, KVD), lambda r, pt: (r, 0, 0)),
                  const((KVD, DEC_BATCH)), const((KVD, DEC_BATCH)), const((C_HEADS, DEC_BATCH)), hbm, hbm, hbm],
        out_specs=pl.BlockSpec((None, C_HEADS, KVD), lambda r, pt: (r, 0, 0)),
        scratch_shapes=[pltpu.VMEM((2, N_PAGES, KVD, PAGE), F32), pltpu.VMEM((2, N_PAGES, KVD, PAGE), F32),
                        pltpu.VMEM((2, N_PAGES, C_HEADS, PAGE), F32),
                        pltpu.VMEM((C_HEADS, (N_PAGES + 1) * PAGE), F32),
                        pltpu.VMEM((N_PAGES * C_HEADS, PAGE), F32),
                        pltpu.SemaphoreType.DMA((3, 2))],
    )
    return pl.pallas_call(
        functools.partial(_fox_sample_kernel, layer),
        grid_spec=grid_spec,
        out_shape=jax.ShapeDtypeStruct((DEC_BATCH, C_HEADS, KVD), F32),
        compiler_params=pltpu.CompilerParams(dimension_semantics=("arbitrary",), vmem_limit_bytes=VMEM_LIMIT,
                                             disable_bounds_checks=True),
        name="fox_sample",
    )(page_flat, qbd, kts, vts, lfts, kc, vc, lfc)


def _t5_bucket(dist):
    n = jnp.maximum(dist, 0)
    max_exact = N_BUCKETS // 2
    nf = jnp.maximum(n, 1).astype(F32)
    large = max_exact + (jnp.log(nf / max_exact) / math.log(MAX_DISTANCE / max_exact)
                         * (N_BUCKETS - max_exact)).astype(jnp.int32)
    large = jnp.minimum(large, N_BUCKETS - 1)
    return jnp.where(n < max_exact, n, large)


def kernel(x_prompt, x_sample, cache_swa_k, cache_swa_v, state_conv, cache_fox_k, cache_fox_v, cache_fox_logf, page_table, rel_bias_table, attn_sinks, w_in_even, conv_w, w_out_even, w_in_odd, b_forget, w_out_odd, ln_g, ln_b, w_group, b_group, w_router, b_router, w_gate, w_up, w_down):
    x = (x_prompt.reshape(N_PROMPT, D_MODEL),
         jnp.concatenate([x_sample.reshape(DEC_BATCH, D_MODEL), jnp.zeros((TM - DEC_BATCH, D_MODEL), F32)], axis=0))

    qi = jnp.arange(WINDOW)[:, None]
    kj = jnp.arange(2 * WINDOW)[None, :]
    dist = WINDOW + qi - kj
    band = (dist >= 0) & (dist < WINDOW)
    onehot = (_t5_bucket(dist)[:, :, None] == jnp.arange(N_BUCKETS)[None, None, :]).astype(F32)
    bias_t = jnp.einsum('qkb,bh->hkq', onehot, rel_bias_table, precision=lax.Precision.HIGHEST)
    bias_p = jnp.stack([jnp.where((band & (kj >= WINDOW)).T[None], bias_t, NEG_INF),
                        jnp.where(band.T[None], bias_t, NEG_INF)])
    bias_s = rel_bias_table[_t5_bucket(WINDOW - jnp.arange(WINDOW))].T
    bnew_b = jnp.broadcast_to(rel_bias_table[0][:, None], (A_HEADS, LANES))

    swa_kt = jnp.transpose(cache_swa_k, (0, 1, 3, 4, 2))
    swa_vt = jnp.transpose(cache_swa_v, (0, 1, 3, 4, 2))
    pool = cache_fox_k.shape[1]
    fox_kt = jnp.transpose(cache_fox_k, (0, 1, 3, 4, 2)).reshape(DEPTH // 2, pool, KVD, PAGE)
    fox_vt = jnp.transpose(cache_fox_v, (0, 1, 3, 4, 2)).reshape(DEPTH // 2, pool, KVD, PAGE)
    fox_lft = jnp.transpose(cache_fox_logf, (0, 1, 3, 2))
    page_flat = page_table.reshape(-1)

    wr = jnp.concatenate([w_router, w_group], axis=-1)
    wr_hi = wr.astype(BF16)
    wr_lo = (wr - wr_hi.astype(F32)).astype(BF16)
    zpad = lambda n: jnp.zeros((DEPTH, D_MODEL, n), BF16)
    wr_all = jnp.concatenate([wr_hi, zpad(12), wr_lo, zpad(LANES - 52)], axis=-1)
    br_all = jnp.concatenate([b_router, b_group, jnp.zeros((DEPTH, LANES - 20), F32)], axis=-1)
    eye = jnp.eye(C_KV, dtype=F32)
    perm_k, perm_q = _aug_constants()

    outs = {n: [] for n in ("swa_kp", "swa_vp", "swa_ks", "swa_vs", "conv_p", "conv_s",
                            "fox_kp", "fox_vp", "fox_lp", "fox_ks", "fox_vs", "fox_ls")}
    for layer in range(DEPTH):
        if layer % 2 == 0:
            e = layer // 2
            s0 = state_conv[e, :, 0, :]
            s1 = state_conv[e, :, 1, :]
            q, k, v, cv, convp, us, klt, vlt = _even_in(x, w_in_even[e].astype(BF16), conv_w[e], s0, s1)
            sink_b = jnp.broadcast_to(attn_sinks[e][:, None], (A_HEADS, LANES))
            att_p = _swa_prompt(q, k, v, bias_p, sink_b)
            att_s = _swa_sample(q, k, v, swa_kt[e], swa_vt[e], bias_s, bnew_b, sink_b)
            conv = cv
            w_out = w_out_even[e]
            ks =k[N_PROMPT:N_PROMPT + DEC_BATCH].reshape(DEC_BATCH, 1, A_KV, HEAD_DIM)
            vs = v[N_PROMPT:N_PROMPT + DEC_BATCH].reshape(DEC_BATCH, 1, A_KV, HEAD_DIM)
            last = lambda t: jnp.transpose(t.reshape(BATCH, A_KV, HEAD_DIM, WINDOW), (0, 3, 1, 2))
            outs["swa_kp"].append(last(klt))
            outs["swa_vp"].append(last(vlt))
            outs["swa_ks"].append(jnp.concatenate([cache_swa_k[e][:, 1:], ks], axis=1))
            outs["swa_vs"].append(jnp.concatenate([cache_swa_v[e][:, 1:], vs], axis=1))
            outs["conv_p"].append(convp)
            outs["conv_s"].append(jnp.stack([s1, us], axis=1))
        else:
            o = layer // 2
            wt_bf = jnp.transpose(w_in_odd[o][:, 0:1536]).astype(BF16)
            wkf_bf = jnp.concatenate([w_in_odd[o][:, 1024:1280], w_in_odd[o][:, 1536:1552],
                                      jnp.zeros((D_MODEL, 384 - 272), F32)], axis=-1).astype(BF16)
            bf_pad = jnp.concatenate([b_forget[o], jnp.zeros((LANES - C_HEADS,), F32)])[None, :]
            qa, ka, ktp, vtp, lftp, qts, kts, vts, lfts = _odd_in(x, wt_bf, wkf_bf, bf_pad, perm_k, perm_q)
            att_p = _fox_prompt(qa, ka, vtp)
            qs = qts.T.reshape(DEC_BATCH, C_KV, C_GROUP, 1, HEAD_DIM) * SCALE
            qbd = (qs * eye[None, :, None, :, None]).reshape(DEC_BATCH, C_HEADS, KVD)
            of = _fox_sample(o, page_flat, qbd, kts, vts, lfts, fox_kt, fox_vt, fox_lft)
            of = of.reshape(DEC_BATCH, C_KV, C_GROUP, C_KV, HEAD_DIM)
            att_s = jnp.sum(of * eye[None, :, None, :, None], axis=3).reshape(DEC_BATCH, C_HEADS * HEAD_DIM)
            att_s = jnp.concatenate([att_s, jnp.zeros((TM - DEC_BATCH, C_HEADS * HEAD_DIM), F32)], axis=0)
            conv = None
            w_out = w_out_odd[o]
            outs["fox_kp"].append(jnp.transpose(ktp.reshape(BATCH, C_KV, HEAD_DIM, SEQ), (0, 3, 1, 2)))
            outs["fox_vp"].append(jnp.transpose(vtp.reshape(BATCH, C_KV, HEAD_DIM, SEQ), (0, 3, 1, 2)))
            outs["fox_lp"].append(jnp.transpose(lftp, (0, 2, 1)))
            outs["fox_ks"].append(jnp.transpose(kts.reshape(C_KV, HEAD_DIM, DEC_BATCH), (2, 0, 1))[:, None])
            outs["fox_vs"].append(jnp.transpose(vts.reshape(C_KV, HEAD_DIM, DEC_BATCH), (2, 0, 1))[:, None])
            outs["fox_ls"].append(lfts.T[:, None, :])
        x1, rinfo, totals = _post_mix(att_p, att_s, conv, x, w_out.astype(BF16), ln_g[layer, 0][None, :], ln_b[layer, 0][None, :],
                                      wr_all[layer], br_all[layer][None, :])
        x = _moe(x1, rinfo, totals, layer, w_gate, w_up, w_down,
                 ln_g[layer, 1][None, :], ln_b[layer, 1][None, :])

    st = {n: jnp.stack(vl) for n, vl in outs.items()}
    y_prompt = x[0].reshape(BATCH, SEQ, D_MODEL)
    y_sample = x[1][0:DEC_BATCH].reshape(DEC_BATCH, 1, D_MODEL)
    return (y_prompt, y_sample, st["swa_kp"], st["swa_vp"], st["swa_ks"], st["swa_vs"], st["conv_p"], st["conv_s"],
            st["fox_kp"], st["fox_vp"], st["fox_lp"], st["fox_ks"], st["fox_vs"], st["fox_ls"])
```

```python
import functools
import math

import jax
import jax.numpy as jnp
import numpy as np
from jax import lax
from jax.experimental import pallas as pl
from jax.experimental.pallas import tpu as pltpu

F32 = jnp.float32
BF16 = jnp.bfloat16

D_MODEL = 1024
BATCH = 4
SEQ = 4096
DEC_BATCH = 128
PAGE = 128
N_PAGES = 64
HEAD_DIM = 64
A_HEADS = 8
A_KV = 2
WINDOW = 128
B_WIDTH = 512
C_HEADS = 16
C_KV = 4
N_BUCKETS = 32
MAX_DISTANCE = 128
N_GROUPS = 4
N_EXPERTS = 16
D_EXPERT = 256
DEPTH = 4
ALPHA = (2.0 * DEPTH) ** 0.25
LN_EPS = 1e-5
NEG_INF = -1e30
SCALE = HEAD_DIM ** -0.5

N_PROMPT = BATCH * SEQ
TM = 512
N_PTILES = N_PROMPT // TM
TILES_PER_SEQ = SEQ // TM
N_TILES = N_PTILES + 1
N_TOK = N_TILES * TM
LANES = 128

TE = 512
TE_SUB = 256
N_SLOTS = 2 * N_TOK
N_VISITS = N_SLOTS // TE + N_EXPERTS - 1

VMEM_LIMIT = 56 * 1024 * 1024


def _cparams(*sem):
    return pltpu.CompilerParams(dimension_semantics=sem, vmem_limit_bytes=VMEM_LIMIT)


def _layer_norm(y, g, b):
    mu = jnp.mean(y, axis=-1, keepdims=True)
    yc = y - mu
    var = jnp.mean(yc * yc, axis=-1, keepdims=True)
    return yc * lax.rsqrt(var + LN_EPS) * g + b


_TOKEN_SPECS = [pl.BlockSpec((TM, D_MODEL), lambda i: (jnp.minimum(i, N_PTILES - 1), 0)),
                pl.BlockSpec((TM, D_MODEL), lambda i: (0, 0))]


def _token_tile(i, xp_ref, xs_ref):
    return jnp.where(i < N_PTILES, xp_ref[...], xs_ref[...])


def _even_in_kernel(xp_ref, xs_ref, w_ref, cw_ref, s0_ref, s1_ref,
                    q_ref, k_ref, v_ref, cv_ref, convp_ref, us_ref, klt_ref, vlt_ref, carry_ref):
    i = pl.program_id(0)
    h = jnp.dot(_token_tile(i, xp_ref, xs_ref).astype(BF16), w_ref[...], preferred_element_type=F32)
    q_ref[...] = h[:, 0:512]
    k_ref[...] = h[:, 512:640]
    v_ref[...] = h[:, 640:768]
    bg = h[:, 768:1280]
    u = h[:, 1280:1792] * h[:, 1792:2304]
    w0 = cw_ref[0:1, :]
    w1 = cw_ref[1:2, :]
    w2 = cw_ref[2:3, :]

    @pl.when(i < N_PTILES)
    def _():
        @pl.when(i % TILES_PER_SEQ == 0)
        def _():
            carry_ref[...] = jnp.zeros_like(carry_ref)
        row = lax.broadcasted_iota(jnp.int32, (TM, B_WIDTH), 0)
        c2 = carry_ref[0:1, :]
        c1 = carry_ref[1:2, :]
        u1 = jnp.where(row == 0, c1, pltpu.roll(u, 1, 0))
        u2 = jnp.where(row == 0, c2, jnp.where(row == 1, c1, pltpu.roll(u, 2, 0)))
        cv_ref[...] = bg * (w0 * u2 + w1 * u1 + w2 * u)
        carry_ref[0:2, :] = u[TM - 2:TM, :]
        convp_ref[...] = u[TM - 2:TM, :]

        @pl.when(i % TILES_PER_SEQ == TILES_PER_SEQ - 1)
        def _():
            klt_ref[...] = h[TM - WINDOW:TM, 512:640].T
            vlt_ref[...] = h[TM - WINDOW:TM, 640:768].T

    @pl.when(i == N_PTILES)
    def _():
        us = u[0:DEC_BATCH, :]
        z = w0 * s0_ref[...] + w1 * s1_ref[...] + w2 * us
        cv_ref[0:DEC_BATCH, :] = bg[0:DEC_BATCH, :] * z
        cv_ref[DEC_BATCH:TM, :] = jnp.zeros((TM - DEC_BATCH, B_WIDTH), F32)
        us_ref[...] = us


def _even_in(x, w_bf, cw, s0, s1):
    tile = lambda n: pl.BlockSpec((TM, n), lambda i: (i, 0))
    full = lambda shape: pl.BlockSpec(shape, lambda i: (0,) * len(shape))
    return pl.pallas_call(
        _even_in_kernel,
        grid=(N_TILES,),
        in_specs=_TOKEN_SPECS + [full((D_MODEL, 2304)), full((3, B_WIDTH)),
                  full((DEC_BATCH, B_WIDTH)), full((DEC_BATCH, B_WIDTH))],
        out_specs=[tile(512), tile(128), tile(128), tile(B_WIDTH),
                   pl.BlockSpec((None, 2, B_WIDTH),
                                lambda i: (jnp.minimum(i // TILES_PER_SEQ, BATCH - 1), 0, 0)),
                   full((DEC_BATCH, B_WIDTH)),
                   pl.BlockSpec((None, 128, WINDOW), lambda i: (jnp.minimum(i // TILES_PER_SEQ, BATCH - 1), 0, 0)),
                   pl.BlockSpec((None, 128, WINDOW), lambda i: (jnp.minimum(i // TILES_PER_SEQ, BATCH - 1), 0, 0))],
        out_shape=[jax.ShapeDtypeStruct((N_TOK, 512), F32),
                   jax.ShapeDtypeStruct((N_TOK, 128), F32),
                   jax.ShapeDtypeStruct((N_TOK, 128), F32),
                   jax.ShapeDtypeStruct((N_TOK, B_WIDTH), F32),
                   jax.ShapeDtypeStruct((BATCH, 2, B_WIDTH), F32),
                   jax.ShapeDtypeStruct((DEC_BATCH, B_WIDTH), F32),
                   jax.ShapeDtypeStruct((BATCH, 128, WINDOW), F32),
                   jax.ShapeDtypeStruct((BATCH, 128, WINDOW), F32)],
        scratch_shapes=[pltpu.VMEM((8, B_WIDTH), F32)],
        compiler_params=_cparams("arbitrary"),
        name="even_in",
    )(*x, w_bf, cw, s0, s1)


def _swa_prompt_kernel(q_ref, kc_ref, kp_ref, vc_ref, vp_ref, bias_ref, sink_ref, o_ref, *st_refs):
    qt = (q_ref[...] * SCALE).T
    kk = jnp.concatenate([kp_ref[...], kc_ref[...]], axis=0).astype(BF16)
    vvt = jnp.concatenate([vp_ref[...], vc_ref[...]], axis=0).T
    zeros = jnp.zeros((HEAD_DIM, WINDOW), F32)
    ones = jnp.ones((SWA_ACC_ROWS - HEAD_DIM, 2 * WINDOW), F32)
    cmax = []
    for h in range(A_HEADS):
        kv = h // (A_HEADS // A_KV)
        qh = qt[h * HEAD_DIM:(h + 1) * HEAD_DIM, :]
        qh = jnp.concatenate([qh, zeros] if kv == 0 else [zeros, qh], axis=0).astype(BF16)
        st = jnp.dot(kk, qh, preferred_element_type=F32) + bias_ref[h]
        st_refs[h][...] = st
        cmax.append(jnp.max(st, axis=0, keepdims=True))
    outs = []
    for h in range(A_HEADS):
        kv = h // (A_HEADS // A_KV)
        sk = sink_ref[h:h + 1, :]
        m = jnp.maximum(cmax[h], sk)
        p = jnp.exp(st_refs[h][...] - m)
        vt = jnp.concatenate([vvt[kv * HEAD_DIM:(kv + 1) * HEAD_DIM, :], ones], axis=0).astype(BF16)
        acc = jnp.dot(vt, p.astype(BF16), preferred_element_type=F32)
        den = acc[HEAD_DIM:HEAD_DIM + 1, :] + jnp.exp(sk - m)
        outs.append(acc[0:HEAD_DIM, :] / den)
    o_ref[...] = jnp.concatenate(outs, axis=0).T


def _swa_prompt(q, k, v, bias, sink_b):
    nblk = SEQ // WINDOW
    cur = lambda b, j: (b * nblk + j, 0)
    prev = lambda b, j: (jnp.maximum(b * nblk + j - 1, 0), 0)
    return pl.pallas_call(
        _swa_prompt_kernel,
        grid=(BATCH, nblk),
        in_specs=[pl.BlockSpec((WINDOW, 512), cur),
                  pl.BlockSpec((WINDOW, 128), cur), pl.BlockSpec((WINDOW, 128), prev),
                  pl.BlockSpec((WINDOW, 128), cur), pl.BlockSpec((WINDOW, 128), prev),
                  pl.BlockSpec((None, A_HEADS, 2 * WINDOW, WINDOW), lambda b, j: (jnp.minimum(j, 1), 0, 0, 0)),
                  pl.BlockSpec((A_HEADS, LANES), lambda b, j: (0, 0))],
        out_specs=pl.BlockSpec((WINDOW, 512), cur),
        out_shape=jax.ShapeDtypeStruct((N_PROMPT, 512), F32),
        scratch_shapes=[pltpu.VMEM((2 * WINDOW, WINDOW), F32)] * A_HEADS,
        compiler_params=_cparams("arbitrary", "arbitrary"),
        name="swa_prompt",
    )(q, k, k, v, v, bias, sink_b)


SWA_RB = 8
SWA_ACC_ROWS = HEAD_DIM + 16


def _swa_sample_kernel(q_ref, k_ref, v_ref, kt_ref, vt_ref, bias_ref, bnew_ref, sink_ref, o_ref):
    lane = lax.broadcasted_iota(jnp.int32, (4, WINDOW), 1)
    grp = A_HEADS // A_KV
    results = []
    for r in range(SWA_RB):
        for kv in range(A_KV):
            hs = slice(kv * grp, (kv + 1) * grp)
            q4 = jnp.concatenate(
                [q_ref[r:r + 1, (kv * grp + g) * HEAD_DIM:(kv * grp + g + 1) * HEAD_DIM] for g in range(grp)],
                axis=0)
            kt = kt_ref[r, kv]
            vt = vt_ref[r, kv]
            s_old = jnp.dot(q4.astype(BF16), kt.astype(BF16), preferred_element_type=F32)
            s_old = s_old * SCALE + bias_ref[hs, :]
            s_old = jnp.where(lane >= 1, s_old, NEG_INF)
            kn = k_ref[r:r + 1, kv * HEAD_DIM:(kv + 1) * HEAD_DIM]
            vn = v_ref[r:r + 1, kv * HEAD_DIM:(kv + 1) * HEAD_DIM]
            s_new = jnp.sum(q4 * kn, axis=-1, keepdims=True) * SCALE + bnew_ref[hs, 0:1]
            sk = sink_ref[hs, 0:1]
            m = jnp.maximum(jnp.maximum(jnp.max(s_old, axis=-1, keepdims=True), s_new), sk)
            p_old = jnp.exp(s_old - m)
            p_new = jnp.exp(s_new - m)
            den = jnp.sum(p_old, axis=-1, keepdims=True) + p_new + jnp.exp(sk - m)
            o = lax.dot_general(p_old.astype(BF16), vt.astype(BF16), (((1,), (1,)), ((), ())),
                                preferred_element_type=F32)
            results.append((o + p_new * vn) / den)
    for r in range(SWA_RB):
        for kv in range(A_KV):
            o = results[r * A_KV + kv]
            for g in range(grp):
                hh = kv * grp + g
                o_ref[r:r + 1, hh * HEAD_DIM:(hh + 1) * HEAD_DIM] = o[g:g + 1, :]


def _swa_sample(q, k, v, kt, vt, bias_s, bnew_b, sink_b):
    base = N_PROMPT // SWA_RB
    rows = lambda n: pl.BlockSpec((SWA_RB, n), lambda i: (base + i, 0))
    cache = pl.BlockSpec((SWA_RB, A_KV, HEAD_DIM, WINDOW), lambda i: (i, 0, 0, 0))
    small = pl.BlockSpec((A_HEADS, LANES), lambda i: (0, 0))
    att = pl.pallas_call(
        _swa_sample_kernel,
        grid=(DEC_BATCH // SWA_RB,),
        in_specs=[rows(512), rows(128), rows(128), cache, cache, small, small, small],
        out_specs=pl.BlockSpec((SWA_RB, 512), lambda i: (i, 0)),
        out_shape=jax.ShapeDtypeStruct((DEC_BATCH, 512), F32),
        compiler_params=_cparams("arbitrary"),
        name="swa_sample",
    )(q, k, v, kt, vt, bias_s, bnew_b, sink_b)
    return jnp.concatenate([att, jnp.zeros((TM - DEC_BATCH, 512), F32)], axis=0)


def _route(x1, wr_ref, br_ref, below_ref, cnt_ref):
    x_hi = x1.astype(BF16)
    x_lo = (x1 - x_hi.astype(F32)).astype(BF16)
    pa = jnp.dot(x_hi, wr_ref[...], preferred_element_type=F32)
    pb = jnp.dot(x_lo, wr_ref[...], preferred_element_type=F32)
    logits = pa + pltpu.roll(pa, LANES - 32, 1) + pb + br_ref[...]
    lane = lax.broadcasted_iota(jnp.int32, logits.shape, 1)
    lane_f = lane.astype(F32)
    lane_grp = (lane >> 2).astype(F32)
    is_grp = (lane >= N_EXPERTS) & (lane < N_EXPERTS + N_GROUPS)
    big = 1e9
    gl = jnp.where(is_grp, logits, NEG_INF)
    gmax = jnp.max(gl, axis=-1, keepdims=True)
    gidx = jnp.min(jnp.where(is_grp & (logits == gmax), lane_f - N_EXPERTS, big), axis=-1, keepdims=True)
    gsum = jnp.sum(jnp.where(is_grp, jnp.exp(gl - gmax), 0.0), axis=-1, keepdims=True)
    grp_w = 1.0 / gsum
    in_grp = (lane < N_EXPERTS) & (lane_grp == gidx)
    e1 = jnp.where(in_grp, logits, NEG_INF)
    t1 = jnp.max(e1, axis=-1, keepdims=True)
    i1 = jnp.min(jnp.where(in_grp & (logits == t1), lane_f, big), axis=-1, keepdims=True)
    rest = in_grp & (lane_f != i1)
    e2 = jnp.where(rest, logits, NEG_INF)
    t2 = jnp.max(e2, axis=-1, keepdims=True)
    i2 = jnp.min(jnp.where(rest & (logits == t2), lane_f, big), axis=-1, keepdims=True)
    ex = jnp.exp(t2 - t1)
    g1 = grp_w / (1.0 + ex)
    g2 = grp_w * ex / (1.0 + ex)
    sel1 = lane_f == i1
    sel2 = lane_f == i2
    onehot = jnp.where(sel1 | sel2, 1.0, 0.0)
    before = jnp.dot(below_ref[...], onehot.astype(BF16), preferred_element_type=F32) + cnt_ref[0:1, :]
    rank1 = jnp.sum(jnp.where(sel1, before, 0.0), axis=-1, keepdims=True)
    rank2 = jnp.sum(jnp.where(sel2, before, 0.0), axis=-1, keepdims=True)
    cnt_ref[0:1, :] = cnt_ref[0:1, :] + jnp.sum(onehot, axis=0, keepdims=True)
    out = jnp.where(lane == 0, i1, 0.0)
    out = jnp.where(lane == 1, i2, out)
    out = jnp.where(lane == 2, g1, out)
    out = jnp.where(lane == 3, g2, out)
    out = jnp.where(lane == 4, rank1, out)
    out = jnp.where(lane == 5, rank2, out)
    return out


def _post_mix_kernel(has_conv, *refs):
    n_in = 3 if has_conv else 2
    att_p_ref, att_s_ref = refs[0:2]
    xp_ref, xs_ref, w_ref, g_ref, b_ref, wr_ref, br_ref, below_ref, x1_ref, r_ref, tot_ref, cnt_ref = refs[n_in:]
    i = pl.program_id(0)

    @pl.when(i == 0)
    def _():
        cnt_ref[...] = jnp.zeros_like(cnt_ref)
    att = jnp.where(i < N_PTILES, att_p_ref[...], att_s_ref[...]).astype(BF16)
    kw = att.shape[1]
    mix = jnp.dot(att, w_ref[0:kw, :], preferred_element_type=F32)
    if has_conv:
        mix = mix + jnp.dot(refs[2][...].astype(BF16), w_ref[kw:D_MODEL, :], preferred_element_type=F32)
    x1 = _layer_norm(ALPHA * _token_tile(i, xp_ref, xs_ref) + mix, g_ref[...], b_ref[...])
    x1_ref[...] = x1
    r_ref[...] = _route(x1, wr_ref, br_ref, below_ref, cnt_ref)
    tot_ref[...] = cnt_ref[...]


def _post_mix(att_p, att_s, conv, x, w_bf, g, b, wr_bf, br):
    kw = att_p.shape[1]
    tile = lambda n: pl.BlockSpec((TM, n), lambda i: (i, 0))
    full = lambda shape: pl.BlockSpec(shape, lambda i: (0,) * len(shape))
    mix_specs = [pl.BlockSpec((TM, kw), lambda i: (jnp.minimum(i, N_PTILES - 1), 0)), full((TM, kw))]
    mixes = [att_p, att_s]
    if conv is not None:
        mix_specs.append(tile(D_MODEL - kw))
        mixes.append(conv)
    return pl.pallas_call(
        functools.partial(_post_mix_kernel, conv is not None),
        grid=(N_TILES,),
        in_specs=mix_specs + _TOKEN_SPECS + [full((D_MODEL, D_MODEL)), full((1, D_MODEL)),
                              full((1, D_MODEL)), full((D_MODEL, LANES)), full((1, LANES)), full((TM, TM))],
        out_specs=[tile(D_MODEL), tile(LANES), full((8, LANES))],
        out_shape=[jax.ShapeDtypeStruct((N_TOK, D_MODEL), F32),
                   jax.ShapeDtypeStruct((N_TOK, LANES), F32),
                   jax.ShapeDtypeStruct((8, LANES), F32)],
        scratch_shapes=[pltpu.VMEM((8, LANES), F32)],
        compiler_params=_cparams("arbitrary"),
        name="post_mix",
    )(*mixes, *x, w_bf, g, b, wr_bf, br, jnp.asarray(np.tril(np.ones((TM, TM), np.float32), -1), BF16))


def _expert_kernel(vt_ref, ve_ref, lo_ref, hi_ref, first_ref, newexp_ref, xs_ref, wg_ref, wu_ref, wd_ref, ys_ref,
                   wgub_ref, wdb_ref):
    del vt_ref, ve_ref
    v = pl.program_id(0)
    lo = lo_ref[v]
    hi = hi_ref[v]

    @pl.when((hi > lo) & (newexp_ref[v] == 1))
    def _():
        wgub_ref[:, 0:D_EXPERT] = wg_ref[...].astype(BF16)
        wgub_ref[:, D_EXPERT:2 * D_EXPERT] = wu_ref[...].astype(BF16)
        wdb_ref[...] = wd_ref[...].astype(BF16)

    @pl.when(hi > lo)
    def _():
        row = lax.broadcasted_iota(jnp.int32, (TE, 1), 0)
        mine = jnp.where((row >= lo) & (row < hi), 1.0, 0.0)
        wgu = wgub_ref[...]
        wd = wdb_ref[...]
        parts = []
        for s in range(TE // TE_SUB):
            rows = slice(s * TE_SUB, (s + 1) * TE_SUB)
            gu = jnp.dot(xs_ref[rows, :].astype(BF16), wgu, preferred_element_type=F32)
            g = gu[:, 0:D_EXPERT]
            u = gu[:, D_EXPERT:2 * D_EXPERT]
            h = g * (1.0 / (1.0 + jnp.exp(-g))) * u * mine[rows, :]
            parts.append(jnp.dot(h.astype(BF16), wd, preferred_element_type=F32))
        y = jnp.concatenate(parts, axis=0)

        @pl.when(first_ref[v] == 1)
        def _():
            ys_ref[...] = y

        @pl.when(first_ref[v] == 0)
        def _():
            ys_ref[...] += y


def _experts(meta, xs, layer, wg, wu, wd):
    tile_map = lambda v, vt, ve, lo, hi, fi, ne: (vt[v], 0)
    w_map = lambda v, vt, ve, lo, hi, fi, ne: (layer, ve[v], 0, 0)
    grid_spec = pltpu.PrefetchScalarGridSpec(
        num_scalar_prefetch=6,
        grid=(N_VISITS,),
        in_specs=[pl.BlockSpec((TE, D_MODEL), tile_map),
                  pl.BlockSpec((None, None, D_MODEL, D_EXPERT), w_map),
                  pl.BlockSpec((None, None, D_MODEL, D_EXPERT), w_map),
                  pl.BlockSpec((None, None, D_EXPERT, D_MODEL), w_map)],
        out_specs=pl.BlockSpec((TE, D_MODEL), tile_map),
        scratch_shapes=[pltpu.VMEM((D_MODEL, 2 * D_EXPERT), BF16), pltpu.VMEM((D_EXPERT, D_MODEL), BF16)],
    )
    return pl.pallas_call(
        _expert_kernel,
        grid_spec=grid_spec,
        out_shape=jax.ShapeDtypeStruct((N_SLOTS, D_MODEL), F32),
        compiler_params=_cparams("arbitrary"),
        name="experts",
    )(*meta, xs, wg, wu, wd)


def _dispatch(rinfo, totals):
    ids = rinfo[:, 0:2].astype(jnp.int32)
    ranks = rinfo[:, 4:6].astype(jnp.int32)
    counts = totals[0, 0:N_EXPERTS].astype(jnp.int32)
    e_iota = jnp.arange(N_EXPERTS, dtype=jnp.int32)
    incl = (e_iota[None, :] <= e_iota[:, None]).astype(jnp.int32)
    end = jnp.sum(incl * counts[None, :], axis=1)
    off = end - counts
    pos_of_pair = jnp.sum(jnp.where(e_iota[None, None, :] < ids[:, :, None], counts[None, None, :], 0), axis=2) + ranks
    tok = lax.broadcasted_iota(jnp.int32, (N_TOK, 2), 0)
    _, sorted_tok = lax.sort((ids.reshape(-1), tok.reshape(-1)), num_keys=1, is_stable=True)
    first_tile = off // TE
    n_vis = jnp.where(counts > 0, (end - 1) // TE - first_tile + 1, 0)
    v_end = jnp.sum(incl * n_vis[None, :], axis=1)
    v = jnp.arange(N_VISITS, dtype=jnp.int32)
    valid = v < v_end[N_EXPERTS - 1]
    e_v = jnp.minimum(jnp.sum((v[:, None] >= v_end[None, :]).astype(jnp.int32), axis=1), N_EXPERTS - 1)
    pick = (e_v[:, None] == e_iota[None, :]).astype(jnp.int32)
    sel = lambda a: jnp.sum(pick * a[None, :], axis=1)
    tile_v = jnp.where(valid, sel(first_tile) + v - sel(v_end - n_vis), N_SLOTS // TE - 1)
    lo = jnp.where(valid, jnp.clip(sel(off) - tile_v * TE, 0, TE), 0)
    hi = jnp.where(valid, jnp.clip(sel(end) - tile_v * TE, 0, TE), 0)
    prev_tile = jnp.concatenate([jnp.full((1,), -1, jnp.int32), tile_v[:-1]])
    prev_e = jnp.concatenate([jnp.full((1,), -1, jnp.int32), e_v[:-1]])
    first = (valid & (tile_v != prev_tile)).astype(jnp.int32)
    newexp = (valid & (e_v != prev_e)).astype(jnp.int32)
    meta = (tile_v.astype(jnp.int32), e_v.astype(jnp.int32), lo.astype(jnp.int32), hi.astype(jnp.int32), first, newexp)
    return sorted_tok, pos_of_pair, meta


def _combine_ln(x1_ref, ya_ref, yb_ref, r_ref, g_ref, b_ref):
    r = r_ref[...]
    moe = r[:, 2:3] * ya_ref[...] + r[:, 3:4] * yb_ref[...]
    return _layer_norm(ALPHA * x1_ref[...] + moe, g_ref[...], b_ref[...])


def _ln2_kernel(x1_ref, ya_ref, yb_ref, r_ref, g_ref, b_ref, yp_ref, ys_ref):
    i = pl.program_id(0)
    y = _combine_ln(x1_ref, ya_ref, yb_ref, r_ref, g_ref, b_ref)

    @pl.when(i < N_PTILES)
    def _():
        yp_ref[...] = y

    @pl.when(i == N_PTILES)
    def _():
        ys_ref[...] = y


def _ln2(x1, ya, yb, rinfo, g, b):
    tile = pl.BlockSpec((TM, D_MODEL), lambda i: (i, 0))
    rtile = pl.BlockSpec((TM, LANES), lambda i: (i, 0))
    vec = pl.BlockSpec((1, D_MODEL), lambda i: (0, 0))
    return pl.pallas_call(
        _ln2_kernel,
        grid=(N_TILES,),
        in_specs=[tile, tile, tile, rtile, vec, vec],
        out_specs=_TOKEN_SPECS,
        out_shape=[jax.ShapeDtypeStruct((N_PROMPT, D_MODEL), F32), jax.ShapeDtypeStruct((TM, D_MODEL), F32)],
        compiler_params=_cparams("arbitrary"),
        name="ln2",
    )(x1, ya, yb, rinfo, g, b)


def _moe(x1, rinfo, totals, layer, wg, wu, wd, g2, b2):
    sorted_tok, pos_of_pair, meta = _dispatch(rinfo, totals)
    ys = _experts(meta, x1[sorted_tok], layer, wg, wu, wd)
    return _ln2(x1, ys[pos_of_pair[:, 0]], ys[pos_of_pair[:, 1]], rinfo, g2, b2)


LOG2E = math.log2(math.e)
QSCALE = SCALE * LOG2E
AUG_ONE = HEAD_DIM
AUG_CK = HEAD_DIM + 3
L2_ONE = 48
C_GROUP = C_HEADS // C_KV


def _aug_constants():
    perm_k = np.zeros((256 + LANES, C_KV * LANES), np.float32)
    for kv in range(C_KV):
        for d in range(HEAD_DIM):
            perm_k[kv * HEAD_DIM + d, kv * LANES + d] = 1.0
        for p in range(3):
            perm_k[256 + L2_ONE, kv * LANES + AUG_ONE + p] = 1.0
            for g in range(C_GROUP):
                perm_k[256 + 16 * p + kv * C_GROUP + g, kv * LANES + AUG_CK + 4 * p + g] = 1.0
    perm_q = np.zeros((C_HEADS * HEAD_DIM, HEAD_DIM), np.float32)
    for h in range(C_HEADS):
        for p in range(3):
            perm_q[h * HEAD_DIM + p, 16 * p + h] = 1.0
            perm_q[h * HEAD_DIM + 3 + 4 * p + h % C_GROUP, L2_ONE] = -1.0
    return jnp.asarray(perm_k, BF16), jnp.asarray(perm_q, BF16)


def _split3(v):
    hi = v.astype(BF16).astype(F32)
    mid = (v - hi).astype(BF16).astype(F32)
    lo = (v - hi - mid).astype(BF16).astype(F32)
    return hi, mid, lo


def _split_layout(v, lane):
    hi, mid, lo = _split3(jnp.where(lane < C_HEADS, v, 0.0))
    return hi + pltpu.roll(mid, 16, 1) + pltpu.roll(lo, 32, 1)


def _odd_in_kernel(xp_ref, xs_ref, wt_ref, wkf_ref, bf_ref, permk_ref, permq_ref,
                   qa_ref, ka_ref, ktp_ref, vtp_ref, lftp_ref, qts_ref, kts_ref, vts_ref, lfts_ref,
                   carry_ref, r_ref):
    i = pl.program_id(0)
    xb = _token_tile(i, xp_ref, xs_ref).astype(BF16)
    nt_dims = (((1,), (1,)), ((), ()))
    hr = jnp.dot(xb, wkf_ref[...], preferred_element_type=F32)
    lane = lax.broadcasted_iota(jnp.int32, (TM, LANES), 1)
    z = hr[:, 256:384] + bf_ref[...]
    lf = -(jnp.maximum(-z, 0.0) + jnp.log1p(jnp.exp(-jnp.abs(z))))
    lf = jnp.where(lane < C_HEADS, lf, 0.0)
    lft = lf.T[0:C_HEADS, :]

    @pl.when(i < N_PTILES)
    def _():
        @pl.when(i % TILES_PER_SEQ == 0)
        def _():
            carry_ref[...] = jnp.zeros_like(carry_ref)
        ht = lax.dot_general(wt_ref[...], xb, nt_dims, preferred_element_type=F32)
        row = lax.broadcasted_iota(jnp.int32, (TM, TM), 0)
        col = lax.broadcasted_iota(jnp.int32, (TM, TM), 1)
        tri = jnp.where(row >= col, 1.0, 0.0).astype(BF16)
        r_ref[...] = jnp.dot(tri, _split_layout(lf, lane).astype(BF16), preferred_element_type=F32)
        r = r_ref[...]
        c = r + pltpu.roll(r, LANES - 16, 1) + pltpu.roll(r, LANES - 32, 1)
        c = jnp.where(lane < C_HEADS, c, 0.0) + carry_ref[0:1, :]
        carry_ref[0:1, :] = c[TM - 1:TM, :]
        l2 = _split_layout(c * LOG2E, lane) + jnp.where(lane == L2_ONE, 1.0, 0.0)
        kin = jnp.concatenate([hr[:, 0:256], l2], axis=1).astype(BF16)
        ka_ref[...] = jnp.dot(kin, permk_ref[...], preferred_element_type=F32).astype(BF16)
        l2t = l2.T[0:HEAD_DIM, :].astype(BF16)
        qextra = jnp.dot(permq_ref[...], l2t, preferred_element_type=F32)
        for h in range(C_HEADS):
            rows = slice(h * HEAD_DIM, (h + 1) * HEAD_DIM)
            qa_ref[h, 0:HEAD_DIM, :] = (ht[rows, :] * QSCALE).astype(BF16)
            qa_ref[h, HEAD_DIM:2 * HEAD_DIM, :] = qextra[rows, :].astype(BF16)
        ktp_ref[...] = ht[1024:1280, :]
        vtp_ref[...] = ht[1280:1536, :]
        lftp_ref[...] = lft

    @pl.when(i == N_PTILES)
    def _():
        ht = lax.dot_general(wt_ref[...], xb[0:DEC_BATCH, :], nt_dims, preferred_element_type=F32)
        qts_ref[...] = ht[0:1024, :]
        kts_ref[...] = ht[1024:1280, :]
        vts_ref[...] = ht[1280:1536, :]
        lfts_ref[...] = lft[:, 0:DEC_BATCH]


def _odd_in(x, wt_bf, wkf_bf, bf, perm_k, perm_q):
    full = lambda shape: pl.BlockSpec(shape, lambda i: (0,) * len(shape))
    bidx = lambda i: jnp.minimum(i // TILES_PER_SEQ, BATCH - 1)
    tidx = lambda i: jnp.where(i < N_PTILES, i % TILES_PER_SEQ, TILES_PER_SEQ - 1)
    tpose = lambda rows: pl.BlockSpec((None, rows, TM), lambda i: (bidx(i), 0, tidx(i)))
    return pl.pallas_call(
        _odd_in_kernel,
        grid=(N_TILES,),
        in_specs=_TOKEN_SPECS + [full((1536, D_MODEL)), full((D_MODEL, 384)),
                  full((1, LANES)), full((256 + LANES, C_KV * LANES)), full((C_HEADS * HEAD_DIM, HEAD_DIM))],
        out_specs=[pl.BlockSpec((None, C_HEADS, 2 * HEAD_DIM, TM), lambda i: (bidx(i), 0, 0, tidx(i))),
                   pl.BlockSpec((TM, C_KV * LANES), lambda i: (jnp.minimum(i, N_PTILES - 1), 0)),
                   tpose(256), tpose(256), tpose(C_HEADS),
                   full((1024, DEC_BATCH)), full((256, DEC_BATCH)), full((256, DEC_BATCH)),
                   full((C_HEADS, DEC_BATCH))],
        out_shape=[jax.ShapeDtypeStruct((BATCH, C_HEADS, 2 * HEAD_DIM, SEQ), BF16),
                   jax.ShapeDtypeStruct((N_PROMPT, C_KV * LANES), BF16),
                   jax.ShapeDtypeStruct((BATCH, 256, SEQ), F32),
                   jax.ShapeDtypeStruct((BATCH, 256, SEQ), F32),
                   jax.ShapeDtypeStruct((BATCH, C_HEADS, SEQ), F32),
                   jax.ShapeDtypeStruct((1024, DEC_BATCH), F32),
                   jax.ShapeDtypeStruct((256, DEC_BATCH), F32),
                   jax.ShapeDtypeStruct((256, DEC_BATCH), F32),
                   jax.ShapeDtypeStruct((C_HEADS, DEC_BATCH), F32)],
        scratch_shapes=[pltpu.VMEM((8, LANES), F32), pltpu.VMEM((TM, LANES), F32)],
        compiler_params=_cparams("arbitrary"),
        name="odd_in",
    )(*x, wt_bf, wkf_bf, bf, perm_k, perm_q)


FT = 256
ACC_ROWS = HEAD_DIM + 16


def _fox_prompt_kernel(qa_ref, ka_ref, vt_ref, o_ref, *scratch):
    acc_refs = scratch[0:C_GROUP]
    st_refs = scratch[C_GROUP:2 * C_GROUP]
    srow = lax.broadcasted_iota(jnp.int32, (FT, FT), 0)
    tcol = lax.broadcasted_iota(jnp.int32, (FT, FT), 1)

    def q_tile(qi, _):
        t0 = pl.multiple_of(qi * FT, FT)
        qts = [qa_ref[g, :, pl.ds(t0, FT)] for g in range(C_GROUP)]
        for acc_ref in acc_refs:
            acc_ref[...] = jnp.zeros_like(acc_ref)

        def chunk(s0, nk, ms, masked):
            ka = ka_ref[pl.ds(s0, nk), :]
            vt = jnp.concatenate([vt_ref[:, pl.ds(s0, nk)], jnp.ones((ACC_ROWS - HEAD_DIM, nk), F32)],
                                 axis=0).astype(BF16)
            bmax = []
            for g in range(C_GROUP):
                st = jnp.dot(ka, qts[g], preferred_element_type=F32)
                if masked:
                    diag = jnp.where(srow <= tcol, st[nk - FT:nk, :], NEG_INF)
                    st = diag if nk == FT else jnp.concatenate([st[0:nk - FT, :], diag], axis=0)
                st_refs[g][0:nk, :] = st
                bmax.append(jnp.max(st, axis=0, keepdims=True))
            out = []
            for g in range(C_GROUP):
                m_new = jnp.maximum(ms[g], bmax[g])
                a = jnp.exp2(ms[g] - m_new)
                p = jnp.exp2(st_refs[g][0:nk, :] - m_new)
                acc_refs[g][...] = a * acc_refs[g][...] + jnp.dot(vt, p.astype(BF16), preferred_element_type=F32)
                out.append(m_new)
            return tuple(out)

        init = (jnp.full((1, FT), NEG_INF, F32),) * C_GROUP
        n4 = qi // 4
        ms = lax.fori_loop(0, n4, lambda j, c: chunk(pl.multiple_of(j * 4 * FT, 4 * FT), 4 * FT, c, False), init)
        s_tail = pl.multiple_of(n4 * 4 * FT, 4 * FT)

        def tail(nb):
            def run(ms_in):
                chunk(s_tail, nb * FT, ms_in, True)
                return 0
            return run

        lax.switch(qi % 4, [tail(nb) for nb in (1, 2, 3, 4)], ms)
        heads = []
        for g in range(C_GROUP):
            acc = acc_refs[g][...]
            heads.append(acc[0:HEAD_DIM, :] * (1.0 / acc[HEAD_DIM:HEAD_DIM + 1, :]))
        o_ref[pl.ds(t0, FT), :] = jnp.concatenate(heads, axis=0).T
        return 0

    lax.fori_loop(0, SEQ // FT, q_tile, 0)


def _fox_prompt(qa, ka, vtp):
    return pl.pallas_call(
        _fox_prompt_kernel,
        grid=(BATCH, C_KV),
        in_specs=[pl.BlockSpec((None, C_GROUP, 2 * HEAD_DIM, SEQ), lambda b, kv: (b, kv, 0, 0)),
                  pl.BlockSpec((SEQ, LANES), lambda b, kv: (b, kv)),
                  pl.BlockSpec((None, HEAD_DIM, SEQ), lambda b, kv: (b, kv, 0))],
        out_specs=pl.BlockSpec((SEQ, C_GROUP * HEAD_DIM), lambda b, kv: (b, kv)),
        out_shape=jax.ShapeDtypeStruct((N_PROMPT, C_HEADS * HEAD_DIM), F32),
        scratch_shapes=[pltpu.VMEM((ACC_ROWS, FT), F32)] * C_GROUP + [pltpu.VMEM((4 * FT, FT), F32)] * C_GROUP,
        compiler_params=_cparams("arbitrary", "arbitrary"),
        name="fox_prompt",
    )(qa, ka, vtp)


PG = 32
KVD = C_KV * HEAD_DIM


def _fox_sample_kernel(layer, pt_ref, qbd_ref, kts_ref, vts_ref, lfts_ref, kc_hbm, vc_hbm, lfc_hbm,
                       o_ref, kbuf, vbuf, lfbuf, s_ref, off_ref, sem):
    r = pl.program_id(0)
    slot = r % 2
    caches = ((kc_hbm, kbuf), (vc_hbm, vbuf), (lfc_hbm, lfbuf))

    def start_fetch(req, to_slot):
        def body(p, _):
            page = pt_ref[req * N_PAGES + p]
            for t, (src, dst) in enumerate(caches):
                pltpu.make_async_copy(src.at[layer, page], dst.at[to_slot, p], sem.at[t, to_slot]).start()
            return 0
        lax.fori_loop(0, N_PAGES, body, 0, unroll=4)

    @pl.when(r == 0)
    def _():
        start_fetch(0, 0)

    @pl.when(r + 1 < DEC_BATCH)
    def _():
        start_fetch(r + 1, 1 - slot)

    for t, (src, dst) in enumerate(caches):
        pltpu.make_async_copy(src.at[layer, pl.ds(0, N_PAGES)], dst.at[slot], sem.at[t, slot]).wait()

    qbd = qbd_ref[...].astype(BF16)
    rr = lax.broadcasted_iota(jnp.int32, (PAGE, PAGE), 0)
    cc = lax.broadcasted_iota(jnp.int32, (PAGE, PAGE), 1)
    upper = jnp.where(rr <= cc, 1.0, 0.0).astype(BF16)
    n = PG * C_HEADS
    rows = N_PAGES * C_HEADS

    tot = jnp.sum(lfbuf[slot].reshape(rows, PAGE), axis=-1, keepdims=True)
    incl = tot
    shift = C_HEADS
    while shift < rows:
        incl = incl + jnp.concatenate([jnp.zeros((shift, 1), F32), incl[0:rows - shift, :]], axis=0)
        shift *= 2
    off_ref[...] = jnp.broadcast_to(incl - tot, (rows, PAGE))
    total = incl[rows - C_HEADS:rows, :]

    def score_group(j, m):
        p0 = pl.multiple_of(j * PG, PG)
        lf_all = lfbuf[slot, pl.ds(p0, PG)].reshape(n, PAGE)
        parts = jnp.concatenate(_split3(lf_all), axis=0).astype(BF16)
        c3 = jnp.dot(parts, upper, preferred_element_type=F32)
        c_all = c3[0:n, :] + c3[n:2 * n, :] + c3[2 * n:3 * n, :] + off_ref[pl.ds(pl.multiple_of(j * n, n), n), :]
        s_parts = []
        for g in range(PG):
            kt = kbuf[slot, p0 + g].astype(BF16)
            s_parts.append(jnp.dot(qbd, kt, preferred_element_type=F32) - c_all[g * C_HEADS:(g + 1) * C_HEADS, :])
        s = jnp.concatenate(s_parts, axis=1)
        s_ref[:, pl.ds(pl.multiple_of(j * PG * PAGE, PG * PAGE), PG * PAGE)] = s
        return jnp.maximum(m, jnp.max(s, axis=-1, keepdims=True))

    m = lax.fori_loop(0, N_PAGES // PG, score_group, jnp.full((C_HEADS, 1), NEG_INF, F32))
    lane = lax.broadcasted_iota(jnp.int32, (C_HEADS, DEC_BATCH), 1)
    s_new = jnp.dot(qbd, kts_ref[...].astype(BF16), preferred_element_type=F32) - (total + lfts_ref[...])
    s_new = jnp.where(lane == r, s_new, NEG_INF)
    m = jnp.maximum(m, jnp.max(s_new, axis=-1, keepdims=True))

    def value_group(j, c):
        l, acc_t = c
        p0 = pl.multiple_of(j * PG, PG)
        p = jnp.exp(s_ref[:, pl.ds(pl.multiple_of(j * PG * PAGE, PG * PAGE), PG * PAGE)] - m)
        vt = jnp.concatenate([vbuf[slot, p0 + g] for g in range(PG)], axis=1).astype(BF16)
        acc_t = acc_t + jnp.dot(vt, p.T.astype(BF16), preferred_element_type=F32)
        return l + jnp.sum(p, axis=-1, keepdims=True), acc_t

    p_new = jnp.exp(s_new - m)
    init = (jnp.sum(p_new, axis=-1, keepdims=True),
            jnp.dot(vts_ref[...].astype(BF16), p_new.T.astype(BF16), preferred_element_type=F32))
    l, acc_t = lax.fori_loop(0, N_PAGES // PG, value_group, init)
    o_ref[...] = acc_t.T / l


def _fox_sample(layer, page_flat, qbd, kts, vts, lfts, kc, vc, lfc):
    const = lambda shape: pl.BlockSpec(shape, lambda r, pt: (0,) * len(shape))
    hbm = pl.BlockSpec(memory_space=pl.ANY)
    grid_spec = pltpu.PrefetchScalarGridSpec(
        num_scalar_prefetch=1,
        grid=(DEC_BATCH,),
        in_specs=[pl.BlockSpec((None, C_HEADS, KVD), lambda r, pt: (r, 0, 0)),
                  const((KVD, DEC_BATCH)), const((KVD, DEC_BATCH)), const((C_HEADS, DEC_BATCH)), hbm, hbm, hbm],
        out_specs=pl.BlockSpec((None, C_HEADS, KVD), lambda r, pt: (r, 0, 0)),
        scratch_shapes=[pltpu.VMEM((2, N_PAGES, KVD, PAGE), F32), pltpu.VMEM((2, N_PAGES, KVD, PAGE), F32),
                        pltpu.VMEM((2, N_PAGES, C_HEADS, PAGE), F32),
                        pltpu.VMEM((C_HEADS, (N_PAGES + 1) * PAGE), F32),
                        pltpu.VMEM((N_PAGES * C_HEADS, PAGE), F32),
                        pltpu.SemaphoreType.DMA((3, 2))],
    )
    return pl.pallas_call(
        functools.partial(_fox_sample_kernel, layer),
        grid_spec=grid_spec,
        out_shape=jax.ShapeDtypeStruct((DEC_BATCH, C_HEADS, KVD), F32),
        compiler_params=pltpu.CompilerParams(dimension_semantics=("arbitrary",), vmem_limit_bytes=VMEM_LIMIT,
                                             disable_bounds_checks=True),
        name="fox_sample",
    )(page_flat, qbd, kts, vts, lfts, kc, vc, lfc)


def _t5_bucket(dist):
    n = jnp.maximum(dist, 0)
    max_exact = N_BUCKETS // 2
    nf = jnp.maximum(n, 1).astype(F32)
    large = max_exact + (jnp.log(nf / max_exact) / math.log(MAX_DISTANCE / max_exact)
                         * (N_BUCKETS - max_exact)).astype(jnp.int32)
    large = jnp.minimum(large, N_BUCKETS - 1)
    return jnp.where(n < max_exact, n, large)


def kernel(x_prompt, x_sample, cache_swa_k, cache_swa_v, state_conv, cache_fox_k, cache_fox_v, cache_fox_logf, page_table, rel_bias_table, attn_sinks, w_in_even, conv_w, w_out_even, w_in_odd, b_forget, w_out_odd, ln_g, ln_b, w_group, b_group, w_router, b_router, w_gate, w_up, w_down):
    x = (x_prompt.reshape(N_PROMPT, D_MODEL),
         jnp.concatenate([x_sample.reshape(DEC_BATCH, D_MODEL), jnp.zeros((TM - DEC_BATCH, D_MODEL), F32)], axis=0))

    qi = jnp.arange(WINDOW)[:, None]
    kj = jnp.arange(2 * WINDOW)[None, :]
    dist = WINDOW + qi - kj
    band = (dist >= 0) & (dist < WINDOW)
    onehot = (_t5_bucket(dist)[:, :, None] == jnp.arange(N_BUCKETS)[None, None, :]).astype(F32)
    bias_t = jnp.einsum('qkb,bh->hkq', onehot, rel_bias_table, precision=lax.Precision.HIGHEST)
    bias_p = jnp.stack([jnp.where((band & (kj >= WINDOW)).T[None], bias_t, NEG_INF),
                        jnp.where(band.T[None], bias_t, NEG_INF)])
    bias_s = rel_bias_table[_t5_bucket(WINDOW - jnp.arange(WINDOW))].T
    bnew_b = jnp.broadcast_to(rel_bias_table[0][:, None], (A_HEADS, LANES))

    swa_kt = jnp.transpose(cache_swa_k, (0, 1, 3, 4, 2))
    swa_vt = jnp.transpose(cache_swa_v, (0, 1, 3, 4, 2))
    pool = cache_fox_k.shape[1]
    fox_kt = jnp.transpose(cache_fox_k, (0, 1, 3, 4, 2)).reshape(DEPTH // 2, pool, KVD, PAGE)
    fox_vt = jnp.transpose(cache_fox_v, (0, 1, 3, 4, 2)).reshape(DEPTH // 2, pool, KVD, PAGE)
    fox_lft = jnp.transpose(cache_fox_logf, (0, 1, 3, 2))
    page_flat = page_table.reshape(-1)

    wr = jnp.concatenate([w_router, w_group], axis=-1)
    wr_hi = wr.astype(BF16)
    wr_lo = (wr - wr_hi.astype(F32)).astype(BF16)
    zpad = lambda n: jnp.zeros((DEPTH, D_MODEL, n), BF16)
    wr_all = jnp.concatenate([wr_hi, zpad(12), wr_lo, zpad(LANES - 52)], axis=-1)
    br_all = jnp.concatenate([b_router, b_group, jnp.zeros((DEPTH, LANES - 20), F32)], axis=-1)
    eye = jnp.eye(C_KV, dtype=F32)
    perm_k, perm_q = _aug_constants()

    outs = {n: [] for n in ("swa_kp", "swa_vp", "swa_ks", "swa_vs", "conv_p", "conv_s",
                            "fox_kp", "fox_vp", "fox_lp", "fox_ks", "fox_vs", "fox_ls")}
    for layer in range(DEPTH):
        if layer % 2 == 0:
            e = layer // 2
            s0 = state_conv[e, :, 0, :]
            s1 = state_conv[e, :, 1, :]
            q, k, v, cv, convp, us, klt, vlt = _even_in(x, w_in_even[e].astype(BF16), conv_w[e], s0, s1)
            sink_b = jnp.broadcast_to(attn_sinks[e][:, None], (A_HEADS, LANES))
            att_p = _swa_prompt(q, k, v, bias_p, sink_b)
            att_s = _swa_sample(q, k, v, swa_kt[e], swa_vt[e], bias_s, bnew_b, sink_b)
            conv = cv
            w_out = w_out_even[e]
            ks =k[N_PROMPT:N_PROMPT + DEC_BATCH].reshape(DEC_BATCH, 1, A_KV, HEAD_DIM)
            vs = v[N_PROMPT:N_PROMPT + DEC_BATCH].reshape(DEC_BATCH, 1, A_KV, HEAD_DIM)
            last = lambda t: jnp.transpose(t.reshape(BATCH, A_KV, HEAD_DIM, WINDOW), (0, 3, 1, 2))
            outs["swa_kp"].append(last(klt))
            outs["swa_vp"].append(last(vlt))
            outs["swa_ks"].append(jnp.concatenate([cache_swa_k[e][:, 1:], ks], axis=1))
            outs["swa_vs"].append(jnp.concatenate([cache_swa_v[e][:, 1:], vs], axis=1))
            outs["conv_p"].append(convp)
            outs["conv_s"].append(jnp.stack([s1, us], axis=1))
        else:
            o = layer // 2
            wt_bf = jnp.transpose(w_in_odd[o][:, 0:1536]).astype(BF16)
            wkf_bf = jnp.concatenate([w_in_odd[o][:, 1024:1280], w_in_odd[o][:, 1536:1552],
                                      jnp.zeros((D_MODEL, 384 - 272), F32)], axis=-1).astype(BF16)
            bf_pad = jnp.concatenate([b_forget[o], jnp.zeros((LANES - C_HEADS,), F32)])[None, :]
            qa, ka, ktp, vtp, lftp, qts, kts, vts, lfts = _odd_in(x, wt_bf, wkf_bf, bf_pad, perm_k, perm_q)
            att_p = _fox_prompt(qa, ka, vtp)
            qs = qts.T.reshape(DEC_BATCH, C_KV, C_GROUP, 1, HEAD_DIM) * SCALE
            qbd = (qs * eye[None, :, None, :, None]).reshape(DEC_BATCH, C_HEADS, KVD)
            of = _fox_sample(o, page_flat, qbd, kts, vts, lfts, fox_kt, fox_vt, fox_lft)
            of = of.reshape(DEC_BATCH, C_KV, C_GROUP, C_KV, HEAD_DIM)
            att_s = jnp.sum(of * eye[None, :, None, :, None], axis=3).reshape(DEC_BATCH, C_HEADS * HEAD_DIM)
            att_s = jnp.concatenate([att_s, jnp.zeros((TM - DEC_BATCH, C_HEADS * HEAD_DIM), F32)], axis=0)
            conv = None
            w_out = w_out_odd[o]
            outs["fox_kp"].append(jnp.transpose(ktp.reshape(BATCH, C_KV, HEAD_DIM, SEQ), (0, 3, 1, 2)))
            outs["fox_vp"].append(jnp.transpose(vtp.reshape(BATCH, C_KV, HEAD_DIM, SEQ), (0, 3, 1, 2)))
            outs["fox_lp"].append(jnp.transpose(lftp, (0, 2, 1)))
            outs["fox_ks"].append(jnp.transpose(kts.reshape(C_KV, HEAD_DIM, DEC_BATCH), (2, 0, 1))[:, None])
            outs["fox_vs"].append(jnp.transpose(vts.reshape(C_KV, HEAD_DIM, DEC_BATCH), (2, 0, 1))[:, None])
            outs["fox_ls"].append(lfts.T[:, None, :])
        x1, rinfo, totals = _post_mix(att_p, att_s, conv, x, w_out.astype(BF16), ln_g[layer, 0][None, :], ln_b[layer, 0][None, :],
                                      wr_all[layer], br_all[layer][None, :])
        x = _moe(x1, rinfo, totals, layer, w_gate, w_up, w_down,
                 ln_g[layer, 1][None, :], ln_b[layer, 1][None, :])

    st = {n: jnp.stack(vl) for n, vl in outs.items()}
    y_prompt = x[0].reshape(BATCH, SEQ, D_MODEL)
    y_sample = x[1][0:DEC_BATCH].reshape(DEC_BATCH, 1, D_MODEL)
    return (y_prompt, y_sample, st["swa_kp"], st["swa_vp"], st["swa_ks"], st["swa_vs"], st["conv_p"], st["conv_s"],
            st["fox_kp"], st["fox_vp"], st["fox_lp"], st["fox_ks"], st["fox_vs"], st["fox_ls"])
```
